```python
import jax
import jax.numpy as jnp
from jax import lax
import numpy as np

D_MODEL = 1024
BATCH = 4
SEQ = 4096
DEPTH = 2
DEC_BATCH = 4
DEC_SEQ = 8192
PAST_LEN = 128

GRID_W = 64
ROPE_THETA = 10000.0
Q_BLOCK = 128
EPS = 1e-6
N_MEM = 256
MEM_HEADS = 4
MEM_HD = D_MODEL // MEM_HEADS
GQA_HEADS = 8
GQA_KV_HEADS = 2
GQA_HD = 128
MLA_HEADS = 8
MLA_Q_LORA = 384
MLA_KV_LORA = 256
MLA_NOPE = 64
MLA_ROPE = 32
MLA_V = 128
N_BRANCH = 3
N_GROUPS = 8
EXPERTS_PER_GROUP = 8
N_EXPERTS = N_GROUPS * EXPERTS_PER_GROUP
TOP_K = 2
D_EXPERT = 256
MOE_BLOCK = 128

GQA_Q_W = GQA_HEADS * GQA_HD
GQA_KV_W = GQA_KV_HEADS * GQA_HD
MEM_Q_W = MEM_HEADS * MEM_HD
GATE_W = N_BRANCH * D_MODEL
SPLIT_1 = GQA_Q_W
SPLIT_2 = SPLIT_1 + GQA_KV_W
SPLIT_3 = SPLIT_2 + GQA_KV_W
SPLIT_4 = SPLIT_3 + MLA_Q_LORA
SPLIT_5 = SPLIT_4 + MLA_KV_LORA
SPLIT_6 = SPLIT_5 + MLA_ROPE
SPLIT_7 = SPLIT_6 + MEM_Q_W
IN_W = SPLIT_7 + GATE_W
IN_SPLITS = (SPLIT_1, SPLIT_2, SPLIT_3, SPLIT_4, SPLIT_5, SPLIT_6, SPLIT_7)

kernel_name = "hybrid_gqa_mla_memxattn_hmoe_encoder"


def rmsnorm(x, g):
    xf = x.astype(jnp.float32)
    y = xf * lax.rsqrt(jnp.mean(xf * xf, axis=-1, keepdims=True) + EPS)
    return (y * g.astype(jnp.float32)).astype(x.dtype)


def axial_rope_tables(n_tokens, rot_dim):
    rows = n_tokens // GRID_W
    row_idx = jnp.repeat(jnp.arange(rows, dtype=jnp.float32), GRID_W)
    col_idx = jnp.tile(jnp.arange(GRID_W, dtype=jnp.float32), rows)
    n_freq = rot_dim // 4
    inv_freq = ROPE_THETA ** (-jnp.arange(n_freq, dtype=jnp.float32) / n_freq)
    ang = jnp.concatenate([row_idx[:, None] * inv_freq, col_idx[:, None] * inv_freq], axis=-1)
    return jnp.cos(ang), jnp.sin(ang)


def apply_rope(x, cos, sin):
    half = x.shape[-1] // 2
    xf = x.astype(jnp.float32)
    x1, x2 = xf[..., :half], xf[..., half:]
    c = cos[None, :, None, :]
    s = sin[None, :, None, :]
    return jnp.concatenate([x1 * c - x2 * s, x2 * c + x1 * s], axis=-1).astype(x.dtype)


def block_attention(q, k, v, scale):
    b, s, h, d = q.shape
    hkv = k.shape[2]
    g = h // hkv
    nblk = s // Q_BLOCK
    qb = q.reshape(b, nblk, Q_BLOCK, hkv, g, d).transpose(1, 0, 2, 3, 4, 5)

    def one_block(qi):
        sc = jnp.einsum('bqkgd,btkd->bkgqt', qi, k, preferred_element_type=jnp.float32) * scale
        p = jax.nn.softmax(sc, axis=-1).astype(v.dtype)
        return jnp.einsum('bkgqt,btke->bqkge', p, v)

    out = lax.map(one_block, qb)
    return out.transpose(1, 0, 2, 3, 4, 5).reshape(b, s, h, v.shape[-1])


def mixer_sublayer(h, mem, cos_g, sin_g, cos_m, sin_m, norm_mem, w_in, gqa_q_norm, gqa_k_norm,
                   mla_q_a_norm, mla_kv_a_norm, mla_w_qb, mla_w_kvb, mem_w_kv, w_out):
    b, s, _ = h.shape
    proj = h @ w_in
    q_g, k_g, v_g, q_a, kv_a, k_r, q_c, gates = jnp.split(proj, IN_SPLITS, axis=-1)

    q_g = apply_rope(rmsnorm(q_g.reshape(b, s, GQA_HEADS, GQA_HD), gqa_q_norm), cos_g, sin_g)
    k_g = apply_rope(rmsnorm(k_g.reshape(b, s, GQA_KV_HEADS, GQA_HD), gqa_k_norm), cos_g, sin_g)
    v_g = v_g.reshape(b, s, GQA_KV_HEADS, GQA_HD)
    o_a = block_attention(q_g, k_g, v_g, GQA_HD ** -0.5).reshape(b, s, D_MODEL)

    q = (rmsnorm(q_a, mla_q_a_norm) @ mla_w_qb).reshape(b, s, MLA_HEADS, MLA_NOPE + MLA_ROPE)
    q_rope = apply_rope(q[..., MLA_NOPE:], cos_m, sin_m)
    kv = (rmsnorm(kv_a, mla_kv_a_norm) @ mla_w_kvb).reshape(b, s, MLA_HEADS, MLA_NOPE + MLA_V)
    k_nope, v_m = kv[..., :MLA_NOPE], kv[..., MLA_NOPE:]
    k_rope = apply_rope(k_r[:, :, None, :], cos_m, sin_m)
    q_full = jnp.concatenate([q[..., :MLA_NOPE], q_rope], axis=-1)
    k_full = jnp.concatenate([k_nope, jnp.broadcast_to(k_rope, (b, s, MLA_HEADS, MLA_ROPE))], axis=-1)
    o_b = block_attention(q_full, k_full, v_m, (MLA_NOPE + MLA_ROPE) ** -0.5).reshape(b, s, D_MODEL)

    mkv = rmsnorm(mem, norm_mem) @ mem_w_kv
    mk, mv = jnp.split(mkv, 2, axis=-1)
    n_mem = mem.shape[1]
    mk = mk.reshape(b, n_mem, MEM_HEADS, MEM_HD)
    mv = mv.reshape(b, n_mem, MEM_HEADS, MEM_HD)
    o_c = block_attention(q_c.reshape(b, s, MEM_HEADS, MEM_HD), mk, mv, MEM_HD ** -0.5).reshape(b, s, D_MODEL)

    gt = jax.nn.sigmoid(gates.reshape(b, s, N_BRANCH, D_MODEL).astype(jnp.float32)).astype(h.dtype)
    merged = gt[:, :, 0] * o_a + gt[:, :, 1] * o_b + gt[:, :, 2] * o_c
    return merged @ w_out


def hier_moe(h, w_group, b_group, w_expert, b_expert, w_gate_up, w_down):
    b, s, d = h.shape
    hf = h.reshape(-1, d)
    n = hf.shape[0]
    g_logits = (hf @ w_group).astype(jnp.float32) + b_group.astype(jnp.float32)
    g_prob = jax.nn.softmax(g_logits, axis=-1)
    g_idx = jnp.argmax(g_logits, axis=-1).astype(jnp.int32)
    g_w = jnp.take_along_axis(g_prob, g_idx[:, None], axis=-1)
    e_logits = ((hf @ w_expert).astype(jnp.float32) + b_expert.astype(jnp.float32)).reshape(n, N_GROUPS, EXPERTS_PER_GROUP)
    e_logits = jnp.take_along_axis(e_logits, g_idx[:, None, None], axis=1)[:, 0]
    top_val, top_loc = lax.top_k(e_logits, TOP_K)
    e_w = jax.nn.softmax(top_val, axis=-1) * g_w
    e_idx = g_idx[:, None] * EXPERTS_PER_GROUP + top_loc.astype(jnp.int32)

    m = n * TOP_K
    flat_e = e_idx.reshape(-1)
    order = jnp.argsort(flat_e)
    sorted_e = flat_e[order]
    tok = (order // TOP_K).astype(jnp.int32)
    w_sorted = e_w.reshape(-1)[order]
    sizes = jnp.bincount(flat_e, length=N_EXPERTS).astype(jnp.int32)
    starts = jnp.cumsum(sizes) - sizes
    pad_sizes = (sizes + MOE_BLOCK - 1) // MOE_BLOCK * MOE_BLOCK
    pad_ends = jnp.cumsum(pad_sizes)
    pad_starts = pad_ends - pad_sizes
    dest = pad_starts[sorted_e] + (jnp.arange(m, dtype=jnp.int32) - starts[sorted_e])
    n_blk = (m + MOE_BLOCK - 1) // MOE_BLOCK + N_EXPERTS
    p_len = n_blk * MOE_BLOCK
    buf_tok = jnp.zeros((p_len,), jnp.int32).at[dest].set(tok)
    buf_w = jnp.zeros((p_len,), jnp.float32).at[dest].set(w_sorted)
    blk_start = jnp.arange(n_blk, dtype=jnp.int32) * MOE_BLOCK
    blk_e = jnp.minimum(jnp.searchsorted(pad_ends, blk_start, side='right'), N_EXPERTS - 1).astype(jnp.int32)

    def one_block(args):
        tok_b, w_b, e = args
        xb = hf[tok_b]
        gu = xb @ w_gate_up[e]
        a, u = jnp.split(gu, 2, axis=-1)
        yb = (jax.nn.silu(a) * u) @ w_down[e]
        return yb * w_b[:, None].astype(yb.dtype)

    ys = lax.map(one_block, (buf_tok.reshape(n_blk, MOE_BLOCK), buf_w.reshape(n_blk, MOE_BLOCK), blk_e))
    y = jnp.zeros_like(hf).at[buf_tok].add(ys.reshape(p_len, d))
    return y.reshape(b, s, d)


def trunk(x, mem, norm_mix, norm_mem, w_in, gqa_q_norm, gqa_k_norm, mla_q_a_norm, mla_kv_a_norm,
          mla_w_qb, mla_w_kvb, mem_w_kv, w_out, norm_ffn, w_group, b_group, w_expert, b_expert,
          w_gate_up, w_down, norm_final):
    n_tok = x.shape[1]
    cos_g, sin_g = axial_rope_tables(n_tok, GQA_HD)
    cos_m, sin_m = axial_rope_tables(n_tok, MLA_ROPE)
    for l in range(DEPTH):
        x = x + mixer_sublayer(rmsnorm(x, norm_mix[l]), mem, cos_g, sin_g, cos_m, sin_m, norm_mem[l], w_in[l],
                               gqa_q_norm[l], gqa_k_norm[l], mla_q_a_norm[l], mla_kv_a_norm[l],
                               mla_w_qb[l], mla_w_kvb[l], mem_w_kv[l], w_out[l])
        x = x + hier_moe(rmsnorm(x, norm_ffn[l]), w_group[l], b_group[l], w_expert[l], b_expert[l],
                         w_gate_up[l], w_down[l])
    return rmsnorm(x, norm_final)


def setup_inputs(seed: int = 0) -> dict:
    key = jax.random.key(seed)
    ks = jax.random.split(key, 24)
    f32 = jnp.float32

    def nrm(k, shape, scale):
        return jax.random.normal(k, shape, f32) * scale

    def gain(k, shape):
        return 1.0 + 0.02 * jax.random.normal(k, shape, f32)

    return {
        'x_prompt': nrm(ks[0], (BATCH, SEQ, D_MODEL), 1.0),
        'x_sample': nrm(ks[1], (DEC_BATCH, DEC_SEQ, D_MODEL), 1.0),
        'mem_prompt': nrm(ks[2], (BATCH, N_MEM, D_MODEL), 1.0),
        'mem_sample': nrm(ks[3], (DEC_BATCH, N_MEM, D_MODEL), 1.0),
        'norm_mix': gain(ks[4], (DEPTH, D_MODEL)),
        'norm_mem': gain(ks[5], (DEPTH, D_MODEL)),
        'w_in': nrm(ks[6], (DEPTH, D_MODEL, IN_W), D_MODEL ** -0.5),
        'gqa_q_norm': gain(ks[7], (DEPTH, GQA_HD)),
        'gqa_k_norm': gain(ks[8], (DEPTH, GQA_HD)),
        'mla_q_a_norm': gain(ks[9], (DEPTH, MLA_Q_LORA)),
        'mla_kv_a_norm': gain(ks[10], (DEPTH, MLA_KV_LORA)),
        'mla_w_qb': nrm(ks[11], (DEPTH, MLA_Q_LORA, MLA_HEADS * (MLA_NOPE + MLA_ROPE)), MLA_Q_LORA ** -0.5),
        'mla_w_kvb': nrm(ks[12], (DEPTH, MLA_KV_LORA, MLA_HEADS * (MLA_NOPE + MLA_V)), MLA_KV_LORA ** -0.5),
        'mem_w_kv': nrm(ks[13], (DEPTH, D_MODEL, 2 * MEM_Q_W), D_MODEL ** -0.5),
        'w_out': nrm(ks[14], (DEPTH, D_MODEL, D_MODEL), D_MODEL ** -0.5),
        'norm_ffn': gain(ks[15], (DEPTH, D_MODEL)),
        'w_group': nrm(ks[16], (DEPTH, D_MODEL, N_GROUPS), D_MODEL ** -0.5),
        'b_group': nrm(ks[17], (DEPTH, N_GROUPS), 0.01),
        'w_expert': nrm(ks[18], (DEPTH, D_MODEL, N_EXPERTS), D_MODEL ** -0.5),
        'b_expert': nrm(ks[19], (DEPTH, N_EXPERTS), 0.01),
        'w_gate_up': nrm(ks[20], (DEPTH, N_EXPERTS, D_MODEL, 2 * D_EXPERT), D_MODEL ** -0.5),
        'w_down': nrm(ks[21], (DEPTH, N_EXPERTS, D_EXPERT, D_MODEL), D_EXPERT ** -0.5),
        'norm_final': gain(ks[22], (D_MODEL,)),
    }


def reference(x_prompt, x_sample, mem_prompt, mem_sample, norm_mix, norm_mem, w_in, gqa_q_norm, gqa_k_norm,
              mla_q_a_norm, mla_kv_a_norm, mla_w_qb, mla_w_kvb, mem_w_kv, w_out, norm_ffn, w_group, b_group,
              w_expert, b_expert, w_gate_up, w_down, norm_final):
    y_prompt = trunk(x_prompt, mem_prompt, norm_mix, norm_mem, w_in, gqa_q_norm, gqa_k_norm, mla_q_a_norm,
                     mla_kv_a_norm, mla_w_qb, mla_w_kvb, mem_w_kv, w_out, norm_ffn, w_group, b_group,
                     w_expert, b_expert, w_gate_up, w_down, norm_final)
    y_sample = trunk(x_sample, mem_sample, norm_mix, norm_mem, w_in, gqa_q_norm, gqa_k_norm, mla_q_a_norm,
                     mla_kv_a_norm, mla_w_qb, mla_w_kvb, mem_w_kv, w_out, norm_ffn, w_group, b_group,
                     w_expert, b_expert, w_gate_up, w_down, norm_final)
    return (y_prompt, y_sample)
```

```python
import functools
import math

import jax
import jax.numpy as jnp
from jax import lax
from jax.experimental import pallas as pl
from jax.experimental.pallas import tpu as pltpu

F32 = jnp.float32
BF16 = jnp.bfloat16

D_MODEL = 1024
EPS = 1e-6
GRID_W = 64
ROPE_THETA = 10000.0
GQA_HEADS, GQA_KV_HEADS, GQA_HD = 8, 2, 128
MLA_HEADS, MLA_Q_LORA, MLA_KV_LORA = 8, 384, 256
MLA_NOPE, MLA_ROPE, MLA_V = 64, 32, 128
MEM_HEADS, MEM_HD = 4, 256
N_BRANCH = 3
N_GROUPS, EXPERTS_PER_GROUP, D_EXPERT = 8, 8, 256
N_EXPERTS = N_GROUPS * EXPERTS_PER_GROUP
HEAD_PAD = 128
LOG2E = 1.4426950408889634

TOK_TILE = 512
ATT_Q_COLS = 2048
ATT_KV_BLOCK = 2048
MOE_ROWS = 256
ROUTER_LANES = 128
VMEM_LIMIT = 56 * 1024 * 1024


def _cparams(n_axes):
    return pltpu.CompilerParams(dimension_semantics=("arbitrary",) * n_axes, vmem_limit_bytes=VMEM_LIMIT)


def _nt(a, b):
    return lax.dot_general(a, b, (((1,), (1,)), ((), ())), preferred_element_type=F32)


def _rms_rows(x, g):
    return x * lax.rsqrt(jnp.mean(x * x, axis=-1, keepdims=True) + EPS) * g


def _rms_cols(x, g):
    return x * lax.rsqrt(jnp.mean(x * x, axis=0, keepdims=True) + EPS) * g


def _const_spec(shape):
    return pl.BlockSpec(shape, lambda *_: (0,) * len(shape))


def _proj_gqa_kernel(x_ref, g_ref, wqT_ref, wk_ref, wvT_ref, gq_ref, gk_ref, cT_ref, sT_ref, cF_ref, sF_ref,
                     q_ref, k_ref, v_ref):
    h = _rms_rows(x_ref[0], g_ref[...]).astype(BF16)
    qT = _nt(wqT_ref[...], h)
    c, s, gq = cT_ref[...], sT_ref[...], gq_ref[...]
    half = GQA_HD // 2
    scale = GQA_HD ** -0.5 * LOG2E
    for hd in range(GQA_HEADS):
        qn = _rms_cols(qT[hd * GQA_HD:(hd + 1) * GQA_HD], gq)
        x1, x2 = qn[:half], qn[half:]
        o = jnp.concatenate([x1 * c - x2 * s, x2 * c + x1 * s], axis=0) * scale
        q_ref[0, hd] = o.astype(BF16)
    k = jnp.dot(h, wk_ref[...], preferred_element_type=F32)
    for j in range(GQA_KV_HEADS):
        kn = _rms_rows(k[:, j * GQA_HD:(j + 1) * GQA_HD], gk_ref[...])
        ko = kn * cF_ref[...] + pltpu.roll(kn, half, 1) * sF_ref[...]
        k_ref[0, j] = ko.astype(BF16)
    vT = _nt(wvT_ref[...], h)
    for j in range(GQA_KV_HEADS):
        v_ref[0, j, 0] = vT[j * GQA_HD:(j + 1) * GQA_HD].astype(BF16)


def _proj_mla_kernel(x_ref, g_ref, wqaT_ref, gqa_ref, wqbT_ref, wkva_ref, gkva_ref, wkn_ref, wkr_ref, wvT_ref,
                     cT_ref, sT_ref, cP_ref, sP_ref, q_ref, k_ref, v_ref):
    h = _rms_rows(x_ref[0], g_ref[...]).astype(BF16)
    qa = _rms_cols(_nt(wqaT_ref[...], h), gqa_ref[...]).astype(BF16)
    qT = jnp.dot(wqbT_ref[...], qa, preferred_element_type=F32)
    c, s = cT_ref[...], sT_ref[...]
    hr = MLA_ROPE // 2
    scale = (MLA_NOPE + MLA_ROPE) ** -0.5 * LOG2E
    for hd in range(MLA_HEADS):
        q = qT[hd * HEAD_PAD:(hd + 1) * HEAD_PAD]
        x1, x2 = q[MLA_NOPE:MLA_NOPE + hr], q[MLA_NOPE + hr:MLA_NOPE + 2 * hr]
        o = jnp.concatenate([q[:MLA_NOPE], x1 * c - x2 * s, x2 * c + x1 * s, q[MLA_NOPE + 2 * hr:]], axis=0) * scale
        q_ref[0, hd] = o.astype(BF16)
    kva = jnp.dot(h, wkva_ref[...], preferred_element_type=F32)
    kvn = _rms_rows(kva, gkva_ref[...]).astype(BF16)
    kr2 = jnp.dot(h, wkr_ref[...], preferred_element_type=F32)
    kr = kr2[:, :HEAD_PAD] * cP_ref[...] + kr2[:, HEAD_PAD:] * sP_ref[...]
    kn = jnp.dot(kvn, wkn_ref[...], preferred_element_type=F32)
    for hd in range(MLA_HEADS):
        k_ref[0, hd] = (kn[:, hd * HEAD_PAD:(hd + 1) * HEAD_PAD] + kr).astype(BF16)
    vT = _nt(wvT_ref[...], kvn)
    for hd in range(MLA_HEADS):
        v_ref[0, hd, 0] = vT[hd * MLA_V:(hd + 1) * MLA_V].astype(BF16)


def _proj_misc_kernel(x_ref, g_ref, wT_ref, qc_ref, gate_ref):
    h = _rms_rows(x_ref[0], g_ref[...]).astype(BF16)
    scale = MEM_HD ** -0.5 * LOG2E
    rows = 512
    n_q = D_MODEL // rows
    for ci in range((D_MODEL + N_BRANCH * D_MODEL) // rows):
        y = _nt(wT_ref[ci * rows:(ci + 1) * rows, :], h)
        if ci < n_q:
            qc_ref[0, ci * rows:(ci + 1) * rows, :] = (y * scale).astype(BF16)
        else:
            r0 = (ci - n_q) * rows
            gate_ref[0, r0:r0 + rows, :] = jax.nn.sigmoid(y).astype(BF16)


def _mem_kv_kernel(mem_ref, g_ref, wk_ref, wvT_ref, mk_ref, mvT_ref):
    h = _rms_rows(mem_ref[0], g_ref[...]).astype(BF16)
    mk_ref[0] = jnp.dot(h, wk_ref[...], preferred_element_type=F32).astype(BF16)
    mvT_ref[0] = _nt(wvT_ref[...], h).astype(BF16)


def _attn_kernel(q_ref, k_ref, v_ref, o_ref, m_sc, l_sc, acc_sc, *, heads_per_step, cols_per_head, n_chunks):
    chunk = TOK_TILE
    subs = [(g, j * chunk) for g in range(heads_per_step) for j in range(cols_per_head // chunk)]
    kv_step = pl.program_id(3)

    @pl.when(kv_step == 0)
    def _():
        m_sc[...] = jnp.full(m_sc.shape, -jnp.inf, F32)
        l_sc[...] = jnp.zeros(l_sc.shape, F32)
        acc_sc[...] = jnp.zeros(acc_sc.shape, F32)

    def body(ci, carry):
        off = pl.multiple_of(ci * chunk, chunk)
        k = k_ref[0, 0, pl.ds(off, chunk), :]
        vT = v_ref[0, 0, ci]
        for u, (g, c0) in enumerate(subs):
            qT = q_ref[0, g, :, c0:c0 + chunk]
            s = jnp.dot(k, qT, preferred_element_type=F32)
            m_prev = m_sc[u]
            m_cur = jnp.maximum(m_prev, jnp.max(s, axis=0, keepdims=True))
            alpha = jnp.exp2(m_prev - m_cur)
            p = jnp.exp2(s - m_cur)
            l_sc[u] = alpha * l_sc[u] + jnp.sum(p, axis=0, keepdims=True)
            acc_sc[u] = alpha * acc_sc[u] + jnp.dot(vT, p.astype(BF16), preferred_element_type=F32)
            m_sc[u] = m_cur
        return carry

    lax.fori_loop(0, n_chunks, body, 0)

    @pl.when(kv_step == pl.num_programs(3) - 1)
    def _():
        for u, (g, c0) in enumerate(subs):
            o_ref[0, g, :, c0:c0 + chunk] = (acc_sc[u] / l_sc[u]).astype(BF16)


def _attention(qT, k, vT, group):
    b, h, hd, s = qT.shape
    hkv = k.shape[1]
    assert h == hkv * group
    cols_per_head = min(ATT_Q_COLS // group, s)
    kv_block = min(ATT_KV_BLOCK, s)
    n_chunks = kv_block // TOK_TILE
    n_sub = group * (cols_per_head // TOK_TILE)
    kern = functools.partial(_attn_kernel, heads_per_step=group, cols_per_head=cols_per_head, n_chunks=n_chunks)
    return pl.pallas_call(
        kern,
        grid=(b, hkv, s // cols_per_head, s // kv_block),
        in_specs=[
            pl.BlockSpec((1, group, hd, cols_per_head), lambda bi, j, qi, ki: (bi, j, 0, qi)),
            pl.BlockSpec((1, 1, kv_block, hd), lambda bi, j, qi, ki: (bi, j, ki, 0)),
            pl.BlockSpec((1, 1, n_chunks, hd, TOK_TILE), lambda bi, j, qi, ki: (bi, j, ki, 0, 0)),
        ],
        out_specs=pl.BlockSpec((1, group, hd, cols_per_head), lambda bi, j, qi, ki: (bi, j, 0, qi)),
        out_shape=jax.ShapeDtypeStruct(qT.shape, BF16),
        scratch_shapes=[
            pltpu.VMEM((n_sub, 1, TOK_TILE), F32),
            pltpu.VMEM((n_sub, 1, TOK_TILE), F32),
            pltpu.VMEM((n_sub, hd, TOK_TILE), F32),
        ],
        compiler_params=_cparams(4),
        name="attention",
    )(qT, k, vT)


def _mem_attn_kernel(qc_ref, mk_ref, mvT_ref, o_ref):
    for hd in range(MEM_HEADS):
        r = slice(hd * MEM_HD, (hd + 1) * MEM_HD)
        s = jnp.dot(mk_ref[0, :, r], qc_ref[0, r, :], preferred_element_type=F32)
        p = jnp.exp2(s - jnp.max(s, axis=0, keepdims=True))
        l = jnp.sum(p, axis=0, keepdims=True)
        o = jnp.dot(mvT_ref[0, r, :], p.astype(BF16), preferred_element_type=F32)
        o_ref[0, r, :] = (o / l).astype(BF16)


def _merge_kernel(oa_ref, ob_ref, oc_ref, gate_ref, x_ref, w_ref, y_ref):
    merged = (gate_ref[0, 0:D_MODEL, :].astype(F32) * oa_ref[0].astype(F32)
              + gate_ref[0, D_MODEL:2 * D_MODEL, :].astype(F32) * ob_ref[0].astype(F32)
              + gate_ref[0, 2 * D_MODEL:, :].astype(F32) * oc_ref[0].astype(F32)).astype(BF16)
    out = lax.dot_general(merged, w_ref[...], (((0,), (0,)), ((), ())), preferred_element_type=F32)
    y_ref[0] = x_ref[0] + out


def _first_index_of_max(v, vmax, iota, n):
    return jnp.min(jnp.where(v == vmax, iota, n), axis=0, keepdims=True)


def _router_kernel(x_ref, g_ref, wT_ref, b_ref, h_ref, ri_ref, rw_ref, cnt_ref, carry_sc):
    step = pl.program_id(0)

    @pl.when(step == 0)
    def _():
        carry_sc[...] = jnp.zeros(carry_sc.shape, F32)

    h = _rms_rows(x_ref[...], g_ref[...])
    h_ref[...] = h
    t = h.shape[0]
    logits = lax.dot_general(wT_ref[...], h, (((1,), (1,)), ((), ())), precision=lax.Precision.HIGHEST,
                             preferred_element_type=F32) + b_ref[...]
    iota_g = lax.broadcasted_iota(jnp.int32, (N_GROUPS, t), 0)
    gl = logits[0:N_GROUPS]
    gmax = jnp.max(gl, axis=0, keepdims=True)
    gidx = _first_index_of_max(gl, gmax, iota_g, N_GROUPS)
    gw = 1.0 / jnp.sum(jnp.exp(gl - gmax), axis=0, keepdims=True)
    sel = jnp.zeros((EXPERTS_PER_GROUP, t), F32)
    for g in range(N_GROUPS):
        r0 = N_GROUPS + g * EXPERTS_PER_GROUP
        sel = sel + jnp.where(gidx == g, logits[r0:r0 + EXPERTS_PER_GROUP], 0.0)
    v1 = jnp.max(sel, axis=0, keepdims=True)
    i1 = _first_index_of_max(sel, v1, iota_g, EXPERTS_PER_GROUP)
    sel2 = jnp.where(iota_g == i1, -jnp.inf, sel)
    v2 = jnp.max(sel2, axis=0, keepdims=True)
    i2 = _first_index_of_max(sel2, v2, iota_g, EXPERTS_PER_GROUP)
    e = jnp.exp(v2 - v1)
    w1 = gw / (1.0 + e)
    w2 = gw * e / (1.0 + e)
    e1 = gidx * EXPERTS_PER_GROUP + i1
    e2 = gidx * EXPERTS_PER_GROUP + i2

    iota_e = lax.broadcasted_iota(jnp.int32, (N_EXPERTS, t), 0)
    hit1 = iota_e == e1
    hit2 = iota_e == e2
    hits = jnp.where(hit1 | hit2, 1.0, 0.0)
    before = (lax.broadcasted_iota(jnp.int32, (t, t), 0) < lax.broadcasted_iota(jnp.int32, (t, t), 1))
    prefix = jnp.dot(hits.astype(BF16), jnp.where(before, 1.0, 0.0).astype(BF16),
                     preferred_element_type=F32) + carry_sc[...]
    r1 = jnp.sum(jnp.where(hit1, prefix, 0.0), axis=0, keepdims=True)
    r2 = jnp.sum(jnp.where(hit2, prefix, 0.0), axis=0, keepdims=True)
    carry_sc[...] = carry_sc[...] + jnp.sum(hits, axis=1, keepdims=True)

    ri_ref[...] = jnp.zeros(ri_ref.shape, jnp.int32)
    ri_ref[0:1, :] = e1
    ri_ref[1:2, :] = e2
    ri_ref[2:3, :] = r1.astype(jnp.int32)
    ri_ref[3:4, :] = r2.astype(jnp.int32)
    rw_ref[...] = jnp.zeros(rw_ref.shape, F32)
    rw_ref[0:1, :] = w1
    rw_ref[1:2, :] = w2
    cnt_ref[...] = jnp.broadcast_to(carry_sc[...], cnt_ref.shape)


def _dest_kernel(cnt_ref, ri_ref, dest_ref, blk_ref, nused_ref, *, n_blocks_padded):
    cnt = cnt_ref[...]
    pad = jnp.floor((cnt + (MOE_ROWS - 1)) * (1.0 / MOE_ROWS)) * MOE_ROWS
    lower = jnp.where(lax.broadcasted_iota(jnp.int32, (N_EXPERTS, N_EXPERTS), 1)
                      <= lax.broadcasted_iota(jnp.int32, (N_EXPERTS, N_EXPERTS), 0), 1.0, 0.0)
    pad_end = jnp.dot(lower, pad, precision=lax.Precision.HIGHEST, preferred_element_type=F32)
    pad_start = (pad_end - pad)[:, 0:1]
    t = ri_ref.shape[1]
    iota_e = lax.broadcasted_iota(jnp.int32, (N_EXPERTS, t), 0)
    dest_ref[...] = jnp.zeros(dest_ref.shape, jnp.int32)
    for kk in range(2):
        start = jnp.sum(jnp.where(iota_e == ri_ref[kk:kk + 1, :], pad_start, 0.0), axis=0, keepdims=True)
        dest_ref[kk:kk + 1, :] = start.astype(jnp.int32) + ri_ref[2 + kk:3 + kk, :]
    blk_start = (lax.broadcasted_iota(jnp.int32, (N_EXPERTS, n_blocks_padded), 1) * MOE_ROWS).astype(F32)
    owner = jnp.sum(jnp.where(pad_end[:, 0:1] <= blk_start, 1.0, 0.0), axis=0, keepdims=True)
    blk_ref[...] = jnp.broadcast_to(jnp.minimum(owner, N_EXPERTS - 1.0).astype(jnp.int32), blk_ref.shape)
    nused_ref[...] = jnp.broadcast_to((pad_end[N_EXPERTS - 1:N_EXPERTS, :] * (1.0 / MOE_ROWS)).astype(jnp.int32),
                                      nused_ref.shape)


def _row_copy(src_ref, src_row, dst_ref, dst_row, sem):
    return pltpu.make_async_copy(src_ref.at[pl.ds(src_row, 1)], dst_ref.at[pl.ds(dst_row, 1)], sem)


def _dispatch_kernel(dest_ref, h_ref, xs_in_ref, xs_ref, sem, *, tile):
    del xs_in_ref
    base = pl.program_id(0) * tile

    def issue(t, carry):
        _row_copy(h_ref, base + t, xs_ref, dest_ref[0, 0, t], sem).start()
        _row_copy(h_ref, base + t, xs_ref, dest_ref[0, 0, tile + t], sem).start()
        return carry

    def drain(t, carry):
        _row_copy(h_ref, 0, xs_ref, 0, sem).wait()
        _row_copy(h_ref, 0, xs_ref, 0, sem).wait()
        return carry

    lax.fori_loop(0, tile, issue, 0)
    lax.fori_loop(0, tile, drain, 0)


def _expert_kernel(blk_ref, nused_ref, xs_ref, wgu_ref, wd_ref, ys_ref):
    b = pl.program_id(0)

    @pl.when(b < nused_ref[0])
    def _():
        gu = jnp.dot(xs_ref[...].astype(BF16), wgu_ref[0], preferred_element_type=F32)
        a, u = gu[:, :D_EXPERT], gu[:, D_EXPERT:]
        hidden = (a * jax.nn.sigmoid(a) * u).astype(BF16)
        ys_ref[...] = jnp.dot(hidden, wd_ref[0], preferred_element_type=F32)

    @pl.when(b >= nused_ref[0])
    def _():
        ys_ref[...] = jnp.zeros(ys_ref.shape, F32)


def _combine_kernel(dest_ref, ys_ref, x_ref, w_ref, g_ref, y_ref, buf, sem, *, tile, final_norm):
    def issue(t, carry):
        _row_copy(ys_ref, dest_ref[0, 0, t], buf.at[0], t, sem).start()
        _row_copy(ys_ref, dest_ref[0, 0, tile + t], buf.at[1], t, sem).start()
        return carry

    def drain(t, carry):
        _row_copy(ys_ref, 0, buf.at[0], 0, sem).wait()
        _row_copy(ys_ref, 0, buf.at[1], 0, sem).wait()
        return carry

    lax.fori_loop(0, tile, issue, 0)
    lax.fori_loop(0, tile, drain, 0)
    w = w_ref[...]
    y = x_ref[...] + (w[:, 0:1] * buf[0] + w[:, 1:2] * buf[1])
    if final_norm:
        y = _rms_rows(y, g_ref[...])
    y_ref[...] = y


def _moe(x, g_ffn, wrT, br, wgu, wd, g_final, final_norm):
    n_tok = x.shape[0]
    tile = min(TOK_TILE, n_tok)
    n_tiles = n_tok // tile
    n_blocks = (2 * n_tok + MOE_ROWS - 1) // MOE_ROWS + N_EXPERTS
    n_blocks_padded = (n_blocks + 127) // 128 * 128
    n_rows = n_blocks * MOE_ROWS

    h, ri, rw, cnt = pl.pallas_call(
        _router_kernel,
        grid=(n_tiles,),
        in_specs=[
            pl.BlockSpec((tile, D_MODEL), lambda i: (i, 0)),
            _const_spec((1, D_MODEL)),
            _const_spec((ROUTER_LANES, D_MODEL)),
            _const_spec((ROUTER_LANES, 1)),
        ],
        out_specs=[
            pl.BlockSpec((tile, D_MODEL), lambda i: (i, 0)),
            pl.BlockSpec((8, tile), lambda i: (0, i)),
            pl.BlockSpec((8, tile), lambda i: (0, i)),
            _const_spec((N_EXPERTS, 128)),
        ],
        out_shape=[
            jax.ShapeDtypeStruct((n_tok, D_MODEL), F32),
            jax.ShapeDtypeStruct((8, n_tok), jnp.int32),
            jax.ShapeDtypeStruct((8, n_tok), F32),
            jax.ShapeDtypeStruct((N_EXPERTS, 128), F32),
        ],
        scratch_shapes=[pltpu.VMEM((N_EXPERTS, 1), F32)],
        compiler_params=_cparams(1),
        name="moe_router",
    )(x, g_ffn, wrT, br)

    dest, blk, nused = pl.pallas_call(
        functools.partial(_dest_kernel, n_blocks_padded=n_blocks_padded),
        grid=(n_tiles,),
        in_specs=[_const_spec((N_EXPERTS, 128)), pl.BlockSpec((8, tile), lambda i: (0, i))],
        out_specs=[
            pl.BlockSpec((8, tile), lambda i: (0, i)),
            _const_spec((8, n_blocks_padded)),
            _const_spec((8, 128)),
        ],
        out_shape=[
            jax.ShapeDtypeStruct((8, n_tok), jnp.int32),
            jax.ShapeDtypeStruct((8, n_blocks_padded), jnp.int32),
            jax.ShapeDtypeStruct((8, 128), jnp.int32),
        ],
        compiler_params=_cparams(1),
        name="moe_dest",
    )(cnt, ri)

    dest_tiles = dest[0:2].reshape(2, n_tiles, tile).transpose(1, 0, 2).reshape(n_tiles, 1, 2 * tile)
    dest_spec = pl.BlockSpec((1, 1, 2 * tile), lambda i: (i, 0, 0), memory_space=pltpu.SMEM)
    any_spec = pl.BlockSpec(memory_space=pl.ANY)

    xs = pl.pallas_call(
        functools.partial(_dispatch_kernel, tile=tile),
        grid=(n_tiles,),
        in_specs=[dest_spec, any_spec, any_spec],
        out_specs=any_spec,
        out_shape=jax.ShapeDtypeStruct((n_rows, D_MODEL), F32),
        scratch_shapes=[pltpu.SemaphoreType.DMA(())],
        input_output_aliases={2: 0},
        compiler_params=_cparams(1),
        name="moe_dispatch",
    )(dest_tiles, h, jnp.zeros((n_rows, D_MODEL), F32))

    ys = pl.pallas_call(
        _expert_kernel,
        grid_spec=pltpu.PrefetchScalarGridSpec(
            num_scalar_prefetch=2,
            grid=(n_blocks,),
            in_specs=[
                pl.BlockSpec((MOE_ROWS, D_MODEL), lambda b, blk, nu: (b, 0)),
                pl.BlockSpec((1, D_MODEL, 2 * D_EXPERT), lambda b, blk, nu: (blk[b], 0, 0)),
                pl.BlockSpec((1, D_EXPERT, D_MODEL), lambda b, blk, nu: (blk[b], 0, 0)),
            ],
            out_specs=pl.BlockSpec((MOE_ROWS, D_MODEL), lambda b, blk, nu: (b, 0)),
        ),
        out_shape=jax.ShapeDtypeStruct((n_rows, D_MODEL), F32),
        compiler_params=_cparams(1),
        name="moe_experts",
    )(blk[0, :n_blocks], nused[0, :1], xs, wgu, wd)

    ctile = min(256, n_tok)
    c_tiles = n_tok // ctile
    dest_c = dest[0:2].reshape(2, c_tiles, ctile).transpose(1, 0, 2).reshape(c_tiles, 1, 2 * ctile)
    return pl.pallas_call(
        functools.partial(_combine_kernel, tile=ctile, final_norm=final_norm),
        grid=(c_tiles,),
        in_specs=[
            pl.BlockSpec((1, 1, 2 * ctile), lambda i: (i, 0, 0), memory_space=pltpu.SMEM),
            any_spec,
            pl.BlockSpec((ctile, D_MODEL), lambda i: (i, 0)),
            pl.BlockSpec((ctile, 2), lambda i: (i, 0)),
            _const_spec((1, D_MODEL)),
        ],
        out_specs=pl.BlockSpec((ctile, D_MODEL), lambda i: (i, 0)),
        out_shape=jax.ShapeDtypeStruct((n_tok, D_MODEL), F32),
        scratch_shapes=[pltpu.VMEM((2, ctile, D_MODEL), F32), pltpu.SemaphoreType.DMA(())],
        compiler_params=_cparams(1),
        name="moe_combine",
    )(dest_c, ys, x, rw[0:2].T, g_final)


def _rope_tables(n_tokens):
    rows = n_tokens // GRID_W
    row_idx = jnp.repeat(jnp.arange(rows, dtype=F32), GRID_W)
    col_idx = jnp.tile(jnp.arange(GRID_W, dtype=F32), rows)

    def angles(rot_dim):
        n_freq = rot_dim // 4
        inv_freq = ROPE_THETA ** (-jnp.arange(n_freq, dtype=F32) / n_freq)
        return jnp.concatenate([row_idx[:, None] * inv_freq, col_idx[:, None] * inv_freq], axis=-1)

    ag, am = angles(GQA_HD), angles(MLA_ROPE)
    cg, sg, cm, sm = jnp.cos(ag), jnp.sin(ag), jnp.cos(am), jnp.sin(am)
    hr = MLA_ROPE // 2
    zl = jnp.zeros((n_tokens, MLA_NOPE), F32)
    zr = jnp.zeros((n_tokens, HEAD_PAD - MLA_NOPE - MLA_ROPE), F32)
    del hr
    return dict(
        cT_g=cg.T, sT_g=sg.T,
        cF_g=jnp.concatenate([cg, cg], axis=1), sF_g=jnp.concatenate([-sg, sg], axis=1),
        cT_m=cm.T, sT_m=sm.T,
        cP_m=jnp.concatenate([zl, cm, cm, zr], axis=1), sP_m=jnp.concatenate([zl, -sm, sm, zr], axis=1),
    )


def _layer_params(l, norm_mix, norm_mem, w_in, gqa_q_norm, gqa_k_norm, mla_q_a_norm, mla_kv_a_norm, mla_w_qb,
                  mla_w_kvb, mem_w_kv, w_out, norm_ffn, w_group, b_group, w_expert, b_expert, w_gate_up, w_down):
    splits = []
    off = 0
    for width in (GQA_HEADS * GQA_HD, GQA_KV_HEADS * GQA_HD, GQA_KV_HEADS * GQA_HD, MLA_Q_LORA, MLA_KV_LORA,
                  MLA_ROPE, MEM_HEADS * MEM_HD, N_BRANCH * D_MODEL):
        splits.append(w_in[l][:, off:off + width])
        off += width
    wq_g, wk_g, wv_g, wq_a, wkv_a, wk_r, wq_c, w_gate = splits
    hr = MLA_ROPE // 2
    pad_l = jnp.zeros((D_MODEL, MLA_NOPE), F32)
    pad_r = jnp.zeros((D_MODEL, HEAD_PAD - MLA_NOPE - MLA_ROPE), F32)
    wk_r_swapped = jnp.concatenate([wk_r[:, hr:], wk_r[:, :hr]], axis=1)
    wkr2 = jnp.concatenate([pad_l, wk_r, pad_r, pad_l, wk_r_swapped, pad_r], axis=1)
    qk = MLA_NOPE + MLA_ROPE
    wqb = jnp.pad(mla_w_qb[l].reshape(MLA_Q_LORA, MLA_HEADS, qk), ((0, 0), (0, 0), (0, HEAD_PAD - qk)))
    wkvb = mla_w_kvb[l].reshape(MLA_KV_LORA, MLA_HEADS, MLA_NOPE + MLA_V)
    wkn = jnp.pad(wkvb[:, :, :MLA_NOPE], ((0, 0), (0, 0), (0, HEAD_PAD - MLA_NOPE)))
    wv_m = wkvb[:, :, MLA_NOPE:]
    n_router = N_GROUPS + N_EXPERTS
    return dict(
        g_mix=norm_mix[l][None, :], g_mem=norm_mem[l][None, :], g_ffn=norm_ffn[l][None, :],
        wqT_g=wq_g.T.astype(BF16), wk_g=wk_g.astype(BF16), wvT_g=wv_g.T.astype(BF16),
        gq=gqa_q_norm[l][:, None], gk=gqa_k_norm[l][None, :],
        wqaT=wq_a.T.astype(BF16), gqa=mla_q_a_norm[l][:, None],
        wqbT=wqb.reshape(MLA_Q_LORA, MLA_HEADS * HEAD_PAD).T.astype(BF16),
        wkva=wkv_a.astype(BF16), gkva=mla_kv_a_norm[l][None, :],
        wkn=wkn.reshape(MLA_KV_LORA, MLA_HEADS * HEAD_PAD).astype(BF16), wkr2=wkr2.astype(BF16),
        wvT_m=wv_m.reshape(MLA_KV_LORA, MLA_HEADS * MLA_V).T.astype(BF16),
        wmiscT=jnp.concatenate([wq_c, w_gate], axis=1).T.astype(BF16),
        wmk=mem_w_kv[l][:, :MEM_HEADS * MEM_HD].astype(BF16),
        wmvT=mem_w_kv[l][:, MEM_HEADS * MEM_HD:].T.astype(BF16),
        w_out=w_out[l].astype(BF16),
        wrT=jnp.concatenate([w_group[l].T, w_expert[l].T, jnp.zeros((ROUTER_LANES - n_router, D_MODEL), F32)], axis=0),
        br=jnp.concatenate([b_group[l], b_expert[l], jnp.zeros((ROUTER_LANES - n_router,), F32)])[:, None],
        wgu=w_gate_up[l].astype(BF16), wd=w_down[l].astype(BF16),
    )


def _mixer(x, mem, p, tabs):
    b, s, _ = x.shape
    tile = min(TOK_TILE, s)
    n_t = s // tile
    grid = (b, n_t)
    x_spec = pl.BlockSpec((1, tile, D_MODEL), lambda bi, i: (bi, i, 0))
    colT = lambda rows: pl.BlockSpec((rows, tile), lambda bi, i: (0, i))
    rowF = pl.BlockSpec((tile, HEAD_PAD), lambda bi, i: (i, 0))
    headT = lambda heads: pl.BlockSpec((1, heads, HEAD_PAD, tile), lambda bi, i: (bi, 0, 0, i))
    keyF = lambda heads: pl.BlockSpec((1, heads, tile, HEAD_PAD), lambda bi, i: (bi, 0, i, 0))
    valT = lambda heads: pl.BlockSpec((1, heads, 1, HEAD_PAD, tile), lambda bi, i: (bi, 0, i, 0, 0))

    qg, kg, vg = pl.pallas_call(
        _proj_gqa_kernel,
        grid=grid,
        in_specs=[x_spec, _const_spec((1, D_MODEL)), _const_spec(p["wqT_g"].shape), _const_spec(p["wk_g"].shape),
                  _const_spec(p["wvT_g"].shape), _const_spec((GQA_HD, 1)), _const_spec((1, GQA_HD)),
                  colT(GQA_HD // 2), colT(GQA_HD // 2), rowF, rowF],
        out_specs=[headT(GQA_HEADS), keyF(GQA_KV_HEADS), valT(GQA_KV_HEADS)],
        out_shape=[jax.ShapeDtypeStruct((b, GQA_HEADS, HEAD_PAD, s), BF16),
                   jax.ShapeDtypeStruct((b, GQA_KV_HEADS, s, HEAD_PAD), BF16),
                   jax.ShapeDtypeStruct((b, GQA_KV_HEADS, n_t, HEAD_PAD, tile), BF16)],
        compiler_params=_cparams(2),
        name="proj_gqa",
    )(x, p["g_mix"], p["wqT_g"], p["wk_g"], p["wvT_g"], p["gq"], p["gk"],
      tabs["cT_g"], tabs["sT_g"], tabs["cF_g"], tabs["sF_g"])

    qm, km, vm = pl.pallas_call(
        _proj_mla_kernel,
        grid=grid,
        in_specs=[x_spec, _const_spec((1, D_MODEL)), _const_spec(p["wqaT"].shape), _const_spec((MLA_Q_LORA, 1)),
                  _const_spec(p["wqbT"].shape), _const_spec(p["wkva"].shape), _const_spec((1, MLA_KV_LORA)),
                  _const_spec(p["wkn"].shape), _const_spec(p["wkr2"].shape), _const_spec(p["wvT_m"].shape),
                  colT(MLA_ROPE // 2), colT(MLA_ROPE // 2), rowF, rowF],
        out_specs=[headT(MLA_HEADS), keyF(MLA_HEADS), valT(MLA_HEADS)],
        out_shape=[jax.ShapeDtypeStruct((b, MLA_HEADS, HEAD_PAD, s), BF16),
                   jax.ShapeDtypeStruct((b, MLA_HEADS, s, HEAD_PAD), BF16),
                   jax.ShapeDtypeStruct((b, MLA_HEADS, n_t, HEAD_PAD, tile), BF16)],
        compiler_params=_cparams(2),
        name="proj_mla",
    )(x, p["g_mix"], p["wqaT"], p["gqa"], p["wqbT"], p["wkva"], p["gkva"], p["wkn"], p["wkr2"], p["wvT_m"],
      tabs["cT_m"], tabs["sT_m"], tabs["cP_m"], tabs["sP_m"])

    featT = lambda rows: pl.BlockSpec((1, rows, tile), lambda bi, i: (bi, 0, i))
    qc, gates = pl.pallas_call(
        _proj_misc_kernel,
        grid=grid,
        in_specs=[x_spec, _const_spec((1, D_MODEL)), _const_spec(p["wmiscT"].shape)],
        out_specs=[featT(D_MODEL), featT(N_BRANCH * D_MODEL)],
        out_shape=[jax.ShapeDtypeStruct((b, D_MODEL, s), BF16),
                   jax.ShapeDtypeStruct((b, N_BRANCH * D_MODEL, s), BF16)],
        compiler_params=_cparams(2),
        name="proj_misc",
    )(x, p["g_mix"], p["wmiscT"])

    n_mem = mem.shape[1]
    mk, mvT = pl.pallas_call(
        _mem_kv_kernel,
        grid=(b,),
        in_specs=[pl.BlockSpec((1, n_mem, D_MODEL), lambda bi: (bi, 0, 0)), _const_spec((1, D_MODEL)),
                  _const_spec(p["wmk"].shape), _const_spec(p["wmvT"].shape)],
        out_specs=[pl.BlockSpec((1, n_mem, D_MODEL), lambda bi: (bi, 0, 0)),
                   pl.BlockSpec((1, D_MODEL, n_mem), lambda bi: (bi, 0, 0))],
        out_shape=[jax.ShapeDtypeStruct((b, n_mem, D_MODEL), BF16), jax.ShapeDtypeStruct((b, D_MODEL, n_mem), BF16)],
        compiler_params=_cparams(1),
        name="mem_kv",
    )(mem, p["g_mem"], p["wmk"], p["wmvT"])

    oa = _attention(qg, kg, vg, GQA_HEADS // GQA_KV_HEADS).reshape(b, D_MODEL, s)
    ob = _attention(qm, km, vm, 1).reshape(b, D_MODEL, s)
    oc = pl.pallas_call(
        _mem_attn_kernel,
        grid=grid,
        in_specs=[featT(D_MODEL), pl.BlockSpec((1, n_mem, D_MODEL), lambda bi, i: (bi, 0, 0)),
                  pl.BlockSpec((1, D_MODEL, n_mem), lambda bi, i: (bi, 0, 0))],
        out_specs=featT(D_MODEL),
        out_shape=jax.ShapeDtypeStruct((b, D_MODEL, s), BF16),
        compiler_params=_cparams(2),
        name="mem_attention",
    )(qc, mk, mvT)

    return pl.pallas_call(
        _merge_kernel,
        grid=grid,
        in_specs=[featT(D_MODEL), featT(D_MODEL), featT(D_MODEL), featT(N_BRANCH * D_MODEL), x_spec,
                  _const_spec((D_MODEL, D_MODEL))],
        out_specs=x_spec,
        out_shape=jax.ShapeDtypeStruct(x.shape, F32),
        compiler_params=_cparams(2),
        name="merge_out",
    )(oa, ob, oc, gates, x, p["w_out"])


def _trunk(x, mem, layers, g_final):
    b, s, d = x.shape
    tabs = _rope_tables(s)
    for l, p in enumerate(layers):
        x = _mixer(x, mem, p, tabs)
        x = _moe(x.reshape(b * s, d), p["g_ffn"], p["wrT"], p["br"], p["wgu"], p["wd"], g_final,
                 final_norm=(l == len(layers) - 1)).reshape(b, s, d)
    return x


def kernel(x_prompt, x_sample, mem_prompt, mem_sample, norm_mix, norm_mem, w_in, gqa_q_norm, gqa_k_norm, mla_q_a_norm, mla_kv_a_norm, mla_w_qb, mla_w_kvb, mem_w_kv, w_out, norm_ffn, w_group, b_group, w_expert, b_expert, w_gate_up, w_down, norm_final):
    depth = w_in.shape[0]
    layers = [_layer_params(l, norm_mix, norm_mem, w_in, gqa_q_norm, gqa_k_norm, mla_q_a_norm, mla_kv_a_norm,
                            mla_w_qb, mla_w_kvb, mem_w_kv, w_out, norm_ffn, w_group, b_group, w_expert, b_expert,
                            w_gate_up, w_down) for l in range(depth)]
    g_final = norm_final[None, :]
    return (_trunk(x_prompt, mem_prompt, layers, g_final), _trunk(x_sample, mem_sample, layers, g_final))
```

```python
import functools
import math

import jax
import jax.numpy as jnp
from jax import lax
from jax.experimental import pallas as pl
from jax.experimental.pallas import tpu as pltpu

F32 = jnp.float32
BF16 = jnp.bfloat16

D_MODEL = 1024
EPS = 1e-6
GRID_W = 64
ROPE_THETA = 10000.0
GQA_HEADS, GQA_KV_HEADS, GQA_HD = 8, 2, 128
MLA_HEADS, MLA_Q_LORA, MLA_KV_LORA = 8, 384, 256
MLA_NOPE, MLA_ROPE, MLA_V = 64, 32, 128
MEM_HEADS, MEM_HD = 4, 256
N_BRANCH = 3
N_GROUPS, EXPERTS_PER_GROUP, D_EXPERT = 8, 8, 256
N_EXPERTS = N_GROUPS * EXPERTS_PER_GROUP
HEAD_PAD = 128
LOG2E = 1.4426950408889634

TOK_TILE = 512
ATT_Q_COLS = 2048
MOE_ROWS = 256
ROUTER_LANES = 128
VMEM_LIMIT = 56 * 1024 * 1024


def _cparams(n_axes):
    return pltpu.CompilerParams(dimension_semantics=("arbitrary",) * n_axes, vmem_limit_bytes=VMEM_LIMIT)


def _nt(a, b):
    return lax.dot_general(a, b, (((1,), (1,)), ((), ())), preferred_element_type=F32)


def _rms_rows(x, g):
    return x * lax.rsqrt(jnp.mean(x * x, axis=-1, keepdims=True) + EPS) * g


def _rms_cols(x, g):
    return x * lax.rsqrt(jnp.mean(x * x, axis=0, keepdims=True) + EPS) * g


def _const_spec(shape):
    return pl.BlockSpec(shape, lambda *_: (0,) * len(shape))


def _proj_gqa_kernel(x_ref, g_ref, wqT_ref, wk_ref, wvT_ref, gq_ref, gk_ref, cT_ref, sT_ref, cF_ref, sF_ref,
                     q_ref, k_ref, v_ref):
    h = _rms_rows(x_ref[0], g_ref[...]).astype(BF16)
    qT = _nt(wqT_ref[...], h)
    c, s, gq = cT_ref[...], sT_ref[...], gq_ref[...]
    half = GQA_HD // 2
    scale = GQA_HD ** -0.5 * LOG2E
    for hd in range(GQA_HEADS):
        qn = _rms_cols(qT[hd * GQA_HD:(hd + 1) * GQA_HD], gq)
        x1, x2 = qn[:half], qn[half:]
        o = jnp.concatenate([x1 * c - x2 * s, x2 * c + x1 * s], axis=0) * scale
        q_ref[0, hd] = o.astype(BF16)
    k = jnp.dot(h, wk_ref[...], preferred_element_type=F32)
    for j in range(GQA_KV_HEADS):
        kn = _rms_rows(k[:, j * GQA_HD:(j + 1) * GQA_HD], gk_ref[...])
        ko = kn * cF_ref[...] + pltpu.roll(kn, half, 1) * sF_ref[...]
        k_ref[0, j] = ko.astype(BF16)
    vT = _nt(wvT_ref[...], h)
    for j in range(GQA_KV_HEADS):
        v_ref[0, j, 0] = vT[j * GQA_HD:(j + 1) * GQA_HD].astype(BF16)


def _proj_mla_kernel(x_ref, g_ref, wqaT_ref, gqa_ref, wqbT_ref, wkva_ref, gkva_ref, wkn_ref, wkr_ref, wvT_ref,
                     cT_ref, sT_ref, cP_ref, sP_ref, q_ref, k_ref, v_ref):
    h = _rms_rows(x_ref[0], g_ref[...]).astype(BF16)
    qa = _rms_cols(_nt(wqaT_ref[...], h), gqa_ref[...]).astype(BF16)
    qT = jnp.dot(wqbT_ref[...], qa, preferred_element_type=F32)
    c, s = cT_ref[...], sT_ref[...]
    hr = MLA_ROPE // 2
    scale = (MLA_NOPE + MLA_ROPE) ** -0.5 * LOG2E
    for hd in range(MLA_HEADS):
        q = qT[hd * HEAD_PAD:(hd + 1) * HEAD_PAD]
        x1, x2 = q[MLA_NOPE:MLA_NOPE + hr], q[MLA_NOPE + hr:MLA_NOPE + 2 * hr]
        o = jnp.concatenate([q[:MLA_NOPE], x1 * c - x2 * s, x2 * c + x1 * s, q[MLA_NOPE + 2 * hr:]], axis=0) * scale
        q_ref[0, hd] = o.astype(BF16)
    kva = jnp.dot(h, wkva_ref[...], preferred_element_type=F32)
    kvn = _rms_rows(kva, gkva_ref[...]).astype(BF16)
    kr2 = jnp.dot(h, wkr_ref[...], preferred_element_type=F32)
    kr = kr2[:, :HEAD_PAD] * cP_ref[...] + kr2[:, HEAD_PAD:] * sP_ref[...]
    kn = jnp.dot(kvn, wkn_ref[...], preferred_element_type=F32)
    for hd in range(MLA_HEADS):
        k_ref[0, hd] = (kn[:, hd * HEAD_PAD:(hd + 1) * HEAD_PAD] + kr).astype(BF16)
    vT = _nt(wvT_ref[...], kvn)
    for hd in range(MLA_HEADS):
        v_ref[0, hd, 0] = vT[hd * MLA_V:(hd + 1) * MLA_V].astype(BF16)


def _proj_misc_kernel(x_ref, g_ref, wT_ref, qc_ref, gate_ref):
    h = _rms_rows(x_ref[0], g_ref[...]).astype(BF16)
    scale = MEM_HD ** -0.5 * LOG2E
    rows = 512
    n_q = D_MODEL // rows
    for ci in range((D_MODEL + N_BRANCH * D_MODEL) // rows):
        y = _nt(wT_ref[ci * rows:(ci + 1) * rows, :], h)
        if ci < n_q:
            qc_ref[0, ci * rows:(ci + 1) * rows, :] = (y * scale).astype(BF16)
        else:
            r0 = (ci - n_q) * rows
            gate_ref[0, r0:r0 + rows, :] = jax.nn.sigmoid(y).astype(BF16)


def _mem_kv_kernel(mem_ref, g_ref, wk_ref, wvT_ref, mk_ref, mvT_ref):
    h = _rms_rows(mem_ref[0], g_ref[...]).astype(BF16)
    mk_ref[0] = jnp.dot(h, wk_ref[...], preferred_element_type=F32).astype(BF16)
    mvT_ref[0] = _nt(wvT_ref[...], h).astype(BF16)


def _attn_kernel(q_ref, k_ref, v_ref, o_ref, m_sc, l_sc, acc_sc, s_sc, mc_sc, *, heads_per_step, cols_per_head,
                 n_chunks):
    chunk = TOK_TILE
    subs = [(g, j * chunk) for g in range(heads_per_step) for j in range(cols_per_head // chunk)]

    m_sc[...] = jnp.full(m_sc.shape, -jnp.inf, F32)
    l_sc[...] = jnp.zeros(l_sc.shape, F32)
    acc_sc[...] = jnp.zeros(acc_sc.shape, F32)

    def score_stage(ci, slot):
        off = pl.multiple_of(ci * chunk, chunk)
        k = k_ref[0, 0, pl.ds(off, chunk), :]
        for u, (g, c0) in enumerate(subs):
            s = jnp.dot(k, q_ref[0, g, :, c0:c0 + chunk], preferred_element_type=F32)
            s_sc[slot, u] = s
            mc_sc[slot, u] = jnp.max(s, axis=0, keepdims=True)

    def softmax_stage(ci, slot):
        vT = v_ref[0, 0, ci]
        for u in range(len(subs)):
            m_prev = m_sc[u]
            m_cur = jnp.maximum(m_prev, mc_sc[slot, u])
            alpha = jnp.exp2(m_prev - m_cur)
            p = jnp.exp2(s_sc[slot, u] - m_cur)
            l_sc[u] = alpha * l_sc[u] + jnp.sum(p, axis=0, keepdims=True)
            acc_sc[u] = alpha * acc_sc[u] + jnp.dot(vT, p.astype(BF16), preferred_element_type=F32)
            m_sc[u] = m_cur

    score_stage(0, 0)
    if n_chunks > 1:
        assert n_chunks % 2 == 0

        def body(i, carry):
            c = 2 * i
            score_stage(c + 1, 1)
            softmax_stage(c, 0)
            score_stage(c + 2, 0)
            softmax_stage(c + 1, 1)
            return carry

        lax.fori_loop(0, n_chunks // 2 - 1, body, 0)
        score_stage(n_chunks - 1, 1)
        softmax_stage(n_chunks - 2, 0)
        softmax_stage(n_chunks - 1, 1)
    else:
        softmax_stage(0, 0)

    for u, (g, c0) in enumerate(subs):
        o_ref[0, g, :, c0:c0 + chunk] = (acc_sc[u] / l_sc[u]).astype(BF16)


def _attention(qT, k, vT, group):
    b, h, hd, s = qT.shape
    hkv = k.shape[1]
    assert h == hkv * group
    cols_per_head = min(ATT_Q_COLS // group, s)
    n_chunks = s // TOK_TILE
    n_sub = group * (cols_per_head // TOK_TILE)
    kern = functools.partial(_attn_kernel, heads_per_step=group, cols_per_head=cols_per_head, n_chunks=n_chunks)
    return pl.pallas_call(
        kern,
        grid=(b, hkv, s // cols_per_head),
        in_specs=[
            pl.BlockSpec((1, group, hd, cols_per_head), lambda bi, j, qi: (bi, j, 0, qi)),
            pl.BlockSpec((1, 1, s, hd), lambda bi, j, qi: (bi, j, 0, 0)),
            pl.BlockSpec((1, 1, n_chunks, hd, TOK_TILE), lambda bi, j, qi: (bi, j, 0, 0, 0)),
        ],
        out_specs=pl.BlockSpec((1, group, hd, cols_per_head), lambda bi, j, qi: (bi, j, 0, qi)),
        out_shape=jax.ShapeDtypeStruct(qT.shape, BF16),
        scratch_shapes=[
            pltpu.VMEM((n_sub, 1, TOK_TILE), F32),
            pltpu.VMEM((n_sub, 1, TOK_TILE), F32),
            pltpu.VMEM((n_sub, hd, TOK_TILE), F32),
            pltpu.VMEM((2, n_sub, TOK_TILE, TOK_TILE), F32),
            pltpu.VMEM((2, n_sub, 1, TOK_TILE), F32),
        ],
        compiler_params=_cparams(3),
        name="attention",
    )(qT, k, vT)


def _mem_attn_kernel(qc_ref, mk_ref, mvT_ref, o_ref):
    for hd in range(MEM_HEADS):
        r = slice(hd * MEM_HD, (hd + 1) * MEM_HD)
        s = jnp.dot(mk_ref[0, :, r], qc_ref[0, r, :], preferred_element_type=F32)
        p = jnp.exp2(s - jnp.max(s, axis=0, keepdims=True))
        l = jnp.sum(p, axis=0, keepdims=True)
        o = jnp.dot(mvT_ref[0, r, :], p.astype(BF16), preferred_element_type=F32)
        o_ref[0, r, :] = (o / l).astype(BF16)


def _merge_kernel(oa_ref, ob_ref, oc_ref, gate_ref, x_ref, w_ref, y_ref):
    merged = (gate_ref[0, 0:D_MODEL, :].astype(F32) * oa_ref[0].astype(F32)
              + gate_ref[0, D_MODEL:2 * D_MODEL, :].astype(F32) * ob_ref[0].astype(F32)
              + gate_ref[0, 2 * D_MODEL:, :].astype(F32) * oc_ref[0].astype(F32)).astype(BF16)
    out = lax.dot_general(merged, w_ref[...], (((0,), (0,)), ((), ())), preferred_element_type=F32)
    y_ref[0] = x_ref[0] + out


def _first_index_of_max(v, vmax, iota, n):
    return jnp.min(jnp.where(v == vmax, iota, n), axis=0, keepdims=True)


def _router_kernel(x_ref, g_ref, wT_ref, b_ref, ri_ref, rw_ref, cnt_ref, carry_sc):
    step = pl.program_id(0)

    @pl.when(step == 0)
    def _():
        carry_sc[...] = jnp.zeros(carry_sc.shape, F32)

    h = _rms_rows(x_ref[...], g_ref[...])
    t = h.shape[0]
    logits = lax.dot_general(wT_ref[...], h, (((1,), (1,)), ((), ())), precision=lax.Precision.HIGHEST,
                             preferred_element_type=F32) + b_ref[...]
    iota_g = lax.broadcasted_iota(jnp.int32, (N_GROUPS, t), 0)
    gl = logits[0:N_GROUPS]
    gmax = jnp.max(gl, axis=0, keepdims=True)
    gidx = _first_index_of_max(gl, gmax, iota_g, N_GROUPS)
    gw = 1.0 / jnp.sum(jnp.exp(gl - gmax), axis=0, keepdims=True)
    sel = jnp.zeros((EXPERTS_PER_GROUP, t), F32)
    for g in range(N_GROUPS):
        r0 = N_GROUPS + g * EXPERTS_PER_GROUP
        sel = sel + jnp.where(gidx == g, logits[r0:r0 + EXPERTS_PER_GROUP], 0.0)
    v1 = jnp.max(sel, axis=0, keepdims=True)
    i1 = _first_index_of_max(sel, v1, iota_g, EXPERTS_PER_GROUP)
    sel2 = jnp.where(iota_g == i1, -jnp.inf, sel)
    v2 = jnp.max(sel2, axis=0, keepdims=True)
    i2 = _first_index_of_max(sel2, v2, iota_g, EXPERTS_PER_GROUP)
    e = jnp.exp(v2 - v1)
    w1 = gw / (1.0 + e)
    w2 = gw * e / (1.0 + e)
    e1 = gidx * EXPERTS_PER_GROUP + i1
    e2 = gidx * EXPERTS_PER_GROUP + i2

    iota_e = lax.broadcasted_iota(jnp.int32, (N_EXPERTS, t), 0)
    hit1 = iota_e == e1
    hit2 = iota_e == e2
    hits = jnp.where(hit1 | hit2, 1.0, 0.0)
    before = (lax.broadcasted_iota(jnp.int32, (t, t), 0) < lax.broadcasted_iota(jnp.int32, (t, t), 1))
    prefix = jnp.dot(hits.astype(BF16), jnp.where(before, 1.0, 0.0).astype(BF16),
                     preferred_element_type=F32) + carry_sc[...]
    r1 = jnp.sum(jnp.where(hit1, prefix, 0.0), axis=0, keepdims=True)
    r2 = jnp.sum(jnp.where(hit2, prefix, 0.0), axis=0, keepdims=True)
    carry_sc[...] = carry_sc[...] + jnp.sum(hits, axis=1, keepdims=True)

    ri_ref[...] = jnp.zeros(ri_ref.shape, jnp.int32)
    ri_ref[0:1, :] = e1
    ri_ref[1:2, :] = e2
    ri_ref[2:3, :] = r1.astype(jnp.int32)
    ri_ref[3:4, :] = r2.astype(jnp.int32)
    rw_ref[...] = jnp.zeros(rw_ref.shape, F32)
    rw_ref[0:1, :] = w1
    rw_ref[1:2, :] = w2
    cnt_ref[...] = jnp.broadcast_to(carry_sc[...], cnt_ref.shape)


def _dest_kernel(cnt_ref, ri_ref, dest_ref, blk_ref, nused_ref, *, n_blocks_padded):
    cnt = cnt_ref[...]
    pad = jnp.floor((cnt + (MOE_ROWS - 1)) * (1.0 / MOE_ROWS)) * MOE_ROWS
    lower = jnp.where(lax.broadcasted_iota(jnp.int32, (N_EXPERTS, N_EXPERTS), 1)
                      <= lax.broadcasted_iota(jnp.int32, (N_EXPERTS, N_EXPERTS), 0), 1.0, 0.0)
    pad_end = jnp.dot(lower, pad, precision=lax.Precision.HIGHEST, preferred_element_type=F32)
    pad_start = (pad_end - pad)[:, 0:1]
    t = ri_ref.shape[1]
    iota_e = lax.broadcasted_iota(jnp.int32, (N_EXPERTS, t), 0)
    dest_ref[...] = jnp.zeros(dest_ref.shape, jnp.int32)
    for kk in range(2):
        start = jnp.sum(jnp.where(iota_e == ri_ref[kk:kk + 1, :], pad_start, 0.0), axis=0, keepdims=True)
        dest_ref[kk:kk + 1, :] = start.astype(jnp.int32) + ri_ref[2 + kk:3 + kk, :]
    blk_start = (lax.broadcasted_iota(jnp.int32, (N_EXPERTS, n_blocks_padded), 1) * MOE_ROWS).astype(F32)
    owner = jnp.sum(jnp.where(pad_end[:, 0:1] <= blk_start, 1.0, 0.0), axis=0, keepdims=True)
    blk_ref[...] = jnp.broadcast_to(jnp.minimum(owner, N_EXPERTS - 1.0).astype(jnp.int32), blk_ref.shape)
    nused_ref[...] = jnp.broadcast_to((pad_end[N_EXPERTS - 1:N_EXPERTS, :] * (1.0 / MOE_ROWS)).astype(jnp.int32),
                                      nused_ref.shape)


def _row_copy(src_ref, src_row, dst_ref, dst_row, sem):
    return pltpu.make_async_copy(src_ref.at[pl.ds(src_row, 1)], dst_ref.at[pl.ds(dst_row, 1)], sem)


def _dispatch_kernel(dest_ref, x_ref, g_ref, xs_in_ref, xs_ref, h_sc, sem, *, tile):
    del xs_in_ref
    h_sc[...] = _rms_rows(x_ref[...], g_ref[...])

    def issue(t, carry):
        _row_copy(h_sc, t, xs_ref, dest_ref[0, 0, t], sem).start()
        _row_copy(h_sc, t, xs_ref, dest_ref[0, 0, tile + t], sem).start()
        return carry

    def drain(t, carry):
        _row_copy(h_sc, 0, xs_ref, 0, sem).wait()
        _row_copy(h_sc, 0, xs_ref, 0, sem).wait()
        return carry

    lax.fori_loop(0, tile, issue, 0)
    lax.fori_loop(0, tile, drain, 0)


def _expert_kernel(blk_ref, nused_ref, xs_ref, wgu_ref, wd_ref, ys_ref):
    b = pl.program_id(0)

    @pl.when(b < nused_ref[0])
    def _():
        gu = jnp.dot(xs_ref[...].astype(BF16), wgu_ref[0], preferred_element_type=F32)
        a, u = gu[:, :D_EXPERT], gu[:, D_EXPERT:]
        hidden = (a * jax.nn.sigmoid(a) * u).astype(BF16)
        ys_ref[...] = jnp.dot(hidden, wd_ref[0], preferred_element_type=F32)

    @pl.when(b >= nused_ref[0])
    def _():
        ys_ref[...] = jnp.zeros(ys_ref.shape, F32)


def _combine_kernel(dest_ref, ys_ref, x_ref, w_ref, g_ref, y_ref, buf, sem, *, tile, final_norm):
    def issue(t, carry):
        _row_copy(ys_ref, dest_ref[0, 0, t], buf.at[0], t, sem).start()
        _row_copy(ys_ref, dest_ref[0, 0, tile + t], buf.at[1], t, sem).start()
        return carry

    def drain(t, carry):
        _row_copy(ys_ref, 0, buf.at[0], 0, sem).wait()
        _row_copy(ys_ref, 0, buf.at[1], 0, sem).wait()
        return carry

    lax.fori_loop(0, tile, issue, 0)
    lax.fori_loop(0, tile, drain, 0)
    w = w_ref[...]
    y = x_ref[...] + (w[:, 0:1] * buf[0] + w[:, 1:2] * buf[1])
    if final_norm:
        y = _rms_rows(y, g_ref[...])
    y_ref[...] = y


def _moe(x, g_ffn, wrT, br, wgu, wd, g_final, final_norm):
    n_tok = x.shape[0]
    tile = min(TOK_TILE, n_tok)
    n_tiles = n_tok // tile
    n_blocks = (2 * n_tok + MOE_ROWS - 1) // MOE_ROWS + N_EXPERTS
    n_blocks_padded = (n_blocks + 127) // 128 * 128
    n_rows = n_blocks * MOE_ROWS

    ri, rw, cnt = pl.pallas_call(
        _router_kernel,
        grid=(n_tiles,),
        in_specs=[
            pl.BlockSpec((tile, D_MODEL), lambda i: (i, 0)),
            _const_spec((1, D_MODEL)),
            _const_spec((ROUTER_LANES, D_MODEL)),
            _const_spec((ROUTER_LANES, 1)),
        ],
        out_specs=[
            pl.BlockSpec((8, tile), lambda i: (0, i)),
            pl.BlockSpec((8, tile), lambda i: (0, i)),
            _const_spec((N_EXPERTS, 128)),
        ],
        out_shape=[
            jax.ShapeDtypeStruct((8, n_tok), jnp.int32),
            jax.ShapeDtypeStruct((8, n_tok), F32),
            jax.ShapeDtypeStruct((N_EXPERTS, 128), F32),
        ],
        scratch_shapes=[pltpu.VMEM((N_EXPERTS, 1), F32)],
        compiler_params=_cparams(1),
        name="moe_router",
    )(x, g_ffn, wrT, br)

    dest, blk, nused = pl.pallas_call(
        functools.partial(_dest_kernel, n_blocks_padded=n_blocks_padded),
        grid=(n_tiles,),
        in_specs=[_const_spec((N_EXPERTS, 128)), pl.BlockSpec((8, tile), lambda i: (0, i))],
        out_specs=[
            pl.BlockSpec((8, tile), lambda i: (0, i)),
            _const_spec((8, n_blocks_padded)),
            _const_spec((8, 128)),
        ],
        out_shape=[
            jax.ShapeDtypeStruct((8, n_tok), jnp.int32),
            jax.ShapeDtypeStruct((8, n_blocks_padded), jnp.int32),
            jax.ShapeDtypeStruct((8, 128), jnp.int32),
        ],
        compiler_params=_cparams(1),
        name="moe_dest",
    )(cnt, ri)

    dest_tiles = dest[0:2].reshape(2, n_tiles, tile).transpose(1, 0, 2).reshape(n_tiles, 1, 2 * tile)
    dest_spec = pl.BlockSpec((1, 1, 2 * tile), lambda i: (i, 0, 0), memory_space=pltpu.SMEM)
    any_spec = pl.BlockSpec(memory_space=pl.ANY)

    xs = pl.pallas_call(
        functools.partial(_dispatch_kernel, tile=tile),
        grid=(n_tiles,),
        in_specs=[dest_spec, pl.BlockSpec((tile, D_MODEL), lambda i: (i, 0)), _const_spec((1, D_MODEL)), any_spec],
        out_specs=any_spec,
        out_shape=jax.ShapeDtypeStruct((n_rows, D_MODEL), F32),
        scratch_shapes=[pltpu.VMEM((tile, D_MODEL), F32), pltpu.SemaphoreType.DMA(())],
        input_output_aliases={3: 0},
        compiler_params=_cparams(1),
        name="moe_dispatch",
    )(dest_tiles, x, g_ffn, jnp.zeros((n_rows, D_MODEL), F32))

    ys = pl.pallas_call(
        _expert_kernel,
        grid_spec=pltpu.PrefetchScalarGridSpec(
            num_scalar_prefetch=2,
            grid=(n_blocks,),
            in_specs=[
                pl.BlockSpec((MOE_ROWS, D_MODEL), lambda b, blk, nu: (b, 0)),
                pl.BlockSpec((1, D_MODEL, 2 * D_EXPERT), lambda b, blk, nu: (blk[b], 0, 0)),
                pl.BlockSpec((1, D_EXPERT, D_MODEL), lambda b, blk, nu: (blk[b], 0, 0)),
            ],
            out_specs=pl.BlockSpec((MOE_ROWS, D_MODEL), lambda b, blk, nu: (b, 0)),
        ),
        out_shape=jax.ShapeDtypeStruct((n_rows, D_MODEL), F32),
        compiler_params=_cparams(1),
        name="moe_experts",
    )(blk[0, :n_blocks], nused[0, :1], xs, wgu, wd)

    ctile = min(256, n_tok)
    c_tiles = n_tok // ctile
    dest_c = dest[0:2].reshape(2, c_tiles, ctile).transpose(1, 0, 2).reshape(c_tiles, 1, 2 * ctile)
    return pl.pallas_call(
        functools.partial(_combine_kernel, tile=ctile, final_norm=final_norm),
        grid=(c_tiles,),
        in_specs=[
            pl.BlockSpec((1, 1, 2 * ctile), lambda i: (i, 0, 0), memory_space=pltpu.SMEM),
            any_spec,
            pl.BlockSpec((ctile, D_MODEL), lambda i: (i, 0)),
            pl.BlockSpec((ctile, 2), lambda i: (i, 0)),
            _const_spec((1, D_MODEL)),
        ],
        out_specs=pl.BlockSpec((ctile, D_MODEL), lambda i: (i, 0)),
        out_shape=jax.ShapeDtypeStruct((n_tok, D_MODEL), F32),
        scratch_shapes=[pltpu.VMEM((2, ctile, D_MODEL), F32), pltpu.SemaphoreType.DMA(())],
        compiler_params=_cparams(1),
        name="moe_combine",
    )(dest_c, ys, x, rw[0:2].T, g_final)


def _rope_tables(n_tokens):
    rows = n_tokens // GRID_W
    row_idx = jnp.repeat(jnp.arange(rows, dtype=F32), GRID_W)
    col_idx = jnp.tile(jnp.arange(GRID_W, dtype=F32), rows)

    def angles(rot_dim):
        n_freq = rot_dim // 4
        inv_freq = ROPE_THETA ** (-jnp.arange(n_freq, dtype=F32) / n_freq)
        return jnp.concatenate([row_idx[:, None] * inv_freq, col_idx[:, None] * inv_freq], axis=-1)

    ag, am = angles(GQA_HD), angles(MLA_ROPE)
    cg, sg, cm, sm = jnp.cos(ag), jnp.sin(ag), jnp.cos(am), jnp.sin(am)
    hr = MLA_ROPE // 2
    zl = jnp.zeros((n_tokens, MLA_NOPE), F32)
    zr = jnp.zeros((n_tokens, HEAD_PAD - MLA_NOPE - MLA_ROPE), F32)
    del hr
    return dict(
        cT_g=cg.T, sT_g=sg.T,
        cF_g=jnp.concatenate([cg, cg], axis=1), sF_g=jnp.concatenate([-sg, sg], axis=1),
        cT_m=cm.T, sT_m=sm.T,
        cP_m=jnp.concatenate([zl, cm, cm, zr], axis=1), sP_m=jnp.concatenate([zl, -sm, sm, zr], axis=1),
    )


def _layer_params(l, norm_mix, norm_mem, w_in, gqa_q_norm, gqa_k_norm, mla_q_a_norm, mla_kv_a_norm, mla_w_qb,
                  mla_w_kvb, mem_w_kv, w_out, norm_ffn, w_group, b_group, w_expert, b_expert, w_gate_up, w_down):
    splits = []
    off = 0
    for width in (GQA_HEADS * GQA_HD, GQA_KV_HEADS * GQA_HD, GQA_KV_HEADS * GQA_HD, MLA_Q_LORA, MLA_KV_LORA,
                  MLA_ROPE, MEM_HEADS * MEM_HD, N_BRANCH * D_MODEL):
        splits.append(w_in[l][:, off:off + width])
        off += width
    wq_g, wk_g, wv_g, wq_a, wkv_a, wk_r, wq_c, w_gate = splits
    hr = MLA_ROPE // 2
    pad_l = jnp.zeros((D_MODEL, MLA_NOPE), F32)
    pad_r = jnp.zeros((D_MODEL, HEAD_PAD - MLA_NOPE - MLA_ROPE), F32)
    wk_r_swapped = jnp.concatenate([wk_r[:, hr:], wk_r[:, :hr]], axis=1)
    wkr2 = jnp.concatenate([pad_l, wk_r, pad_r, pad_l, wk_r_swapped, pad_r], axis=1)
    qk = MLA_NOPE + MLA_ROPE
    wqb = jnp.pad(mla_w_qb[l].reshape(MLA_Q_LORA, MLA_HEADS, qk), ((0, 0), (0, 0), (0, HEAD_PAD - qk)))
    wkvb = mla_w_kvb[l].reshape(MLA_KV_LORA, MLA_HEADS, MLA_NOPE + MLA_V)
    wkn = jnp.pad(wkvb[:, :, :MLA_NOPE], ((0, 0), (0, 0), (0, HEAD_PAD - MLA_NOPE)))
    wv_m = wkvb[:, :, MLA_NOPE:]
    n_router = N_GROUPS + N_EXPERTS
    return dict(
        g_mix=norm_mix[l][None, :], g_mem=norm_mem[l][None, :], g_ffn=norm_ffn[l][None, :],
        wqT_g=wq_g.T.astype(BF16), wk_g=wk_g.astype(BF16), wvT_g=wv_g.T.astype(BF16),
        gq=gqa_q_norm[l][:, None], gk=gqa_k_norm[l][None, :],
        wqaT=wq_a.T.astype(BF16), gqa=mla_q_a_norm[l][:, None],
        wqbT=wqb.reshape(MLA_Q_LORA, MLA_HEADS * HEAD_PAD).T.astype(BF16),
        wkva=wkv_a.astype(BF16), gkva=mla_kv_a_norm[l][None, :],
        wkn=wkn.reshape(MLA_KV_LORA, MLA_HEADS * HEAD_PAD).astype(BF16), wkr2=wkr2.astype(BF16),
        wvT_m=wv_m.reshape(MLA_KV_LORA, MLA_HEADS * MLA_V).T.astype(BF16),
        wmiscT=jnp.concatenate([wq_c, w_gate], axis=1).T.astype(BF16),
        wmk=mem_w_kv[l][:, :MEM_HEADS * MEM_HD].astype(BF16),
        wmvT=mem_w_kv[l][:, MEM_HEADS * MEM_HD:].T.astype(BF16),
        w_out=w_out[l].astype(BF16),
        wrT=jnp.concatenate([w_group[l].T, w_expert[l].T, jnp.zeros((ROUTER_LANES - n_router, D_MODEL), F32)], axis=0),
        br=jnp.concatenate([b_group[l], b_expert[l], jnp.zeros((ROUTER_LANES - n_router,), F32)])[:, None],
        wgu=w_gate_up[l].astype(BF16), wd=w_down[l].astype(BF16),
    )


def _mixer(x, mem, p, tabs):
    b, s, _ = x.shape
    tile = min(TOK_TILE, s)
    n_t = s // tile
    grid = (b, n_t)
    x_spec = pl.BlockSpec((1, tile, D_MODEL), lambda bi, i: (bi, i, 0))
    colT = lambda rows: pl.BlockSpec((rows, tile), lambda bi, i: (0, i))
    rowF = pl.BlockSpec((tile, HEAD_PAD), lambda bi, i: (i, 0))
    headT = lambda heads: pl.BlockSpec((1, heads, HEAD_PAD, tile), lambda bi, i: (bi, 0, 0, i))
    keyF = lambda heads: pl.BlockSpec((1, heads, tile, HEAD_PAD), lambda bi, i: (bi, 0, i, 0))
    valT = lambda heads: pl.BlockSpec((1, heads, 1, HEAD_PAD, tile), lambda bi, i: (bi, 0, i, 0, 0))

    qg, kg, vg = pl.pallas_call(
        _proj_gqa_kernel,
        grid=grid,
        in_specs=[x_spec, _const_spec((1, D_MODEL)), _const_spec(p["wqT_g"].shape), _const_spec(p["wk_g"].shape),
                  _const_spec(p["wvT_g"].shape), _const_spec((GQA_HD, 1)), _const_spec((1, GQA_HD)),
                  colT(GQA_HD // 2), colT(GQA_HD // 2), rowF, rowF],
        out_specs=[headT(GQA_HEADS), keyF(GQA_KV_HEADS), valT(GQA_KV_HEADS)],
        out_shape=[jax.ShapeDtypeStruct((b, GQA_HEADS, HEAD_PAD, s), BF16),
                   jax.ShapeDtypeStruct((b, GQA_KV_HEADS, s, HEAD_PAD), BF16),
                   jax.ShapeDtypeStruct((b, GQA_KV_HEADS, n_t, HEAD_PAD, tile), BF16)],
        compiler_params=_cparams(2),
        name="proj_gqa",
    )(x, p["g_mix"], p["wqT_g"], p["wk_g"], p["wvT_g"], p["gq"], p["gk"],
      tabs["cT_g"], tabs["sT_g"], tabs["cF_g"], tabs["sF_g"])

    qm, km, vm = pl.pallas_call(
        _proj_mla_kernel,
        grid=grid,
        in_specs=[x_spec, _const_spec((1, D_MODEL)), _const_spec(p["wqaT"].shape), _const_spec((MLA_Q_LORA, 1)),
                  _const_spec(p["wqbT"].shape), _const_spec(p["wkva"].shape), _const_spec((1, MLA_KV_LORA)),
                  _const_spec(p["wkn"].shape), _const_spec(p["wkr2"].shape), _const_spec(p["wvT_m"].shape),
                  colT(MLA_ROPE // 2), colT(MLA_ROPE // 2), rowF, rowF],
        out_specs=[headT(MLA_HEADS), keyF(MLA_HEADS), valT(MLA_HEADS)],
        out_shape=[jax.ShapeDtypeStruct((b, MLA_HEADS, HEAD_PAD, s), BF16),
                   jax.ShapeDtypeStruct((b, MLA_HEADS, s, HEAD_PAD), BF16),
                   jax.ShapeDtypeStruct((b, MLA_HEADS, n_t, HEAD_PAD, tile), BF16)],
        compiler_params=_cparams(2),
        name="proj_mla",
    )(x, p["g_mix"], p["wqaT"], p["gqa"], p["wqbT"], p["wkva"], p["gkva"], p["wkn"], p["wkr2"], p["wvT_m"],
      tabs["cT_m"], tabs["sT_m"], tabs["cP_m"], tabs["sP_m"])

    featT = lambda rows: pl.BlockSpec((1, rows, tile), lambda bi, i: (bi, 0, i))
    qc, gates = pl.pallas_call(
        _proj_misc_kernel,
        grid=grid,
        in_specs=[x_spec, _const_spec((1, D_MODEL)), _const_spec(p["wmiscT"].shape)],
        out_specs=[featT(D_MODEL), featT(N_BRANCH * D_MODEL)],
        out_shape=[jax.ShapeDtypeStruct((b, D_MODEL, s), BF16),
                   jax.ShapeDtypeStruct((b, N_BRANCH * D_MODEL, s), BF16)],
        compiler_params=_cparams(2),
        name="proj_misc",
    )(x, p["g_mix"], p["wmiscT"])

    n_mem = mem.shape[1]
    mk, mvT = pl.pallas_call(
        _mem_kv_kernel,
        grid=(b,),
        in_specs=[pl.BlockSpec((1, n_mem, D_MODEL), lambda bi: (bi, 0, 0)), _const_spec((1, D_MODEL)),
                  _const_spec(p["wmk"].shape), _const_spec(p["wmvT"].shape)],
        out_specs=[pl.BlockSpec((1, n_mem, D_MODEL), lambda bi: (bi, 0, 0)),
                   pl.BlockSpec((1, D_MODEL, n_mem), lambda bi: (bi, 0, 0))],
        out_shape=[jax.ShapeDtypeStruct((b, n_mem, D_MODEL), BF16), jax.ShapeDtypeStruct((b, D_MODEL, n_mem), BF16)],
        compiler_params=_cparams(1),
        name="mem_kv",
    )(mem, p["g_mem"], p["wmk"], p["wmvT"])

    oa = _attention(qg, kg, vg, GQA_HEADS // GQA_KV_HEADS).reshape(b, D_MODEL, s)
    ob = _attention(qm, km, vm, 1).reshape(b, D_MODEL, s)
    oc = pl.pallas_call(
        _mem_attn_kernel,
        grid=grid,
        in_specs=[featT(D_MODEL), pl.BlockSpec((1, n_mem, D_MODEL), lambda bi, i: (bi, 0, 0)),
                  pl.BlockSpec((1, D_MODEL, n_mem), lambda bi, i: (bi, 0, 0))],
        out_specs=featT(D_MODEL),
        out_shape=jax.ShapeDtypeStruct((b, D_MODEL, s), BF16),
        compiler_params=_cparams(2),
        name="mem_attention",
    )(qc, mk, mvT)

    return pl.pallas_call(
        _merge_kernel,
        grid=grid,
        in_specs=[featT(D_MODEL), featT(D_MODEL), featT(D_MODEL), featT(N_BRANCH * D_MODEL), x_spec,
                  _const_spec((D_MODEL, D_MODEL))],
        out_specs=x_spec,
        out_shape=jax.ShapeDtypeStruct(x.shape, F32),
        compiler_params=_cparams(2),
        name="merge_out",
    )(oa, ob, oc, gates, x, p["w_out"])


def _trunk(x, mem, layers, g_final):
    b, s, d = x.shape
    tabs = _rope_tables(s)
    for l, p in enumerate(layers):
        x = _mixer(x, mem, p, tabs)
        x = _moe(x.reshape(b * s, d), p["g_ffn"], p["wrT"], p["br"], p["wgu"], p["wd"], g_final,
                 final_norm=(l == len(layers) - 1)).reshape(b, s, d)
    return x


def kernel(x_prompt, x_sample, mem_prompt, mem_sample, norm_mix, norm_mem, w_in, gqa_q_norm, gqa_k_norm, mla_q_a_norm, mla_kv_a_norm, mla_w_qb, mla_w_kvb, mem_w_kv, w_out, norm_ffn, w_group, b_group, w_expert, b_expert, w_gate_up, w_down, norm_final):
    depth = w_in.shape[0]
    layers = [_layer_params(l, norm_mix, norm_mem, w_in, gqa_q_norm, gqa_k_norm, mla_q_a_norm, mla_kv_a_norm,
                            mla_w_qb, mla_w_kvb, mem_w_kv, w_out, norm_ffn, w_group, b_group, w_expert, b_expert,
                            w_gate_up, w_down) for l in range(depth)]
    g_final = norm_final[None, :]
    return (_trunk(x_prompt, mem_prompt, layers, g_final), _trunk(x_sample, mem_sample, layers, g_final))
```

```python
import functools
import math

import jax
import jax.numpy as jnp
from jax import lax
from jax.experimental import pallas as pl
from jax.experimental.pallas import tpu as pltpu

F32 = jnp.float32
BF16 = jnp.bfloat16

D_MODEL = 1024
EPS = 1e-6
GRID_W = 64
ROPE_THETA = 10000.0
GQA_HEADS, GQA_KV_HEADS, GQA_HD = 8, 2, 128
MLA_HEADS, MLA_Q_LORA, MLA_KV_LORA = 8, 384, 256
MLA_NOPE, MLA_ROPE, MLA_V = 64, 32, 128
MEM_HEADS, MEM_HD = 4, 256
N_BRANCH = 3
N_GROUPS, EXPERTS_PER_GROUP, D_EXPERT = 8, 8, 256
N_EXPERTS = N_GROUPS * EXPERTS_PER_GROUP
HEAD_PAD = 128
LOG2E = 1.4426950408889634

TOK_TILE = 512
ATT_Q_COLS = 2048
MOE_ROWS = 256
ROUTER_LANES = 128
ROW_SUBLANES = D_MODEL // 128
DMA_UNROLL = 8
VMEM_LIMIT = 56 * 1024 * 1024


def _cparams(n_axes):
    return pltpu.CompilerParams(dimension_semantics=("arbitrary",) * n_axes, vmem_limit_bytes=VMEM_LIMIT)


def _nt(a, b):
    return lax.dot_general(a, b, (((1,), (1,)), ((), ())), preferred_element_type=F32)


def _rms_rows(x, g):
    return x * lax.rsqrt(jnp.mean(x * x, axis=-1, keepdims=True) + EPS) * g


def _rms_cols(x, g):
    return x * lax.rsqrt(jnp.mean(x * x, axis=0, keepdims=True) + EPS) * g


def _const_spec(shape):
    return pl.BlockSpec(shape, lambda *_: (0,) * len(shape))


def _proj_gqa_kernel(x_ref, g_ref, wqT_ref, wk_ref, wvT_ref, gq_ref, gk_ref, cT_ref, sT_ref, cF_ref, sF_ref,
                     q_ref, k_ref, v_ref):
    h = _rms_rows(x_ref[0], g_ref[...]).astype(BF16)
    qT = _nt(wqT_ref[...], h)
    c, s, gq = cT_ref[...], sT_ref[...], gq_ref[...]
    half = GQA_HD // 2
    scale = GQA_HD ** -0.5 * LOG2E
    for hd in range(GQA_HEADS):
        qn = _rms_cols(qT[hd * GQA_HD:(hd + 1) * GQA_HD], gq)
        x1, x2 = qn[:half], qn[half:]
        o = jnp.concatenate([x1 * c - x2 * s, x2 * c + x1 * s], axis=0) * scale
        q_ref[0, hd] = o.astype(BF16)
    k = jnp.dot(h, wk_ref[...], preferred_element_type=F32)
    for j in range(GQA_KV_HEADS):
        kn = _rms_rows(k[:, j * GQA_HD:(j + 1) * GQA_HD], gk_ref[...])
        ko = kn * cF_ref[...] + pltpu.roll(kn, half, 1) * sF_ref[...]
        k_ref[0, j] = ko.astype(BF16)
    vT = _nt(wvT_ref[...], h)
    for j in range(GQA_KV_HEADS):
        v_ref[0, j, 0] = vT[j * GQA_HD:(j + 1) * GQA_HD].astype(BF16)


def _proj_mla_kernel(x_ref, g_ref, wqaT_ref, gqa_ref, wqbT_ref, wkva_ref, gkva_ref, wkn_ref, wkr_ref, wvT_ref,
                     cT_ref, sT_ref, cP_ref, sP_ref, q_ref, k_ref, v_ref):
    h = _rms_rows(x_ref[0], g_ref[...]).astype(BF16)
    qa = _rms_cols(_nt(wqaT_ref[...], h), gqa_ref[...]).astype(BF16)
    qT = jnp.dot(wqbT_ref[...], qa, preferred_element_type=F32)
    c, s = cT_ref[...], sT_ref[...]
    hr = MLA_ROPE // 2
    scale = (MLA_NOPE + MLA_ROPE) ** -0.5 * LOG2E
    for hd in range(MLA_HEADS):
        q = qT[hd * HEAD_PAD:(hd + 1) * HEAD_PAD]
        x1, x2 = q[MLA_NOPE:MLA_NOPE + hr], q[MLA_NOPE + hr:MLA_NOPE + 2 * hr]
        o = jnp.concatenate([q[:MLA_NOPE], x1 * c - x2 * s, x2 * c + x1 * s, q[MLA_NOPE + 2 * hr:]], axis=0) * scale
        q_ref[0, hd] = o.astype(BF16)
    kva = jnp.dot(h, wkva_ref[...], preferred_element_type=F32)
    kvn = _rms_rows(kva, gkva_ref[...]).astype(BF16)
    kr2 = jnp.dot(h, wkr_ref[...], preferred_element_type=F32)
    kr = kr2[:, :HEAD_PAD] * cP_ref[...] + kr2[:, HEAD_PAD:] * sP_ref[...]
    kn = jnp.dot(kvn, wkn_ref[...], preferred_element_type=F32)
    for hd in range(MLA_HEADS):
        k_ref[0, hd] = (kn[:, hd * HEAD_PAD:(hd + 1) * HEAD_PAD] + kr).astype(BF16)
    vT = _nt(wvT_ref[...], kvn)
    for hd in range(MLA_HEADS):
        v_ref[0, hd, 0] = vT[hd * MLA_V:(hd + 1) * MLA_V].astype(BF16)


def _proj_misc_kernel(x_ref, g_ref, wT_ref, qc_ref, gate_ref):
    h = _rms_rows(x_ref[0], g_ref[...]).astype(BF16)
    scale = MEM_HD ** -0.5 * LOG2E
    rows = 512
    n_q = D_MODEL // rows
    for ci in range((D_MODEL + N_BRANCH * D_MODEL) // rows):
        y = _nt(wT_ref[ci * rows:(ci + 1) * rows, :], h)
        if ci < n_q:
            qc_ref[0, ci * rows:(ci + 1) * rows, :] = (y * scale).astype(BF16)
        else:
            r0 = (ci - n_q) * rows
            gate_ref[0, r0:r0 + rows, :] = jax.nn.sigmoid(y).astype(BF16)


def _mem_kv_kernel(mem_ref, g_ref, wk_ref, wvT_ref, mk_ref, mvT_ref):
    h = _rms_rows(mem_ref[0], g_ref[...]).astype(BF16)
    mk_ref[0] = jnp.dot(h, wk_ref[...], preferred_element_type=F32).astype(BF16)
    mvT_ref[0] = _nt(wvT_ref[...], h).astype(BF16)


def _attn_kernel(q_ref, k_ref, v_ref, o_ref, m_sc, l_sc, acc_sc, s_sc, mc_sc, *, heads_per_step, cols_per_head,
                 n_chunks):
    chunk = TOK_TILE
    subs = [(g, j * chunk) for g in range(heads_per_step) for j in range(cols_per_head // chunk)]

    m_sc[...] = jnp.full(m_sc.shape, -jnp.inf, F32)
    l_sc[...] = jnp.zeros(l_sc.shape, F32)
    acc_sc[...] = jnp.zeros(acc_sc.shape, F32)

    def score_stage(ci, slot):
        off = pl.multiple_of(ci * chunk, chunk)
        k = k_ref[0, 0, pl.ds(off, chunk), :]
        for u, (g, c0) in enumerate(subs):
            s = jnp.dot(k, q_ref[0, g, :, c0:c0 + chunk], preferred_element_type=F32)
            s_sc[slot, u] = s
            mc_sc[slot, u] = jnp.max(s, axis=0, keepdims=True)

    def softmax_stage(ci, slot):
        vT = v_ref[0, 0, ci]
        for u in range(len(subs)):
            m_prev = m_sc[u]
            m_cur = jnp.maximum(m_prev, mc_sc[slot, u])
            alpha = jnp.exp2(m_prev - m_cur)
            p = jnp.exp2(s_sc[slot, u] - m_cur)
            l_sc[u] = alpha * l_sc[u] + jnp.sum(p, axis=0, keepdims=True)
            acc_sc[u] = alpha * acc_sc[u] + jnp.dot(vT, p.astype(BF16), preferred_element_type=F32)
            m_sc[u] = m_cur

    score_stage(0, 0)
    if n_chunks > 1:
        assert n_chunks % 2 == 0

        def body(i, carry):
            c = 2 * i
            score_stage(c + 1, 1)
            softmax_stage(c, 0)
            score_stage(c + 2, 0)
            softmax_stage(c + 1, 1)
            return carry

        lax.fori_loop(0, n_chunks // 2 - 1, body, 0)
        score_stage(n_chunks - 1, 1)
        softmax_stage(n_chunks - 2, 0)
        softmax_stage(n_chunks - 1, 1)
    else:
        softmax_stage(0, 0)

    for u, (g, c0) in enumerate(subs):
        o_ref[0, g, :, c0:c0 + chunk] = (acc_sc[u] / l_sc[u]).astype(BF16)


def _attention(qT, k, vT, group):
    b, h, hd, s = qT.shape
    hkv = k.shape[1]
    assert h == hkv * group
    cols_per_head = min(ATT_Q_COLS // group, s)
    n_chunks = s // TOK_TILE
    n_sub = group * (cols_per_head // TOK_TILE)
    kern = functools.partial(_attn_kernel, heads_per_step=group, cols_per_head=cols_per_head, n_chunks=n_chunks)
    return pl.pallas_call(
        kern,
        grid=(b, hkv, s // cols_per_head),
        in_specs=[
            pl.BlockSpec((1, group, hd, cols_per_head), lambda bi, j, qi: (bi, j, 0, qi)),
            pl.BlockSpec((1, 1, s, hd), lambda bi, j, qi: (bi, j, 0, 0)),
            pl.BlockSpec((1, 1, n_chunks, hd, TOK_TILE), lambda bi, j, qi: (bi, j, 0, 0, 0)),
        ],
        out_specs=pl.BlockSpec((1, group, hd, cols_per_head), lambda bi, j, qi: (bi, j, 0, qi)),
        out_shape=jax.ShapeDtypeStruct(qT.shape, BF16),
        scratch_shapes=[
            pltpu.VMEM((n_sub, 1, TOK_TILE), F32),
            pltpu.VMEM((n_sub, 1, TOK_TILE), F32),
            pltpu.VMEM((n_sub, hd, TOK_TILE), F32),
            pltpu.VMEM((2, n_sub, TOK_TILE, TOK_TILE), F32),
            pltpu.VMEM((2, n_sub, 1, TOK_TILE), F32),
        ],
        compiler_params=_cparams(3),
        name="attention",
    )(qT, k, vT)


def _mem_attn_kernel(qc_ref, mk_ref, mvT_ref, o_ref):
    for hd in range(MEM_HEADS):
        r = slice(hd * MEM_HD, (hd + 1) * MEM_HD)
        s = jnp.dot(mk_ref[0, :, r], qc_ref[0, r, :], preferred_element_type=F32)
        p = jnp.exp2(s - jnp.max(s, axis=0, keepdims=True))
        l = jnp.sum(p, axis=0, keepdims=True)
        o = jnp.dot(mvT_ref[0, r, :], p.astype(BF16), preferred_element_type=F32)
        o_ref[0, r, :] = (o / l).astype(BF16)


def _merge_kernel(oa_ref, ob_ref, oc_ref, gate_ref, x_ref, w_ref, y_ref):
    merged = (gate_ref[0, 0:D_MODEL, :].astype(F32) * oa_ref[0].astype(F32)
              + gate_ref[0, D_MODEL:2 * D_MODEL, :].astype(F32) * ob_ref[0].astype(F32)
              + gate_ref[0, 2 * D_MODEL:, :].astype(F32) * oc_ref[0].astype(F32)).astype(BF16)
    out = lax.dot_general(merged, w_ref[...], (((0,), (0,)), ((), ())), preferred_element_type=F32)
    y_ref[0] = x_ref[0] + out


def _first_index_of_max(v, vmax, iota, n):
    return jnp.min(jnp.where(v == vmax, iota, n), axis=0, keepdims=True)


def _router_kernel(x_ref, g_ref, wT_ref, b_ref, ri_ref, rw_ref, cnt_ref, carry_sc):
    step = pl.program_id(0)

    @pl.when(step == 0)
    def _():
        carry_sc[...] = jnp.zeros(carry_sc.shape, F32)

    h = _rms_rows(x_ref[...], g_ref[...])
    t = h.shape[0]
    logits = lax.dot_general(wT_ref[...], h, (((1,), (1,)), ((), ())), precision=lax.Precision.HIGHEST,
                             preferred_element_type=F32) + b_ref[...]
    iota_g = lax.broadcasted_iota(jnp.int32, (N_GROUPS, t), 0)
    gl = logits[0:N_GROUPS]
    gmax = jnp.max(gl, axis=0, keepdims=True)
    gidx = _first_index_of_max(gl, gmax, iota_g, N_GROUPS)
    gw = 1.0 / jnp.sum(jnp.exp(gl - gmax), axis=0, keepdims=True)
    sel = jnp.zeros((EXPERTS_PER_GROUP, t), F32)
    for g in range(N_GROUPS):
        r0 = N_GROUPS + g * EXPERTS_PER_GROUP
        sel = sel + jnp.where(gidx == g, logits[r0:r0 + EXPERTS_PER_GROUP], 0.0)
    v1 = jnp.max(sel, axis=0, keepdims=True)
    i1 = _first_index_of_max(sel, v1, iota_g, EXPERTS_PER_GROUP)
    sel2 = jnp.where(iota_g == i1, -jnp.inf, sel)
    v2 = jnp.max(sel2, axis=0, keepdims=True)
    i2 = _first_index_of_max(sel2, v2, iota_g, EXPERTS_PER_GROUP)
    e = jnp.exp(v2 - v1)
    w1 = gw / (1.0 + e)
    w2 = gw * e / (1.0 + e)
    e1 = gidx * EXPERTS_PER_GROUP + i1
    e2 = gidx * EXPERTS_PER_GROUP + i2

    iota_e = lax.broadcasted_iota(jnp.int32, (N_EXPERTS, t), 0)
    hit1 = iota_e == e1
    hit2 = iota_e == e2
    hits = jnp.where(hit1 | hit2, 1.0, 0.0)
    before = (lax.broadcasted_iota(jnp.int32, (t, t), 0) < lax.broadcasted_iota(jnp.int32, (t, t), 1))
    prefix = jnp.dot(hits.astype(BF16), jnp.where(before, 1.0, 0.0).astype(BF16),
                     preferred_element_type=F32) + carry_sc[...]
    r1 = jnp.sum(jnp.where(hit1, prefix, 0.0), axis=0, keepdims=True)
    r2 = jnp.sum(jnp.where(hit2, prefix, 0.0), axis=0, keepdims=True)
    carry_sc[...] = carry_sc[...] + jnp.sum(hits, axis=1, keepdims=True)

    ri_ref[...] = jnp.zeros(ri_ref.shape, jnp.int32)
    ri_ref[0:1, :] = e1
    ri_ref[1:2, :] = e2
    ri_ref[2:3, :] = r1.astype(jnp.int32)
    ri_ref[3:4, :] = r2.astype(jnp.int32)
    rw_ref[...] = jnp.zeros(rw_ref.shape, F32)
    rw_ref[0:1, :] = w1
    rw_ref[1:2, :] = w2
    cnt_ref[...] = jnp.broadcast_to(carry_sc[...], cnt_ref.shape)


def _dest_kernel(cnt_ref, ri_ref, dest_ref, blk_ref, nused_ref, *, n_blocks_padded):
    cnt = cnt_ref[...]
    pad = jnp.floor((cnt + (MOE_ROWS - 1)) * (1.0 / MOE_ROWS)) * MOE_ROWS
    lower = jnp.where(lax.broadcasted_iota(jnp.int32, (N_EXPERTS, N_EXPERTS), 1)
                      <= lax.broadcasted_iota(jnp.int32, (N_EXPERTS, N_EXPERTS), 0), 1.0, 0.0)
    pad_end = jnp.dot(lower, pad, precision=lax.Precision.HIGHEST, preferred_element_type=F32)
    pad_start = (pad_end - pad)[:, 0:1]
    t = ri_ref.shape[1]
    iota_e = lax.broadcasted_iota(jnp.int32, (N_EXPERTS, t), 0)
    dest_ref[...] = jnp.zeros(dest_ref.shape, jnp.int32)
    for kk in range(2):
        start = jnp.sum(jnp.where(iota_e == ri_ref[kk:kk + 1, :], pad_start, 0.0), axis=0, keepdims=True)
        dest_ref[kk:kk + 1, :] = (start.astype(jnp.int32) + ri_ref[2 + kk:3 + kk, :]) * ROW_SUBLANES
    blk_start = (lax.broadcasted_iota(jnp.int32, (N_EXPERTS, n_blocks_padded), 1) * MOE_ROWS).astype(F32)
    owner = jnp.sum(jnp.where(pad_end[:, 0:1] <= blk_start, 1.0, 0.0), axis=0, keepdims=True)
    blk_ref[...] = jnp.broadcast_to(jnp.minimum(owner, N_EXPERTS - 1.0).astype(jnp.int32), blk_ref.shape)
    nused_ref[...] = jnp.broadcast_to((pad_end[N_EXPERTS - 1:N_EXPERTS, :] * (1.0 / MOE_ROWS)).astype(jnp.int32),
                                      nused_ref.shape)


def _to_row_tiles(ref, x):
    n = x.shape[0]
    for c in range(ROW_SUBLANES):
        ref[pl.ds(c, n, stride=ROW_SUBLANES), :] = x[:, c * 128:(c + 1) * 128]


def _from_row_tiles(ref, n):
    return jnp.concatenate([ref[pl.ds(c, n, stride=ROW_SUBLANES), :] for c in range(ROW_SUBLANES)], axis=1)


def _tile_copy(src_ref, src_row8, dst_ref, dst_row8, sem):
    return pltpu.make_async_copy(src_ref.at[pl.ds(pl.multiple_of(src_row8, ROW_SUBLANES), ROW_SUBLANES)],
                                 dst_ref.at[pl.ds(pl.multiple_of(dst_row8, ROW_SUBLANES), ROW_SUBLANES)], sem)


def _dispatch_kernel(dest_ref, x_ref, g_ref, xs_in_ref, xs_ref, h_sc, sem, *, tile):
    del xs_in_ref
    _to_row_tiles(h_sc, _rms_rows(x_ref[...], g_ref[...]))

    def issue(t, carry):
        _tile_copy(h_sc, t * ROW_SUBLANES, xs_ref, dest_ref[0, 0, t], sem).start()
        _tile_copy(h_sc, t * ROW_SUBLANES, xs_ref, dest_ref[0, 0, tile + t], sem).start()
        return carry

    def drain(t, carry):
        _tile_copy(h_sc, 0, xs_ref, 0, sem).wait()
        _tile_copy(h_sc, 0, xs_ref, 0, sem).wait()
        return carry

    lax.fori_loop(0, tile, issue, 0, unroll=DMA_UNROLL)
    lax.fori_loop(0, tile, drain, 0, unroll=DMA_UNROLL)


def _expert_kernel(blk_ref, nused_ref, xs_ref, wgu_ref, wd_ref, ys_ref):
    b = pl.program_id(0)

    @pl.when(b < nused_ref[0])
    def _():
        x = _from_row_tiles(xs_ref, MOE_ROWS).astype(BF16)
        gu = jnp.dot(x, wgu_ref[0], preferred_element_type=F32)
        a, u = gu[:, :D_EXPERT], gu[:, D_EXPERT:]
        hidden = (a * jax.nn.sigmoid(a) * u).astype(BF16)
        _to_row_tiles(ys_ref, jnp.dot(hidden, wd_ref[0], preferred_element_type=F32))

    @pl.when(b >= nused_ref[0])
    def _():
        ys_ref[...] = jnp.zeros(ys_ref.shape, F32)


def _combine_kernel(dest_ref, dest_next_ref, ys_ref, x_ref, w_ref, g_ref, y_ref, buf, sem, *, tile, final_norm):
    step = pl.program_id(0)
    slot = lax.rem(step, 2)

    def issue(d_ref, slot_):
        def body(t, carry):
            _tile_copy(ys_ref, d_ref[0, 0, t], buf.at[slot_, 0], t * ROW_SUBLANES, sem.at[slot_]).start()
            _tile_copy(ys_ref, d_ref[0, 0, tile + t], buf.at[slot_, 1], t * ROW_SUBLANES, sem.at[slot_]).start()
            return carry
        lax.fori_loop(0, tile, body, 0, unroll=DMA_UNROLL)

    @pl.when(step == 0)
    def _():
        issue(dest_ref, 0)

    @pl.when(step + 1 < pl.num_programs(0))
    def _():
        issue(dest_next_ref, 1 - slot)

    def drain(t, carry):
        _tile_copy(ys_ref, 0, buf.at[slot, 0], 0, sem.at[slot]).wait()
        _tile_copy(ys_ref, 0, buf.at[slot, 1], 0, sem.at[slot]).wait()
        return carry

    lax.fori_loop(0, tile, drain, 0, unroll=DMA_UNROLL)
    w = w_ref[...]
    y = x_ref[...] + (w[:, 0:1] * _from_row_tiles(buf.at[slot, 0], tile)
                      + w[:, 1:2] * _from_row_tiles(buf.at[slot, 1], tile))
    if final_norm:
        y = _rms_rows(y, g_ref[...])
    y_ref[...] = y


def _moe(x, g_ffn, wrT, br, wgu, wd, g_final, final_norm):
    n_tok = x.shape[0]
    tile = min(TOK_TILE, n_tok)
    n_tiles = n_tok // tile
    n_blocks = (2 * n_tok + MOE_ROWS - 1) // MOE_ROWS + N_EXPERTS
    n_blocks_padded = (n_blocks + 127) // 128 * 128
    n_rows = n_blocks * MOE_ROWS

    ri, rw, cnt = pl.pallas_call(
        _router_kernel,
        grid=(n_tiles,),
        in_specs=[
            pl.BlockSpec((tile, D_MODEL), lambda i: (i, 0)),
            _const_spec((1, D_MODEL)),
            _const_spec((ROUTER_LANES, D_MODEL)),
            _const_spec((ROUTER_LANES, 1)),
        ],
        out_specs=[
            pl.BlockSpec((8, tile), lambda i: (0, i)),
            pl.BlockSpec((8, tile), lambda i: (0, i)),
            _const_spec((N_EXPERTS, 128)),
        ],
        out_shape=[
            jax.ShapeDtypeStruct((8, n_tok), jnp.int32),
            jax.ShapeDtypeStruct((8, n_tok), F32),
            jax.ShapeDtypeStruct((N_EXPERTS, 128), F32),
        ],
        scratch_shapes=[pltpu.VMEM((N_EXPERTS, 1), F32)],
        compiler_params=_cparams(1),
        name="moe_router",
    )(x, g_ffn, wrT, br)

    dest, blk, nused = pl.pallas_call(
        functools.partial(_dest_kernel, n_blocks_padded=n_blocks_padded),
        grid=(n_tiles,),
        in_specs=[_const_spec((N_EXPERTS, 128)), pl.BlockSpec((8, tile), lambda i: (0, i))],
        out_specs=[
            pl.BlockSpec((8, tile), lambda i: (0, i)),
            _const_spec((8, n_blocks_padded)),
            _const_spec((8, 128)),
        ],
        out_shape=[
            jax.ShapeDtypeStruct((8, n_tok), jnp.int32),
            jax.ShapeDtypeStruct((8, n_blocks_padded), jnp.int32),
            jax.ShapeDtypeStruct((8, 128), jnp.int32),
        ],
        compiler_params=_cparams(1),
        name="moe_dest",
    )(cnt, ri)

    dest_tiles = dest[0:2].reshape(2, n_tiles, tile).transpose(1, 0, 2).reshape(n_tiles, 1, 2 * tile)
    dest_spec = pl.BlockSpec((1, 1, 2 * tile), lambda i: (i, 0, 0), memory_space=pltpu.SMEM)
    any_spec = pl.BlockSpec(memory_space=pl.ANY)

    xs = pl.pallas_call(
        functools.partial(_dispatch_kernel, tile=tile),
        grid=(n_tiles,),
        in_specs=[dest_spec, pl.BlockSpec((tile, D_MODEL), lambda i: (i, 0)), _const_spec((1, D_MODEL)), any_spec],
        out_specs=any_spec,
        out_shape=jax.ShapeDtypeStruct((n_rows * ROW_SUBLANES, 128), F32),
        scratch_shapes=[pltpu.VMEM((tile * ROW_SUBLANES, 128), F32), pltpu.SemaphoreType.DMA(())],
        input_output_aliases={3: 0},
        compiler_params=_cparams(1),
        name="moe_dispatch",
    )(dest_tiles, x, g_ffn, jnp.zeros((n_rows * ROW_SUBLANES, 128), F32))

    blk_rows = MOE_ROWS * ROW_SUBLANES
    ys = pl.pallas_call(
        _expert_kernel,
        grid_spec=pltpu.PrefetchScalarGridSpec(
            num_scalar_prefetch=2,
            grid=(n_blocks,),
            in_specs=[
                pl.BlockSpec((blk_rows, 128), lambda b, blk, nu: (b, 0)),
                pl.BlockSpec((1, D_MODEL, 2 * D_EXPERT), lambda b, blk, nu: (blk[b], 0, 0)),
                pl.BlockSpec((1, D_EXPERT, D_MODEL), lambda b, blk, nu: (blk[b], 0, 0)),
            ],
            out_specs=pl.BlockSpec((blk_rows, 128), lambda b, blk, nu: (b, 0)),
        ),
        out_shape=jax.ShapeDtypeStruct((n_rows * ROW_SUBLANES, 128), F32),
        compiler_params=_cparams(1),
        name="moe_experts",
    )(blk[0, :n_blocks], nused[0, :1], xs, wgu, wd)

    ctile = min(256, n_tok)
    c_tiles = n_tok // ctile
    dest_c = dest[0:2].reshape(2, c_tiles, ctile).transpose(1, 0, 2).reshape(c_tiles, 1, 2 * ctile)
    return pl.pallas_call(
        functools.partial(_combine_kernel, tile=ctile, final_norm=final_norm),
        grid=(c_tiles,),
        in_specs=[
            pl.BlockSpec((1, 1, 2 * ctile), lambda i: (i, 0, 0), memory_space=pltpu.SMEM),
            pl.BlockSpec((1, 1, 2 * ctile), lambda i: (jnp.minimum(i + 1, c_tiles - 1), 0, 0),
                         memory_space=pltpu.SMEM),
            any_spec,
            pl.BlockSpec((ctile, D_MODEL), lambda i: (i, 0)),
            pl.BlockSpec((ctile, 2), lambda i: (i, 0)),
            _const_spec((1, D_MODEL)),
        ],
        out_specs=pl.BlockSpec((ctile, D_MODEL), lambda i: (i, 0)),
        out_shape=jax.ShapeDtypeStruct((n_tok, D_MODEL), F32),
        scratch_shapes=[pltpu.VMEM((2, 2, ctile * ROW_SUBLANES, 128), F32), pltpu.SemaphoreType.DMA((2,))],
        compiler_params=_cparams(1),
        name="moe_combine",
    )(dest_c, dest_c, ys, x, rw[0:2].T, g_final)


def _rope_tables(n_tokens):
    rows = n_tokens // GRID_W
    row_idx = jnp.repeat(jnp.arange(rows, dtype=F32), GRID_W)
    col_idx = jnp.tile(jnp.arange(GRID_W, dtype=F32), rows)

    def angles(rot_dim):
        n_freq = rot_dim // 4
        inv_freq = ROPE_THETA ** (-jnp.arange(n_freq, dtype=F32) / n_freq)
        return jnp.concatenate([row_idx[:, None] * inv_freq, col_idx[:, None] * inv_freq], axis=-1)

    ag, am = angles(GQA_HD), angles(MLA_ROPE)
    cg, sg, cm, sm = jnp.cos(ag), jnp.sin(ag), jnp.cos(am), jnp.sin(am)
    hr = MLA_ROPE // 2
    zl = jnp.zeros((n_tokens, MLA_NOPE), F32)
    zr = jnp.zeros((n_tokens, HEAD_PAD - MLA_NOPE - MLA_ROPE), F32)
    del hr
    return dict(
        cT_g=cg.T, sT_g=sg.T,
        cF_g=jnp.concatenate([cg, cg], axis=1), sF_g=jnp.concatenate([-sg, sg], axis=1),
        cT_m=cm.T, sT_m=sm.T,
        cP_m=jnp.concatenate([zl, cm, cm, zr], axis=1), sP_m=jnp.concatenate([zl, -sm, sm, zr], axis=1),
    )


def _layer_params(l, norm_mix, norm_mem, w_in, gqa_q_norm, gqa_k_norm, mla_q_a_norm, mla_kv_a_norm, mla_w_qb,
                  mla_w_kvb, mem_w_kv, w_out, norm_ffn, w_group, b_group, w_expert, b_expert, w_gate_up, w_down):
    splits = []
    off = 0
    for width in (GQA_HEADS * GQA_HD, GQA_KV_HEADS * GQA_HD, GQA_KV_HEADS * GQA_HD, MLA_Q_LORA, MLA_KV_LORA,
                  MLA_ROPE, MEM_HEADS * MEM_HD, N_BRANCH * D_MODEL):
        splits.append(w_in[l][:, off:off + width])
        off += width
    wq_g, wk_g, wv_g, wq_a, wkv_a, wk_r, wq_c, w_gate = splits
    hr = MLA_ROPE // 2
    pad_l = jnp.zeros((D_MODEL, MLA_NOPE), F32)
    pad_r = jnp.zeros((D_MODEL, HEAD_PAD - MLA_NOPE - MLA_ROPE), F32)
    wk_r_swapped = jnp.concatenate([wk_r[:, hr:], wk_r[:, :hr]], axis=1)
    wkr2 = jnp.concatenate([pad_l, wk_r, pad_r, pad_l, wk_r_swapped, pad_r], axis=1)
    qk = MLA_NOPE + MLA_ROPE
    wqb = jnp.pad(mla_w_qb[l].reshape(MLA_Q_LORA, MLA_HEADS, qk), ((0, 0), (0, 0), (0, HEAD_PAD - qk)))
    wkvb = mla_w_kvb[l].reshape(MLA_KV_LORA, MLA_HEADS, MLA_NOPE + MLA_V)
    wkn = jnp.pad(wkvb[:, :, :MLA_NOPE], ((0, 0), (0, 0), (0, HEAD_PAD - MLA_NOPE)))
    wv_m = wkvb[:, :, MLA_NOPE:]
    n_router = N_GROUPS + N_EXPERTS
    return dict(
        g_mix=norm_mix[l][None, :], g_mem=norm_mem[l][None, :], g_ffn=norm_ffn[l][None, :],
        wqT_g=wq_g.T.astype(BF16), wk_g=wk_g.astype(BF16), wvT_g=wv_g.T.astype(BF16),
        gq=gqa_q_norm[l][:, None], gk=gqa_k_norm[l][None, :],
        wqaT=wq_a.T.astype(BF16), gqa=mla_q_a_norm[l][:, None],
        wqbT=wqb.reshape(MLA_Q_LORA, MLA_HEADS * HEAD_PAD).T.astype(BF16),
        wkva=wkv_a.astype(BF16), gkva=mla_kv_a_norm[l][None, :],
        wkn=wkn.reshape(MLA_KV_LORA, MLA_HEADS * HEAD_PAD).astype(BF16), wkr2=wkr2.astype(BF16),
        wvT_m=wv_m.reshape(MLA_KV_LORA, MLA_HEADS * MLA_V).T.astype(BF16),
        wmiscT=jnp.concatenate([wq_c, w_gate], axis=1).T.astype(BF16),
        wmk=mem_w_kv[l][:, :MEM_HEADS * MEM_HD].astype(BF16),
        wmvT=mem_w_kv[l][:, MEM_HEADS * MEM_HD:].T.astype(BF16),
        w_out=w_out[l].astype(BF16),
        wrT=jnp.concatenate([w_group[l].T, w_expert[l].T, jnp.zeros((ROUTER_LANES - n_router, D_MODEL), F32)], axis=0),
        br=jnp.concatenate([b_group[l], b_expert[l], jnp.zeros((ROUTER_LANES - n_router,), F32)])[:, None],
        wgu=w_gate_up[l].astype(BF16), wd=w_down[l].astype(BF16),
    )


def _mixer(x, mem, p, tabs):
    b, s, _ = x.shape
    tile = min(TOK_TILE, s)
    n_t = s // tile
    grid = (b, n_t)
    x_spec = pl.BlockSpec((1, tile, D_MODEL), lambda bi, i: (bi, i, 0))
    colT = lambda rows: pl.BlockSpec((rows, tile), lambda bi, i: (0, i))
    rowF = pl.BlockSpec((tile, HEAD_PAD), lambda bi, i: (i, 0))
    headT = lambda heads: pl.BlockSpec((1, heads, HEAD_PAD, tile), lambda bi, i: (bi, 0, 0, i))
    keyF = lambda heads: pl.BlockSpec((1, heads, tile, HEAD_PAD), lambda bi, i: (bi, 0, i, 0))
    valT = lambda heads: pl.BlockSpec((1, heads, 1, HEAD_PAD, tile), lambda bi, i: (bi, 0, i, 0, 0))

    qg, kg, vg = pl.pallas_call(
        _proj_gqa_kernel,
        grid=grid,
        in_specs=[x_spec, _const_spec((1, D_MODEL)), _const_spec(p["wqT_g"].shape), _const_spec(p["wk_g"].shape),
                  _const_spec(p["wvT_g"].shape), _const_spec((GQA_HD, 1)), _const_spec((1, GQA_HD)),
                  colT(GQA_HD // 2), colT(GQA_HD // 2), rowF, rowF],
        out_specs=[headT(GQA_HEADS), keyF(GQA_KV_HEADS), valT(GQA_KV_HEADS)],
        out_shape=[jax.ShapeDtypeStruct((b, GQA_HEADS, HEAD_PAD, s), BF16),
                   jax.ShapeDtypeStruct((b, GQA_KV_HEADS, s, HEAD_PAD), BF16),
                   jax.ShapeDtypeStruct((b, GQA_KV_HEADS, n_t, HEAD_PAD, tile), BF16)],
        compiler_params=_cparams(2),
        name="proj_gqa",
    )(x, p["g_mix"], p["wqT_g"], p["wk_g"], p["wvT_g"], p["gq"], p["gk"],
      tabs["cT_g"], tabs["sT_g"], tabs["cF_g"], tabs["sF_g"])

    qm, km, vm = pl.pallas_call(
        _proj_mla_kernel,
        grid=grid,
        in_specs=[x_spec, _const_spec((1, D_MODEL)), _const_spec(p["wqaT"].shape), _const_spec((MLA_Q_LORA, 1)),
                  _const_spec(p["wqbT"].shape), _const_spec(p["wkva"].shape), _const_spec((1, MLA_KV_LORA)),
                  _const_spec(p["wkn"].shape), _const_spec(p["wkr2"].shape), _const_spec(p["wvT_m"].shape),
                  colT(MLA_ROPE // 2), colT(MLA_ROPE // 2), rowF, rowF],
        out_specs=[headT(MLA_HEADS), keyF(MLA_HEADS), valT(MLA_HEADS)],
        out_shape=[jax.ShapeDtypeStruct((b, MLA_HEADS, HEAD_PAD, s), BF16),
                   jax.ShapeDtypeStruct((b, MLA_HEADS, s, HEAD_PAD), BF16),
                   jax.ShapeDtypeStruct((b, MLA_HEADS, n_t, HEAD_PAD, tile), BF16)],
        compiler_params=_cparams(2),
        name="proj_mla",
    )(x, p["g_mix"], p["wqaT"], p["gqa"], p["wqbT"], p["wkva"], p["gkva"], p["wkn"], p["wkr2"], p["wvT_m"],
      tabs["cT_m"], tabs["sT_m"], tabs["cP_m"], tabs["sP_m"])

    featT = lambda rows: pl.BlockSpec((1, rows, tile), lambda bi, i: (bi, 0, i))
    qc, gates = pl.pallas_call(
        _proj_misc_kernel,
        grid=grid,
        in_specs=[x_spec, _const_spec((1, D_MODEL)), _const_spec(p["wmiscT"].shape)],
        out_specs=[featT(D_MODEL), featT(N_BRANCH * D_MODEL)],
        out_shape=[jax.ShapeDtypeStruct((b, D_MODEL, s), BF16),
                   jax.ShapeDtypeStruct((b, N_BRANCH * D_MODEL, s), BF16)],
        compiler_params=_cparams(2),
        name="proj_misc",
    )(x, p["g_mix"], p["wmiscT"])

    n_mem = mem.shape[1]
    mk, mvT = pl.pallas_call(
        _mem_kv_kernel,
        grid=(b,),
        in_specs=[pl.BlockSpec((1, n_mem, D_MODEL), lambda bi: (bi, 0, 0)), _const_spec((1, D_MODEL)),
                  _const_spec(p["wmk"].shape), _const_spec(p["wmvT"].shape)],
        out_specs=[pl.BlockSpec((1, n_mem, D_MODEL), lambda bi: (bi, 0, 0)),
                   pl.BlockSpec((1, D_MODEL, n_mem), lambda bi: (bi, 0, 0))],
        out_shape=[jax.ShapeDtypeStruct((b, n_mem, D_MODEL), BF16), jax.ShapeDtypeStruct((b, D_MODEL, n_mem), BF16)],
        compiler_params=_cparams(1),
        name="mem_kv",
    )(mem, p["g_mem"], p["wmk"], p["wmvT"])

    oa = _attention(qg, kg, vg, GQA_HEADS // GQA_KV_HEADS).reshape(b, D_MODEL, s)
    ob = _attention(qm, km, vm, 1).reshape(b, D_MODEL, s)
    oc = pl.pallas_call(
        _mem_attn_kernel,
        grid=grid,
        in_specs=[featT(D_MODEL), pl.BlockSpec((1, n_mem, D_MODEL), lambda bi, i: (bi, 0, 0)),
                  pl.BlockSpec((1, D_MODEL, n_mem), lambda bi, i: (bi, 0, 0))],
        out_specs=featT(D_MODEL),
        out_shape=jax.ShapeDtypeStruct((b, D_MODEL, s), BF16),
        compiler_params=_cparams(2),
        name="mem_attention",
    )(qc, mk, mvT)

    return pl.pallas_call(
        _merge_kernel,
        grid=grid,
        in_specs=[featT(D_MODEL), featT(D_MODEL), featT(D_MODEL), featT(N_BRANCH * D_MODEL), x_spec,
                  _const_spec((D_MODEL, D_MODEL))],
        out_specs=x_spec,
        out_shape=jax.ShapeDtypeStruct(x.shape, F32),
        compiler_params=_cparams(2),
        name="merge_out",
    )(oa, ob, oc, gates, x, p["w_out"])


def _trunk(x, mem, layers, g_final):
    b, s, d = x.shape
    tabs = _rope_tables(s)
    for l, p in enumerate(layers):
        x = _mixer(x, mem, p, tabs)
        x = _moe(x.reshape(b * s, d), p["g_ffn"], p["wrT"], p["br"], p["wgu"], p["wd"], g_final,
                 final_norm=(l == len(layers) - 1)).reshape(b, s, d)
    return x


def kernel(x_prompt, x_sample, mem_prompt, mem_sample, norm_mix, norm_mem, w_in, gqa_q_norm, gqa_k_norm, mla_q_a_norm, mla_kv_a_norm, mla_w_qb, mla_w_kvb, mem_w_kv, w_out, norm_ffn, w_group, b_group, w_expert, b_expert, w_gate_up, w_down, norm_final):
    depth = w_in.shape[0]
    layers = [_layer_params(l, norm_mix, norm_mem, w_in, gqa_q_norm, gqa_k_norm, mla_q_a_norm, mla_kv_a_norm,
                            mla_w_qb, mla_w_kvb, mem_w_kv, w_out, norm_ffn, w_group, b_group, w_expert, b_expert,
                            w_gate_up, w_down) for l in range(depth)]
    g_final = norm_final[None, :]
    return (_trunk(x_prompt, mem_prompt, layers, g_final), _trunk(x_sample, mem_sample, layers, g_final))
```

```python
import functools
import math

import jax
import jax.numpy as jnp
from jax import lax
from jax.experimental import pallas as pl
from jax.experimental.pallas import tpu as pltpu

F32 = jnp.float32
BF16 = jnp.bfloat16

D_MODEL = 1024
EPS = 1e-6
GRID_W = 64
ROPE_THETA = 10000.0
GQA_HEADS, GQA_KV_HEADS, GQA_HD = 8, 2, 128
MLA_HEADS, MLA_Q_LORA, MLA_KV_LORA = 8, 384, 256
MLA_NOPE, MLA_ROPE, MLA_V = 64, 32, 128
MEM_HEADS, MEM_HD = 4, 256
N_BRANCH = 3
N_GROUPS, EXPERTS_PER_GROUP, D_EXPERT = 8, 8, 256
N_EXPERTS = N_GROUPS * EXPERTS_PER_GROUP
HEAD_PAD = 128
BF16_SUBLANES = 16
V_ROWS = HEAD_PAD + BF16_SUBLANES
LOG2E = 1.4426950408889634

TOK_TILE = 512
ATT_Q_COLS = 2048
ATT_CHUNKS_PER_BODY = 4
MOE_ROWS = 256
ROUTER_LANES = 128
ROW_SUBLANES = D_MODEL // 128
DMA_UNROLL = 8
VMEM_LIMIT = 56 * 1024 * 1024


def _cparams(n_axes, flags=None):
    return pltpu.CompilerParams(dimension_semantics=("arbitrary",) * n_axes, vmem_limit_bytes=VMEM_LIMIT,
                                flags=flags)


def _nt(a, b):
    return lax.dot_general(a, b, (((1,), (1,)), ((), ())), preferred_element_type=F32)


def _rms_rows(x, g):
    return x * lax.rsqrt(jnp.mean(x * x, axis=-1, keepdims=True) + EPS) * g


def _rms_cols(x, g):
    return x * lax.rsqrt(jnp.mean(x * x, axis=0, keepdims=True) + EPS) * g


def _const_spec(shape):
    return pl.BlockSpec(shape, lambda *_: (0,) * len(shape))


def _store_values(v_ref, vT):
    t = vT.shape[1]
    v_ref[0:HEAD_PAD, :] = vT.astype(BF16)
    ones_row = lax.broadcasted_iota(jnp.int32, (BF16_SUBLANES, t), 0) == 0
    v_ref[HEAD_PAD:V_ROWS, :] = jnp.where(ones_row, 1.0, 0.0).astype(BF16)


def _proj_gqa_kernel(x_ref, g_ref, wqT_ref, wk_ref, wvT_ref, gq_ref, gk_ref, cT_ref, sT_ref, cF_ref, sF_ref,
                     q_ref, k_ref, v_ref):
    h = _rms_rows(x_ref[0], g_ref[...]).astype(BF16)
    qT = _nt(wqT_ref[...], h)
    c, s, gq = cT_ref[...], sT_ref[...], gq_ref[...]
    half = GQA_HD // 2
    scale = GQA_HD ** -0.5 * LOG2E
    for hd in range(GQA_HEADS):
        qn = _rms_cols(qT[hd * GQA_HD:(hd + 1) * GQA_HD], gq)
        x1, x2 = qn[:half], qn[half:]
        o = jnp.concatenate([x1 * c - x2 * s, x2 * c + x1 * s], axis=0) * scale
        q_ref[0, hd, 0] = o.astype(BF16)
    k = jnp.dot(h, wk_ref[...], preferred_element_type=F32)
    for j in range(GQA_KV_HEADS):
        kn = _rms_rows(k[:, j * GQA_HD:(j + 1) * GQA_HD], gk_ref[...])
        ko = kn * cF_ref[...] + pltpu.roll(kn, half, 1) * sF_ref[...]
        k_ref[0, j] = ko.astype(BF16)
    vT = _nt(wvT_ref[...], h)
    for j in range(GQA_KV_HEADS):
        _store_values(v_ref.at[0, j, 0], vT[j * GQA_HD:(j + 1) * GQA_HD])


def _proj_mla_kernel(x_ref, g_ref, wqaT_ref, gqa_ref, wqbT_ref, wkva_ref, gkva_ref, wkn_ref, wkr_ref, wvT_ref,
                     cT_ref, sT_ref, cP_ref, sP_ref, q_ref, k_ref, v_ref):
    h = _rms_rows(x_ref[0], g_ref[...]).astype(BF16)
    qa = _rms_cols(_nt(wqaT_ref[...], h), gqa_ref[...]).astype(BF16)
    qT = jnp.dot(wqbT_ref[...], qa, preferred_element_type=F32)
    c, s = cT_ref[...], sT_ref[...]
    hr = MLA_ROPE // 2
    scale = (MLA_NOPE + MLA_ROPE) ** -0.5 * LOG2E
    for hd in range(MLA_HEADS):
        q = qT[hd * HEAD_PAD:(hd + 1) * HEAD_PAD]
        x1, x2 = q[MLA_NOPE:MLA_NOPE + hr], q[MLA_NOPE + hr:MLA_NOPE + 2 * hr]
        o = jnp.concatenate([q[:MLA_NOPE], x1 * c - x2 * s, x2 * c + x1 * s, q[MLA_NOPE + 2 * hr:]], axis=0) * scale
        q_ref[0, hd, 0] = o.astype(BF16)
    kva = jnp.dot(h, wkva_ref[...], preferred_element_type=F32)
    kvn = _rms_rows(kva, gkva_ref[...]).astype(BF16)
    kr2 = jnp.dot(h, wkr_ref[...], preferred_element_type=F32)
    kr = kr2[:, :HEAD_PAD] * cP_ref[...] + kr2[:, HEAD_PAD:] * sP_ref[...]
    kn = jnp.dot(kvn, wkn_ref[...], preferred_element_type=F32)
    for hd in range(MLA_HEADS):
        k_ref[0, hd] = (kn[:, hd * HEAD_PAD:(hd + 1) * HEAD_PAD] + kr).astype(BF16)
    vT = _nt(wvT_ref[...], kvn)
    for hd in range(MLA_HEADS):
        _store_values(v_ref.at[0, hd, 0], vT[hd * MLA_V:(hd + 1) * MLA_V])


def _proj_misc_kernel(x_ref, g_ref, wT_ref, qc_ref, gate_ref):
    h = _rms_rows(x_ref[0], g_ref[...]).astype(BF16)
    scale = MEM_HD ** -0.5 * LOG2E
    rows = 512
    n_q = D_MODEL // rows
    for ci in range((D_MODEL + N_BRANCH * D_MODEL) // rows):
        y = _nt(wT_ref[ci * rows:(ci + 1) * rows, :], h)
        if ci < n_q:
            qc_ref[0, ci * rows:(ci + 1) * rows, :] = (y * scale).astype(BF16)
        else:
            r0 = (ci - n_q) * rows
            gate_ref[0, r0:r0 + rows, :] = jax.nn.sigmoid(y).astype(BF16)


def _mem_kv_kernel(mem_ref, g_ref, wk_ref, wvT_ref, mk_ref, mvT_ref):
    h = _rms_rows(mem_ref[0], g_ref[...]).astype(BF16)
    mk_ref[0] = jnp.dot(h, wk_ref[...], preferred_element_type=F32).astype(BF16)
    mvT_ref[0] = _nt(wvT_ref[...], h).astype(BF16)


def _attn_kernel(q_ref, k_ref, v_ref, o_ref, m_sc, acc_sc, s_sc, mc_sc, *, chunks_per_head, n_kv):
    chunk = TOK_TILE
    n_sub = m_sc.shape[0]

    m_sc[...] = jnp.full(m_sc.shape, -jnp.inf, F32)
    acc_sc[...] = jnp.zeros(acc_sc.shape, F32)

    def score_stage(c, slot):
        k = k_ref[0, 0, pl.ds(pl.multiple_of(c * chunk, chunk), chunk), :]
        for u in range(n_sub):
            s = jnp.dot(k, q_ref[0, u // chunks_per_head, u % chunks_per_head],
                        preferred_element_type=F32)
            s_sc[slot, u] = s
            mc_sc[slot, u] = jnp.max(s, axis=0, keepdims=True)

    def softmax_stage(c, slot):
        vT = v_ref[0, 0, c]
        for u in range(n_sub):
            m_prev = m_sc[u]
            m_cur = jnp.maximum(m_prev, mc_sc[slot, u])
            alpha = jnp.exp2(m_prev - m_cur)
            p = jnp.exp2(s_sc[slot, u] - m_cur)
            acc_sc[u] = alpha * acc_sc[u] + jnp.dot(vT, p.astype(BF16), preferred_element_type=F32)
            m_sc[u] = m_cur

    per_body = ATT_CHUNKS_PER_BODY
    n_bodies = (n_kv - 1) // per_body
    score_stage(0, 0)

    def body(i, carry):
        c = per_body * i
        for t in range(per_body):
            score_stage(c + t + 1, (t + 1) % 2)
            softmax_stage(c + t, t % 2)
        return carry

    lax.fori_loop(0, n_bodies, body, 0)
    for c in range(per_body * n_bodies, n_kv - 1):
        score_stage(c + 1, (c + 1) % 2)
        softmax_stage(c, c % 2)
    softmax_stage(n_kv - 1, (n_kv - 1) % 2)

    for u in range(n_sub):
        o_ref[0, u // chunks_per_head, u % chunks_per_head] = (
            acc_sc[u, 0:HEAD_PAD] / acc_sc[u, HEAD_PAD:HEAD_PAD + 1]).astype(BF16)


def _attention(qT, k, vT, group):
    b, h, n_kv, hd, chunk = qT.shape
    hkv = k.shape[1]
    s = n_kv * chunk
    assert h == hkv * group and chunk == TOK_TILE
    chunks_per_head = min(ATT_Q_COLS // group, s) // chunk
    n_sub = group * chunks_per_head
    kern = functools.partial(_attn_kernel, chunks_per_head=chunks_per_head, n_kv=n_kv)
    q_spec = pl.BlockSpec((1, group, chunks_per_head, hd, chunk), lambda bi, j, qi: (bi, j, qi, 0, 0))
    return pl.pallas_call(
        kern,
        grid=(b, hkv, n_kv // chunks_per_head),
        in_specs=[
            q_spec,
            pl.BlockSpec((1, 1, s, hd), lambda bi, j, qi: (bi, j, 0, 0)),
            pl.BlockSpec((1, 1, n_kv, V_ROWS, chunk), lambda bi, j, qi: (bi, j, 0, 0, 0)),
        ],
        out_specs=q_spec,
        out_shape=jax.ShapeDtypeStruct(qT.shape, BF16),
        scratch_shapes=[
            pltpu.VMEM((n_sub, 1, chunk), F32),
            pltpu.VMEM((n_sub, V_ROWS, chunk), F32),
            pltpu.VMEM((2, n_sub, chunk, chunk), F32),
            pltpu.VMEM((2, n_sub, 1, chunk), F32),
        ],
        compiler_params=_cparams(3),
        name="attention",
    )(qT, k, vT)


def _mem_attn_kernel(qc_ref, mk_ref, mvT_ref, o_ref):
    for hd in range(MEM_HEADS):
        r = slice(hd * MEM_HD, (hd + 1) * MEM_HD)
        s = jnp.dot(mk_ref[0, :, r], qc_ref[0, r, :], preferred_element_type=F32)
        p = jnp.exp2(s - jnp.max(s, axis=0, keepdims=True))
        l = jnp.sum(p, axis=0, keepdims=True)
        o = jnp.dot(mvT_ref[0, r, :], p.astype(BF16), preferred_element_type=F32)
        o_ref[0, r, :] = (o / l).astype(BF16)


def _merge_kernel(oa_ref, ob_ref, oc_ref, gate_ref, x_ref, w_ref, y_ref):
    t = x_ref.shape[1]
    oa = oa_ref[0].reshape(D_MODEL, t)
    ob = ob_ref[0].reshape(D_MODEL, t)
    merged = (gate_ref[0, 0:D_MODEL, :].astype(F32) * oa.astype(F32)
              + gate_ref[0, D_MODEL:2 * D_MODEL, :].astype(F32) * ob.astype(F32)
              + gate_ref[0, 2 * D_MODEL:, :].astype(F32) * oc_ref[0].astype(F32)).astype(BF16)
    out = lax.dot_general(merged, w_ref[...], (((0,), (0,)), ((), ())), preferred_element_type=F32)
    y_ref[0] = x_ref[0] + out


def _first_index_of_max(v, vmax, iota, n):
    return jnp.min(jnp.where(v == vmax, iota, n), axis=0, keepdims=True)


def _router_kernel(x_ref, g_ref, wT_ref, b_ref, ri_ref, rw_ref, cnt_ref, carry_sc):
    step = pl.program_id(0)

    @pl.when(step == 0)
    def _():
        carry_sc[...] = jnp.zeros(carry_sc.shape, F32)

    h = _rms_rows(x_ref[...], g_ref[...])
    t = h.shape[0]
    logits = lax.dot_general(wT_ref[...], h, (((1,), (1,)), ((), ())), precision=lax.Precision.HIGHEST,
                             preferred_element_type=F32) + b_ref[...]
    iota_g = lax.broadcasted_iota(jnp.int32, (N_GROUPS, t), 0)
    gl = logits[0:N_GROUPS]
    gmax = jnp.max(gl, axis=0, keepdims=True)
    gidx = _first_index_of_max(gl, gmax, iota_g, N_GROUPS)
    gw = 1.0 / jnp.sum(jnp.exp(gl - gmax), axis=0, keepdims=True)
    sel = jnp.zeros((EXPERTS_PER_GROUP, t), F32)
    for g in range(N_GROUPS):
        r0 = N_GROUPS + g * EXPERTS_PER_GROUP
        sel = sel + jnp.where(gidx == g, logits[r0:r0 + EXPERTS_PER_GROUP], 0.0)
    v1 = jnp.max(sel, axis=0, keepdims=True)
    i1 = _first_index_of_max(sel, v1, iota_g, EXPERTS_PER_GROUP)
    sel2 = jnp.where(iota_g == i1, -jnp.inf, sel)
    v2 = jnp.max(sel2, axis=0, keepdims=True)
    i2 = _first_index_of_max(sel2, v2, iota_g, EXPERTS_PER_GROUP)
    e = jnp.exp(v2 - v1)
    w1 = gw / (1.0 + e)
    w2 = gw * e / (1.0 + e)
    e1 = gidx * EXPERTS_PER_GROUP + i1
    e2 = gidx * EXPERTS_PER_GROUP + i2

    iota_e = lax.broadcasted_iota(jnp.int32, (N_EXPERTS, t), 0)
    hit1 = iota_e == e1
    hit2 = iota_e == e2
    hits = jnp.where(hit1 | hit2, 1.0, 0.0)
    before = (lax.broadcasted_iota(jnp.int32, (t, t), 0) < lax.broadcasted_iota(jnp.int32, (t, t), 1))
    prefix = jnp.dot(hits.astype(BF16), jnp.where(before, 1.0, 0.0).astype(BF16),
                     preferred_element_type=F32) + carry_sc[...]
    r1 = jnp.sum(jnp.where(hit1, prefix, 0.0), axis=0, keepdims=True)
    r2 = jnp.sum(jnp.where(hit2, prefix, 0.0), axis=0, keepdims=True)
    carry_sc[...] = carry_sc[...] + jnp.sum(hits, axis=1, keepdims=True)

    ri_ref[...] = jnp.zeros(ri_ref.shape, jnp.int32)
    ri_ref[0:1, :] = e1
    ri_ref[1:2, :] = e2
    ri_ref[2:3, :] = r1.astype(jnp.int32)
    ri_ref[3:4, :] = r2.astype(jnp.int32)
    rw_ref[...] = jnp.zeros(rw_ref.shape, F32)
    rw_ref[0:1, :] = w1
    rw_ref[1:2, :] = w2
    cnt_ref[...] = jnp.broadcast_to(carry_sc[...], cnt_ref.shape)


def _dest_kernel(cnt_ref, ri_ref, dest_ref, blk_ref, nused_ref, *, n_blocks_padded):
    cnt = cnt_ref[...]
    pad = jnp.floor((cnt + (MOE_ROWS - 1)) * (1.0 / MOE_ROWS)) * MOE_ROWS
    lower = jnp.where(lax.broadcasted_iota(jnp.int32, (N_EXPERTS, N_EXPERTS), 1)
                      <= lax.broadcasted_iota(jnp.int32, (N_EXPERTS, N_EXPERTS), 0), 1.0, 0.0)
    pad_end = jnp.dot(lower, pad, precision=lax.Precision.HIGHEST, preferred_element_type=F32)
    pad_start = (pad_end - pad)[:, 0:1]
    t = ri_ref.shape[1]
    iota_e = lax.broadcasted_iota(jnp.int32, (N_EXPERTS, t), 0)
    dest_ref[...] = jnp.zeros(dest_ref.shape, jnp.int32)
    for kk in range(2):
        start = jnp.sum(jnp.where(iota_e == ri_ref[kk:kk + 1, :], pad_start, 0.0), axis=0, keepdims=True)
        dest_ref[kk:kk + 1, :] = (start.astype(jnp.int32) + ri_ref[2 + kk:3 + kk, :]) * ROW_SUBLANES
    blk_start = (lax.broadcasted_iota(jnp.int32, (N_EXPERTS, n_blocks_padded), 1) * MOE_ROWS).astype(F32)
    owner = jnp.sum(jnp.where(pad_end[:, 0:1] <= blk_start, 1.0, 0.0), axis=0, keepdims=True)
    blk_ref[...] = jnp.broadcast_to(jnp.minimum(owner, N_EXPERTS - 1.0).astype(jnp.int32), blk_ref.shape)
    nused_ref[...] = jnp.broadcast_to((pad_end[N_EXPERTS - 1:N_EXPERTS, :] * (1.0 / MOE_ROWS)).astype(jnp.int32),
                                      nused_ref.shape)


def _to_row_tiles(ref, x):
    n = x.shape[0]
    for c in range(ROW_SUBLANES):
        ref[pl.ds(c, n, stride=ROW_SUBLANES), :] = x[:, c * 128:(c + 1) * 128]


def _from_row_tiles(ref, n):
    return jnp.concatenate([ref[pl.ds(c, n, stride=ROW_SUBLANES), :] for c in range(ROW_SUBLANES)], axis=1)


def _tile_copy(src_ref, src_row8, dst_ref, dst_row8, sem):
    return pltpu.make_async_copy(src_ref.at[pl.ds(pl.multiple_of(src_row8, ROW_SUBLANES), ROW_SUBLANES)],
                                 dst_ref.at[pl.ds(pl.multiple_of(dst_row8, ROW_SUBLANES), ROW_SUBLANES)], sem)


def _dispatch_kernel(dest_ref, x_ref, g_ref, xs_in_ref, xs_ref, h_sc, sem, *, tile):
    del xs_in_ref
    _to_row_tiles(h_sc, _rms_rows(x_ref[...], g_ref[...]))

    def issue(t, carry):
        _tile_copy(h_sc, t * ROW_SUBLANES, xs_ref, dest_ref[0, 0, t], sem).start()
        _tile_copy(h_sc, t * ROW_SUBLANES, xs_ref, dest_ref[0, 0, tile + t], sem).start()
        return carry

    def drain(t, carry):
        _tile_copy(h_sc, 0, xs_ref, 0, sem).wait()
        _tile_copy(h_sc, 0, xs_ref, 0, sem).wait()
        return carry

    lax.fori_loop(0, tile, issue, 0, unroll=DMA_UNROLL)
    lax.fori_loop(0, tile, drain, 0, unroll=DMA_UNROLL)


def _expert_kernel(blk_ref, nused_ref, xs_ref, wgu_ref, wd_ref, ys_ref):
    b = pl.program_id(0)

    @pl.when(b < nused_ref[0])
    def _():
        x = _from_row_tiles(xs_ref, MOE_ROWS).astype(BF16)
        gu = jnp.dot(x, wgu_ref[0], preferred_element_type=F32)
        a, u = gu[:, :D_EXPERT], gu[:, D_EXPERT:]
        hidden = (a * jax.nn.sigmoid(a) * u).astype(BF16)
        _to_row_tiles(ys_ref, jnp.dot(hidden, wd_ref[0], preferred_element_type=F32))

    @pl.when(b >= nused_ref[0])
    def _():
        ys_ref[...] = jnp.zeros(ys_ref.shape, F32)


def _combine_kernel(dest_ref, dest_next_ref, ys_ref, x_ref, w_ref, g_ref, y_ref, buf, sem, *, tile, final_norm):
    step = pl.program_id(0)
    slot = lax.rem(step, 2)

    def issue(d_ref, slot_):
        def body(t, carry):
            _tile_copy(ys_ref, d_ref[0, 0, t], buf.at[slot_, 0], t * ROW_SUBLANES, sem.at[slot_]).start()
            _tile_copy(ys_ref, d_ref[0, 0, tile + t], buf.at[slot_, 1], t * ROW_SUBLANES, sem.at[slot_]).start()
            return carry
        lax.fori_loop(0, tile, body, 0, unroll=DMA_UNROLL)

    @pl.when(step == 0)
    def _():
        issue(dest_ref, 0)

    @pl.when(step + 1 < pl.num_programs(0))
    def _():
        issue(dest_next_ref, 1 - slot)

    def drain(t, carry):
        _tile_copy(ys_ref, 0, buf.at[slot, 0], 0, sem.at[slot]).wait()
        _tile_copy(ys_ref, 0, buf.at[slot, 1], 0, sem.at[slot]).wait()
        return carry

    lax.fori_loop(0, tile, drain, 0, unroll=DMA_UNROLL)
    w = w_ref[...]
    y = x_ref[...] + (w[:, 0:1] * _from_row_tiles(buf.at[slot, 0], tile)
                      + w[:, 1:2] * _from_row_tiles(buf.at[slot, 1], tile))
    if final_norm:
        y = _rms_rows(y, g_ref[...])
    y_ref[...] = y


def _moe(x, g_ffn, wrT, br, wgu, wd, g_final, final_norm):
    n_tok = x.shape[0]
    tile = min(TOK_TILE, n_tok)
    n_tiles = n_tok // tile
    n_blocks = (2 * n_tok + MOE_ROWS - 1) // MOE_ROWS + N_EXPERTS
    n_blocks_padded = (n_blocks + 127) // 128 * 128
    n_rows = n_blocks * MOE_ROWS

    ri, rw, cnt = pl.pallas_call(
        _router_kernel,
        grid=(n_tiles,),
        in_specs=[
            pl.BlockSpec((tile, D_MODEL), lambda i: (i, 0)),
            _const_spec((1, D_MODEL)),
            _const_spec((ROUTER_LANES, D_MODEL)),
            _const_spec((ROUTER_LANES, 1)),
        ],
        out_specs=[
            pl.BlockSpec((8, tile), lambda i: (0, i)),
            pl.BlockSpec((8, tile), lambda i: (0, i)),
            _const_spec((N_EXPERTS, 128)),
        ],
        out_shape=[
            jax.ShapeDtypeStruct((8, n_tok), jnp.int32),
            jax.ShapeDtypeStruct((8, n_tok), F32),
            jax.ShapeDtypeStruct((N_EXPERTS, 128), F32),
        ],
        scratch_shapes=[pltpu.VMEM((N_EXPERTS, 1), F32)],
        compiler_params=_cparams(1),
        name="moe_router",
    )(x, g_ffn, wrT, br)

    dest, blk, nused = pl.pallas_call(
        functools.partial(_dest_kernel, n_blocks_padded=n_blocks_padded),
        grid=(n_tiles,),
        in_specs=[_const_spec((N_EXPERTS, 128)), pl.BlockSpec((8, tile), lambda i: (0, i))],
        out_specs=[
            pl.BlockSpec((8, tile), lambda i: (0, i)),
            _const_spec((8, n_blocks_padded)),
            _const_spec((8, 128)),
        ],
        out_shape=[
            jax.ShapeDtypeStruct((8, n_tok), jnp.int32),
            jax.ShapeDtypeStruct((8, n_blocks_padded), jnp.int32),
            jax.ShapeDtypeStruct((8, 128), jnp.int32),
        ],
        compiler_params=_cparams(1),
        name="moe_dest",
    )(cnt, ri)

    dest_tiles = dest[0:2].reshape(2, n_tiles, tile).transpose(1, 0, 2).reshape(n_tiles, 1, 2 * tile)
    dest_spec = pl.BlockSpec((1, 1, 2 * tile), lambda i: (i, 0, 0), memory_space=pltpu.SMEM)
    any_spec = pl.BlockSpec(memory_space=pl.ANY)

    xs = pl.pallas_call(
        functools.partial(_dispatch_kernel, tile=tile),
        grid=(n_tiles,),
        in_specs=[dest_spec, pl.BlockSpec((tile, D_MODEL), lambda i: (i, 0)), _const_spec((1, D_MODEL)), any_spec],
        out_specs=any_spec,
        out_shape=jax.ShapeDtypeStruct((n_rows * ROW_SUBLANES, 128), F32),
        scratch_shapes=[pltpu.VMEM((tile * ROW_SUBLANES, 128), F32), pltpu.SemaphoreType.DMA(())],
        input_output_aliases={3: 0},
        compiler_params=_cparams(1),
        name="moe_dispatch",
    )(dest_tiles, x, g_ffn, jnp.zeros((n_rows * ROW_SUBLANES, 128), F32))

    blk_rows = MOE_ROWS * ROW_SUBLANES
    ys = pl.pallas_call(
        _expert_kernel,
        grid_spec=pltpu.PrefetchScalarGridSpec(
            num_scalar_prefetch=2,
            grid=(n_blocks,),
            in_specs=[
                pl.BlockSpec((blk_rows, 128), lambda b, blk, nu: (b, 0)),
                pl.BlockSpec((1, D_MODEL, 2 * D_EXPERT), lambda b, blk, nu: (blk[b], 0, 0)),
                pl.BlockSpec((1, D_EXPERT, D_MODEL), lambda b, blk, nu: (blk[b], 0, 0)),
            ],
            out_specs=pl.BlockSpec((blk_rows, 128), lambda b, blk, nu: (b, 0)),
        ),
        out_shape=jax.ShapeDtypeStruct((n_rows * ROW_SUBLANES, 128), F32),
        compiler_params=_cparams(1),
        name="moe_experts",
    )(blk[0, :n_blocks], nused[0, :1], xs, wgu, wd)

    ctile = min(256, n_tok)
    c_tiles = n_tok // ctile
    dest_c = dest[0:2].reshape(2, c_tiles, ctile).transpose(1, 0, 2).reshape(c_tiles, 1, 2 * ctile)
    return pl.pallas_call(
        functools.partial(_combine_kernel, tile=ctile, final_norm=final_norm),
        grid=(c_tiles,),
        in_specs=[
            pl.BlockSpec((1, 1, 2 * ctile), lambda i: (i, 0, 0), memory_space=pltpu.SMEM),
            pl.BlockSpec((1, 1, 2 * ctile), lambda i: (jnp.minimum(i + 1, c_tiles - 1), 0, 0),
                         memory_space=pltpu.SMEM),
            any_spec,
            pl.BlockSpec((ctile, D_MODEL), lambda i: (i, 0)),
            pl.BlockSpec((ctile, 2), lambda i: (i, 0)),
            _const_spec((1, D_MODEL)),
        ],
        out_specs=pl.BlockSpec((ctile, D_MODEL), lambda i: (i, 0)),
        out_shape=jax.ShapeDtypeStruct((n_tok, D_MODEL), F32),
        scratch_shapes=[pltpu.VMEM((2, 2, ctile * ROW_SUBLANES, 128), F32), pltpu.SemaphoreType.DMA((2,))],
        compiler_params=_cparams(1),
        name="moe_combine",
    )(dest_c, dest_c, ys, x, rw[0:2].T, g_final)


def _rope_tables(n_tokens):
    rows = n_tokens // GRID_W
    row_idx = jnp.repeat(jnp.arange(rows, dtype=F32), GRID_W)
    col_idx = jnp.tile(jnp.arange(GRID_W, dtype=F32), rows)

    def angles(rot_dim):
        n_freq = rot_dim // 4
        inv_freq = ROPE_THETA ** (-jnp.arange(n_freq, dtype=F32) / n_freq)
        return jnp.concatenate([row_idx[:, None] * inv_freq, col_idx[:, None] * inv_freq], axis=-1)

    ag, am = angles(GQA_HD), angles(MLA_ROPE)
    cg, sg, cm, sm = jnp.cos(ag), jnp.sin(ag), jnp.cos(am), jnp.sin(am)
    hr = MLA_ROPE // 2
    zl = jnp.zeros((n_tokens, MLA_NOPE), F32)
    zr = jnp.zeros((n_tokens, HEAD_PAD - MLA_NOPE - MLA_ROPE), F32)
    del hr
    return dict(
        cT_g=cg.T, sT_g=sg.T,
        cF_g=jnp.concatenate([cg, cg], axis=1), sF_g=jnp.concatenate([-sg, sg], axis=1),
        cT_m=cm.T, sT_m=sm.T,
        cP_m=jnp.concatenate([zl, cm, cm, zr], axis=1), sP_m=jnp.concatenate([zl, -sm, sm, zr], axis=1),
    )


def _layer_params(l, norm_mix, norm_mem, w_in, gqa_q_norm, gqa_k_norm, mla_q_a_norm, mla_kv_a_norm, mla_w_qb,
                  mla_w_kvb, mem_w_kv, w_out, norm_ffn, w_group, b_group, w_expert, b_expert, w_gate_up, w_down):
    splits = []
    off = 0
    for width in (GQA_HEADS * GQA_HD, GQA_KV_HEADS * GQA_HD, GQA_KV_HEADS * GQA_HD, MLA_Q_LORA, MLA_KV_LORA,
                  MLA_ROPE, MEM_HEADS * MEM_HD, N_BRANCH * D_MODEL):
        splits.append(w_in[l][:, off:off + width])
        off += width
    wq_g, wk_g, wv_g, wq_a, wkv_a, wk_r, wq_c, w_gate = splits
    hr = MLA_ROPE // 2
    pad_l = jnp.zeros((D_MODEL, MLA_NOPE), F32)
    pad_r = jnp.zeros((D_MODEL, HEAD_PAD - MLA_NOPE - MLA_ROPE), F32)
    wk_r_swapped = jnp.concatenate([wk_r[:, hr:], wk_r[:, :hr]], axis=1)
    wkr2 = jnp.concatenate([pad_l, wk_r, pad_r, pad_l, wk_r_swapped, pad_r], axis=1)
    qk = MLA_NOPE + MLA_ROPE
    wqb = jnp.pad(mla_w_qb[l].reshape(MLA_Q_LORA, MLA_HEADS, qk), ((0, 0), (0, 0), (0, HEAD_PAD - qk)))
    wkvb = mla_w_kvb[l].reshape(MLA_KV_LORA, MLA_HEADS, MLA_NOPE + MLA_V)
    wkn = jnp.pad(wkvb[:, :, :MLA_NOPE], ((0, 0), (0, 0), (0, HEAD_PAD - MLA_NOPE)))
    wv_m = wkvb[:, :, MLA_NOPE:]
    n_router = N_GROUPS + N_EXPERTS
    return dict(
        g_mix=norm_mix[l][None, :], g_mem=norm_mem[l][None, :], g_ffn=norm_ffn[l][None, :],
        wqT_g=wq_g.T.astype(BF16), wk_g=wk_g.astype(BF16), wvT_g=wv_g.T.astype(BF16),
        gq=gqa_q_norm[l][:, None], gk=gqa_k_norm[l][None, :],
        wqaT=wq_a.T.astype(BF16), gqa=mla_q_a_norm[l][:, None],
        wqbT=wqb.reshape(MLA_Q_LORA, MLA_HEADS * HEAD_PAD).T.astype(BF16),
        wkva=wkv_a.astype(BF16), gkva=mla_kv_a_norm[l][None, :],
        wkn=wkn.reshape(MLA_KV_LORA, MLA_HEADS * HEAD_PAD).astype(BF16), wkr2=wkr2.astype(BF16),
        wvT_m=wv_m.reshape(MLA_KV_LORA, MLA_HEADS * MLA_V).T.astype(BF16),
        wmiscT=jnp.concatenate([wq_c, w_gate], axis=1).T.astype(BF16),
        wmk=mem_w_kv[l][:, :MEM_HEADS * MEM_HD].astype(BF16),
        wmvT=mem_w_kv[l][:, MEM_HEADS * MEM_HD:].T.astype(BF16),
        w_out=w_out[l].astype(BF16),
        wrT=jnp.concatenate([w_group[l].T, w_expert[l].T, jnp.zeros((ROUTER_LANES - n_router, D_MODEL), F32)], axis=0),
        br=jnp.concatenate([b_group[l], b_expert[l], jnp.zeros((ROUTER_LANES - n_router,), F32)])[:, None],
        wgu=w_gate_up[l].astype(BF16), wd=w_down[l].astype(BF16),
    )


def _mixer(x, mem, p, tabs):
    b, s, _ = x.shape
    tile = min(TOK_TILE, s)
    n_t = s // tile
    grid = (b, n_t)
    x_spec = pl.BlockSpec((1, tile, D_MODEL), lambda bi, i: (bi, i, 0))
    colT = lambda rows: pl.BlockSpec((rows, tile), lambda bi, i: (0, i))
    rowF = pl.BlockSpec((tile, HEAD_PAD), lambda bi, i: (i, 0))
    headT = lambda heads: pl.BlockSpec((1, heads, 1, HEAD_PAD, tile), lambda bi, i: (bi, 0, i, 0, 0))
    keyF = lambda heads: pl.BlockSpec((1, heads, tile, HEAD_PAD), lambda bi, i: (bi, 0, i, 0))
    valT = lambda heads: pl.BlockSpec((1, heads, 1, V_ROWS, tile), lambda bi, i: (bi, 0, i, 0, 0))

    qg, kg, vg = pl.pallas_call(
        _proj_gqa_kernel,
        grid=grid,
        in_specs=[x_spec, _const_spec((1, D_MODEL)), _const_spec(p["wqT_g"].shape), _const_spec(p["wk_g"].shape),
                  _const_spec(p["wvT_g"].shape), _const_spec((GQA_HD, 1)), _const_spec((1, GQA_HD)),
                  colT(GQA_HD // 2), colT(GQA_HD // 2), rowF, rowF],
        out_specs=[headT(GQA_HEADS), keyF(GQA_KV_HEADS), valT(GQA_KV_HEADS)],
        out_shape=[jax.ShapeDtypeStruct((b, GQA_HEADS, n_t, HEAD_PAD, tile), BF16),
                   jax.ShapeDtypeStruct((b, GQA_KV_HEADS, s, HEAD_PAD), BF16),
                   jax.ShapeDtypeStruct((b, GQA_KV_HEADS, n_t, V_ROWS, tile), BF16)],
        compiler_params=_cparams(2),
        name="proj_gqa",
    )(x, p["g_mix"], p["wqT_g"], p["wk_g"], p["wvT_g"], p["gq"], p["gk"],
      tabs["cT_g"], tabs["sT_g"], tabs["cF_g"], tabs["sF_g"])

    qm, km, vm = pl.pallas_call(
        _proj_mla_kernel,
        grid=grid,
        in_specs=[x_spec, _const_spec((1, D_MODEL)), _const_spec(p["wqaT"].shape), _const_spec((MLA_Q_LORA, 1)),
                  _const_spec(p["wqbT"].shape), _const_spec(p["wkva"].shape), _const_spec((1, MLA_KV_LORA)),
                  _const_spec(p["wkn"].shape), _const_spec(p["wkr2"].shape), _const_spec(p["wvT_m"].shape),
                  colT(MLA_ROPE // 2), colT(MLA_ROPE // 2), rowF, rowF],
        out_specs=[headT(MLA_HEADS), keyF(MLA_HEADS), valT(MLA_HEADS)],
        out_shape=[jax.ShapeDtypeStruct((b, MLA_HEADS, n_t, HEAD_PAD, tile), BF16),
                   jax.ShapeDtypeStruct((b, MLA_HEADS, s, HEAD_PAD), BF16),
                   jax.ShapeDtypeStruct((b, MLA_HEADS, n_t, V_ROWS, tile), BF16)],
        compiler_params=_cparams(2),
        name="proj_mla",
    )(x, p["g_mix"], p["wqaT"], p["gqa"], p["wqbT"], p["wkva"], p["gkva"], p["wkn"], p["wkr2"], p["wvT_m"],
      tabs["cT_m"], tabs["sT_m"], tabs["cP_m"], tabs["sP_m"])

    featT = lambda rows: pl.BlockSpec((1, rows, tile), lambda bi, i: (bi, 0, i))
    qc, gates = pl.pallas_call(
        _proj_misc_kernel,
        grid=grid,
        in_specs=[x_spec, _const_spec((1, D_MODEL)), _const_spec(p["wmiscT"].shape)],
        out_specs=[featT(D_MODEL), featT(N_BRANCH * D_MODEL)],
        out_shape=[jax.ShapeDtypeStruct((b, D_MODEL, s), BF16),
                   jax.ShapeDtypeStruct((b, N_BRANCH * D_MODEL, s), BF16)],
        compiler_params=_cparams(2),
        name="proj_misc",
    )(x, p["g_mix"], p["wmiscT"])

    n_mem = mem.shape[1]
    mk, mvT = pl.pallas_call(
        _mem_kv_kernel,
        grid=(b,),
        in_specs=[pl.BlockSpec((1, n_mem, D_MODEL), lambda bi: (bi, 0, 0)), _const_spec((1, D_MODEL)),
                  _const_spec(p["wmk"].shape), _const_spec(p["wmvT"].shape)],
        out_specs=[pl.BlockSpec((1, n_mem, D_MODEL), lambda bi: (bi, 0, 0)),
                   pl.BlockSpec((1, D_MODEL, n_mem), lambda bi: (bi, 0, 0))],
        out_shape=[jax.ShapeDtypeStruct((b, n_mem, D_MODEL), BF16), jax.ShapeDtypeStruct((b, D_MODEL, n_mem), BF16)],
        compiler_params=_cparams(1),
        name="mem_kv",
    )(mem, p["g_mem"], p["wmk"], p["wmvT"])

    oa = _attention(qg, kg, vg, GQA_HEADS // GQA_KV_HEADS)
    ob = _attention(qm, km, vm, 1)
    oc = pl.pallas_call(
        _mem_attn_kernel,
        grid=grid,
        in_specs=[featT(D_MODEL), pl.BlockSpec((1, n_mem, D_MODEL), lambda bi, i: (bi, 0, 0)),
                  pl.BlockSpec((1, D_MODEL, n_mem), lambda bi, i: (bi, 0, 0))],
        out_specs=featT(D_MODEL),
        out_shape=jax.ShapeDtypeStruct((b, D_MODEL, s), BF16),
        compiler_params=_cparams(2),
        name="mem_attention",
    )(qc, mk, mvT)

    return pl.pallas_call(
        _merge_kernel,
        grid=grid,
        in_specs=[headT(GQA_HEADS), headT(MLA_HEADS), featT(D_MODEL), featT(N_BRANCH * D_MODEL), x_spec,
                  _const_spec((D_MODEL, D_MODEL))],
        out_specs=x_spec,
        out_shape=jax.ShapeDtypeStruct(x.shape, F32),
        compiler_params=_cparams(2),
        name="merge_out",
    )(oa, ob, oc, gates, x, p["w_out"])


def _trunk(x, mem, layers, g_final):
    b, s, d = x.shape
    tabs = _rope_tables(s)
    for l, p in enumerate(layers):
        x = _mixer(x, mem, p, tabs)
        x = _moe(x.reshape(b * s, d), p["g_ffn"], p["wrT"], p["br"], p["wgu"], p["wd"], g_final,
                 final_norm=(l == len(layers) - 1)).reshape(b, s, d)
    return x


def kernel(x_prompt, x_sample, mem_prompt, mem_sample, norm_mix, norm_mem, w_in, gqa_q_norm, gqa_k_norm, mla_q_a_norm, mla_kv_a_norm, mla_w_qb, mla_w_kvb, mem_w_kv, w_out, norm_ffn, w_group, b_group, w_expert, b_expert, w_gate_up, w_down, norm_final):
    depth = w_in.shape[0]
    layers = [_layer_params(l, norm_mix, norm_mem, w_in, gqa_q_norm, gqa_k_norm, mla_q_a_norm, mla_kv_a_norm,
                            mla_w_qb, mla_w_kvb, mem_w_kv, w_out, norm_ffn, w_group, b_group, w_expert, b_expert,
                            w_gate_up, w_down) for l in range(depth)]
    g_final = norm_final[None, :]
    return (_trunk(x_prompt, mem_prompt, layers, g_final), _trunk(x_sample, mem_sample, layers, g_final))
```

```python
import functools
import math

import jax
import jax.numpy as jnp
from jax import lax
from jax.experimental import pallas as pl
from jax.experimental.pallas import tpu as pltpu

F32 = jnp.float32
BF16 = jnp.bfloat16

D_MODEL = 1024
EPS = 1e-6
GRID_W = 64
ROPE_THETA = 10000.0
GQA_HEADS, GQA_KV_HEADS, GQA_HD = 8, 2, 128
MLA_HEADS, MLA_Q_LORA, MLA_KV_LORA = 8, 384, 256
MLA_NOPE, MLA_ROPE, MLA_V = 64, 32, 128
MEM_HEADS, MEM_HD = 4, 256
N_BRANCH = 3
N_GROUPS, EXPERTS_PER_GROUP, D_EXPERT = 8, 8, 256
N_EXPERTS = N_GROUPS * EXPERTS_PER_GROUP
HEAD_PAD = 128
BF16_SUBLANES = 16
V_ROWS = HEAD_PAD + BF16_SUBLANES
LOG2E = 1.4426950408889634

TOK_TILE = 512
ATT_Q_COLS = 2048
ATT_CHUNKS_PER_BODY = 4
MOE_ROWS = 256
ROUTER_LANES = 128
ROW_SUBLANES = D_MODEL // 128
DMA_UNROLL = 8
VMEM_LIMIT = 56 * 1024 * 1024


def _cparams(n_axes, flags=None):
    return pltpu.CompilerParams(dimension_semantics=("arbitrary",) * n_axes, vmem_limit_bytes=VMEM_LIMIT,
                                flags=flags)


def _nt(a, b):
    return lax.dot_general(a, b, (((1,), (1,)), ((), ())), preferred_element_type=F32)


def _rms_rows(x, g):
    return x * lax.rsqrt(jnp.mean(x * x, axis=-1, keepdims=True) + EPS) * g


def _rms_cols(x, g):
    return x * lax.rsqrt(jnp.mean(x * x, axis=0, keepdims=True) + EPS) * g


def _const_spec(shape):
    return pl.BlockSpec(shape, lambda *_: (0,) * len(shape))


def _store_values(v_ref, vT):
    t = vT.shape[1]
    v_ref[0:HEAD_PAD, :] = vT.astype(BF16)
    ones_row = lax.broadcasted_iota(jnp.int32, (BF16_SUBLANES, t), 0) == 0
    v_ref[HEAD_PAD:V_ROWS, :] = jnp.where(ones_row, 1.0, 0.0).astype(BF16)


def _proj_gqa_kernel(x_ref, g_ref, wqT_ref, wk_ref, wvT_ref, gq_ref, gk_ref, cT_ref, sT_ref, cF_ref, sF_ref,
                     q_ref, k_ref, v_ref):
    h = _rms_rows(x_ref[0], g_ref[...]).astype(BF16)
    qT = _nt(wqT_ref[...], h)
    c, s, gq = cT_ref[...], sT_ref[...], gq_ref[...]
    half = GQA_HD // 2
    scale = GQA_HD ** -0.5 * LOG2E
    for hd in range(GQA_HEADS):
        qn = _rms_cols(qT[hd * GQA_HD:(hd + 1) * GQA_HD], gq)
        x1, x2 = qn[:half], qn[half:]
        o = jnp.concatenate([x1 * c - x2 * s, x2 * c + x1 * s], axis=0) * scale
        q_ref[0, hd, 0] = o.astype(BF16)
    k = jnp.dot(h, wk_ref[...], preferred_element_type=F32)
    for j in range(GQA_KV_HEADS):
        kn = _rms_rows(k[:, j * GQA_HD:(j + 1) * GQA_HD], gk_ref[...])
        ko = kn * cF_ref[...] + pltpu.roll(kn, half, 1) * sF_ref[...]
        k_ref[0, j] = ko.astype(BF16)
    vT = _nt(wvT_ref[...], h)
    for j in range(GQA_KV_HEADS):
        _store_values(v_ref.at[0, j, 0], vT[j * GQA_HD:(j + 1) * GQA_HD])


def _proj_mla_kernel(x_ref, g_ref, wqaT_ref, gqa_ref, wqbT_ref, wkva_ref, gkva_ref, wkn_ref, wkr_ref, wvT_ref,
                     cT_ref, sT_ref, cP_ref, sP_ref, q_ref, k_ref, v_ref):
    h = _rms_rows(x_ref[0], g_ref[...]).astype(BF16)
    qa = _rms_cols(_nt(wqaT_ref[...], h), gqa_ref[...]).astype(BF16)
    qT = jnp.dot(wqbT_ref[...], qa, preferred_element_type=F32)
    c, s = cT_ref[...], sT_ref[...]
    hr = MLA_ROPE // 2
    scale = (MLA_NOPE + MLA_ROPE) ** -0.5 * LOG2E
    for hd in range(MLA_HEADS):
        q = qT[hd * HEAD_PAD:(hd + 1) * HEAD_PAD]
        x1, x2 = q[MLA_NOPE:MLA_NOPE + hr], q[MLA_NOPE + hr:MLA_NOPE + 2 * hr]
        o = jnp.concatenate([q[:MLA_NOPE], x1 * c - x2 * s, x2 * c + x1 * s, q[MLA_NOPE + 2 * hr:]], axis=0) * scale
        q_ref[0, hd, 0] = o.astype(BF16)
    kva = jnp.dot(h, wkva_ref[...], preferred_element_type=F32)
    kvn = _rms_rows(kva, gkva_ref[...]).astype(BF16)
    kr2 = jnp.dot(h, wkr_ref[...], preferred_element_type=F32)
    kr = kr2[:, :HEAD_PAD] * cP_ref[...] + kr2[:, HEAD_PAD:] * sP_ref[...]
    kn = jnp.dot(kvn, wkn_ref[...], preferred_element_type=F32)
    for hd in range(MLA_HEADS):
        k_ref[0, hd] = (kn[:, hd * HEAD_PAD:(hd + 1) * HEAD_PAD] + kr).astype(BF16)
    vT = _nt(wvT_ref[...], kvn)
    for hd in range(MLA_HEADS):
        _store_values(v_ref.at[0, hd, 0], vT[hd * MLA_V:(hd + 1) * MLA_V])


def _proj_misc_kernel(x_ref, g_ref, wT_ref, qc_ref, gate_ref):
    h = _rms_rows(x_ref[0], g_ref[...]).astype(BF16)
    scale = MEM_HD ** -0.5 * LOG2E
    rows = 512
    n_q = D_MODEL // rows
    for ci in range((D_MODEL + N_BRANCH * D_MODEL) // rows):
        y = _nt(wT_ref[ci * rows:(ci + 1) * rows, :], h)
        if ci < n_q:
            qc_ref[0, ci * rows:(ci + 1) * rows, :] = (y * scale).astype(BF16)
        else:
            r0 = (ci - n_q) * rows
            gate_ref[0, r0:r0 + rows, :] = jax.nn.sigmoid(y).astype(BF16)


def _mem_kv_kernel(mem_ref, g_ref, wk_ref, wvT_ref, mk_ref, mvT_ref):
    h = _rms_rows(mem_ref[0], g_ref[...]).astype(BF16)
    mk_ref[0] = jnp.dot(h, wk_ref[...], preferred_element_type=F32).astype(BF16)
    mvT_ref[0] = _nt(wvT_ref[...], h).astype(BF16)


def _attn_kernel(q_ref, k_ref, v_ref, o_ref, m_sc, acc_sc, s_sc, mc_sc, *, chunks_per_head, n_kv):
    chunk = TOK_TILE
    n_sub = m_sc.shape[0]

    m_sc[...] = jnp.full(m_sc.shape, -jnp.inf, F32)
    acc_sc[...] = jnp.zeros(acc_sc.shape, F32)

    def scores(c, u, slot):
        k = k_ref[0, 0, pl.ds(pl.multiple_of(c * chunk, chunk), chunk), :]
        s = jnp.dot(k, q_ref[0, u // chunks_per_head, u % chunks_per_head],
                    preferred_element_type=F32)
        s_sc[slot, u] = s
        mc_sc[slot, u] = jnp.max(s, axis=0, keepdims=True)

    def softmax_values(c, u, slot):
        m_prev = m_sc[u]
        m_cur = jnp.maximum(m_prev, mc_sc[slot, u])
        alpha = jnp.exp2(m_prev - m_cur)
        p = jnp.exp2(s_sc[slot, u] - m_cur)
        acc_sc[u] = alpha * acc_sc[u] + jnp.dot(v_ref[0, 0, c], p.astype(BF16), preferred_element_type=F32)
        m_sc[u] = m_cur

    def stage(c_scores, c_values):
        for u in range(n_sub):
            if c_scores is not None:
                scores(c_scores, u, c_scores % 2)
            if c_values is not None:
                softmax_values(c_values, u, c_values % 2)

    per_body = ATT_CHUNKS_PER_BODY
    n_bodies = (n_kv - 1) // per_body
    stage(0, None)

    def body(i, carry):
        for t in range(per_body):
            for u in range(n_sub):
                scores(per_body * i + t + 1, u, (t + 1) % 2)
                softmax_values(per_body * i + t, u, t % 2)
        return carry

    lax.fori_loop(0, n_bodies, body, 0)
    for c in range(per_body * n_bodies, n_kv - 1):
        stage(c + 1, c)
    stage(None, n_kv - 1)

    for u in range(n_sub):
        o_ref[0, u // chunks_per_head, u % chunks_per_head] = (
            acc_sc[u, 0:HEAD_PAD] / acc_sc[u, HEAD_PAD:HEAD_PAD + 1]).astype(BF16)


def _attention(qT, k, vT, group):
    b, h, n_kv, hd, chunk = qT.shape
    hkv = k.shape[1]
    s = n_kv * chunk
    assert h == hkv * group and chunk == TOK_TILE
    chunks_per_head = min(ATT_Q_COLS // group, s) // chunk
    n_sub = group * chunks_per_head
    kern = functools.partial(_attn_kernel, chunks_per_head=chunks_per_head, n_kv=n_kv)
    q_spec = pl.BlockSpec((1, group, chunks_per_head, hd, chunk), lambda bi, j, qi: (bi, j, qi, 0, 0))
    return pl.pallas_call(
        kern,
        grid=(b, hkv, n_kv // chunks_per_head),
        in_specs=[
            q_spec,
            pl.BlockSpec((1, 1, s, hd), lambda bi, j, qi: (bi, j, 0, 0)),
            pl.BlockSpec((1, 1, n_kv, V_ROWS, chunk), lambda bi, j, qi: (bi, j, 0, 0, 0)),
        ],
        out_specs=q_spec,
        out_shape=jax.ShapeDtypeStruct(qT.shape, BF16),
        scratch_shapes=[
            pltpu.VMEM((n_sub, 1, chunk), F32),
            pltpu.VMEM((n_sub, V_ROWS, chunk), F32),
            pltpu.VMEM((2, n_sub, chunk, chunk), F32),
            pltpu.VMEM((2, n_sub, 1, chunk), F32),
        ],
        compiler_params=_cparams(3),
        name="attention",
    )(qT, k, vT)


def _mem_attn_kernel(qc_ref, mk_ref, mvT_ref, o_ref):
    for hd in range(MEM_HEADS):
        r = slice(hd * MEM_HD, (hd + 1) * MEM_HD)
        s = jnp.dot(mk_ref[0, :, r], qc_ref[0, r, :], preferred_element_type=F32)
        p = jnp.exp2(s - jnp.max(s, axis=0, keepdims=True))
        l = jnp.sum(p, axis=0, keepdims=True)
        o = jnp.dot(mvT_ref[0, r, :], p.astype(BF16), preferred_element_type=F32)
        o_ref[0, r, :] = (o / l).astype(BF16)


def _merge_kernel(oa_ref, ob_ref, oc_ref, gate_ref, x_ref, w_ref, y_ref):
    t = x_ref.shape[1]
    oa = oa_ref[0].reshape(D_MODEL, t)
    ob = ob_ref[0].reshape(D_MODEL, t)
    merged = (gate_ref[0, 0:D_MODEL, :].astype(F32) * oa.astype(F32)
              + gate_ref[0, D_MODEL:2 * D_MODEL, :].astype(F32) * ob.astype(F32)
              + gate_ref[0, 2 * D_MODEL:, :].astype(F32) * oc_ref[0].astype(F32)).astype(BF16)
    out = lax.dot_general(merged, w_ref[...], (((0,), (0,)), ((), ())), preferred_element_type=F32)
    y_ref[0] = x_ref[0] + out


def _first_index_of_max(v, vmax, iota, n):
    return jnp.min(jnp.where(v == vmax, iota, n), axis=0, keepdims=True)


def _router_kernel(x_ref, g_ref, wT_ref, b_ref, ri_ref, rw_ref, cnt_ref, carry_sc):
    step = pl.program_id(0)

    @pl.when(step == 0)
    def _():
        carry_sc[...] = jnp.zeros(carry_sc.shape, F32)

    h = _rms_rows(x_ref[...], g_ref[...])
    t = h.shape[0]
    logits = lax.dot_general(wT_ref[...], h, (((1,), (1,)), ((), ())), precision=lax.Precision.HIGHEST,
                             preferred_element_type=F32) + b_ref[...]
    iota_g = lax.broadcasted_iota(jnp.int32, (N_GROUPS, t), 0)
    gl = logits[0:N_GROUPS]
    gmax = jnp.max(gl, axis=0, keepdims=True)
    gidx = _first_index_of_max(gl, gmax, iota_g, N_GROUPS)
    gw = 1.0 / jnp.sum(jnp.exp(gl - gmax), axis=0, keepdims=True)
    sel = jnp.zeros((EXPERTS_PER_GROUP, t), F32)
    for g in range(N_GROUPS):
        r0 = N_GROUPS + g * EXPERTS_PER_GROUP
        sel = sel + jnp.where(gidx == g, logits[r0:r0 + EXPERTS_PER_GROUP], 0.0)
    v1 = jnp.max(sel, axis=0, keepdims=True)
    i1 = _first_index_of_max(sel, v1, iota_g, EXPERTS_PER_GROUP)
    sel2 = jnp.where(iota_g == i1, -jnp.inf, sel)
    v2 = jnp.max(sel2, axis=0, keepdims=True)
    i2 = _first_index_of_max(sel2, v2, iota_g, EXPERTS_PER_GROUP)
    e = jnp.exp(v2 - v1)
    w1 = gw / (1.0 + e)
    w2 = gw * e / (1.0 + e)
    e1 = gidx * EXPERTS_PER_GROUP + i1
    e2 = gidx * EXPERTS_PER_GROUP + i2

    iota_e = lax.broadcasted_iota(jnp.int32, (N_EXPERTS, t), 0)
    hit1 = iota_e == e1
    hit2 = iota_e == e2
    hits = jnp.where(hit1 | hit2, 1.0, 0.0)
    before = (lax.broadcasted_iota(jnp.int32, (t, t), 0) < lax.broadcasted_iota(jnp.int32, (t, t), 1))
    prefix = jnp.dot(hits.astype(BF16), jnp.where(before, 1.0, 0.0).astype(BF16),
                     preferred_element_type=F32) + carry_sc[...]
    r1 = jnp.sum(jnp.where(hit1, prefix, 0.0), axis=0, keepdims=True)
    r2 = jnp.sum(jnp.where(hit2, prefix, 0.0), axis=0, keepdims=True)
    carry_sc[...] = carry_sc[...] + jnp.sum(hits, axis=1, keepdims=True)

    ri_ref[...] = jnp.zeros(ri_ref.shape, jnp.int32)
    ri_ref[0:1, :] = e1
    ri_ref[1:2, :] = e2
    ri_ref[2:3, :] = r1.astype(jnp.int32)
    ri_ref[3:4, :] = r2.astype(jnp.int32)
    rw_ref[...] = jnp.zeros(rw_ref.shape, F32)
    rw_ref[0:1, :] = w1
    rw_ref[1:2, :] = w2
    cnt_ref[...] = jnp.broadcast_to(carry_sc[...], cnt_ref.shape)


def _dest_kernel(cnt_ref, ri_ref, dest_ref, blk_ref, nused_ref, *, n_blocks_padded):
    cnt = cnt_ref[...]
    pad = jnp.floor((cnt + (MOE_ROWS - 1)) * (1.0 / MOE_ROWS)) * MOE_ROWS
    lower = jnp.where(lax.broadcasted_iota(jnp.int32, (N_EXPERTS, N_EXPERTS), 1)
                      <= lax.broadcasted_iota(jnp.int32, (N_EXPERTS, N_EXPERTS), 0), 1.0, 0.0)
    pad_end = jnp.dot(lower, pad, precision=lax.Precision.HIGHEST, preferred_element_type=F32)
    pad_start = (pad_end - pad)[:, 0:1]
    t = ri_ref.shape[1]
    iota_e = lax.broadcasted_iota(jnp.int32, (N_EXPERTS, t), 0)
    dest_ref[...] = jnp.zeros(dest_ref.shape, jnp.int32)
    for kk in range(2):
        start = jnp.sum(jnp.where(iota_e == ri_ref[kk:kk + 1, :], pad_start, 0.0), axis=0, keepdims=True)
        dest_ref[kk:kk + 1, :] = (start.astype(jnp.int32) + ri_ref[2 + kk:3 + kk, :]) * ROW_SUBLANES
    blk_start = (lax.broadcasted_iota(jnp.int32, (N_EXPERTS, n_blocks_padded), 1) * MOE_ROWS).astype(F32)
    owner = jnp.sum(jnp.where(pad_end[:, 0:1] <= blk_start, 1.0, 0.0), axis=0, keepdims=True)
    blk_ref[...] = jnp.broadcast_to(jnp.minimum(owner, N_EXPERTS - 1.0).astype(jnp.int32), blk_ref.shape)
    nused_ref[...] = jnp.broadcast_to((pad_end[N_EXPERTS - 1:N_EXPERTS, :] * (1.0 / MOE_ROWS)).astype(jnp.int32),
                                      nused_ref.shape)


def _to_row_tiles(ref, x):
    n = x.shape[0]
    for c in range(ROW_SUBLANES):
        ref[pl.ds(c, n, stride=ROW_SUBLANES), :] = x[:, c * 128:(c + 1) * 128]


def _from_row_tiles(ref, n):
    return jnp.concatenate([ref[pl.ds(c, n, stride=ROW_SUBLANES), :] for c in range(ROW_SUBLANES)], axis=1)


def _tile_copy(src_ref, src_row8, dst_ref, dst_row8, sem):
    return pltpu.make_async_copy(src_ref.at[pl.ds(pl.multiple_of(src_row8, ROW_SUBLANES), ROW_SUBLANES)],
                                 dst_ref.at[pl.ds(pl.multiple_of(dst_row8, ROW_SUBLANES), ROW_SUBLANES)], sem)


def _dispatch_kernel(dest_ref, x_ref, g_ref, xs_in_ref, xs_ref, h_sc, sem, *, tile):
    del xs_in_ref
    _to_row_tiles(h_sc, _rms_rows(x_ref[...], g_ref[...]))

    def issue(t, carry):
        _tile_copy(h_sc, t * ROW_SUBLANES, xs_ref, dest_ref[0, 0, t], sem).start()
        _tile_copy(h_sc, t * ROW_SUBLANES, xs_ref, dest_ref[0, 0, tile + t], sem).start()
        return carry

    def drain(t, carry):
        _tile_copy(h_sc, 0, xs_ref, 0, sem).wait()
        _tile_copy(h_sc, 0, xs_ref, 0, sem).wait()
        return carry

    lax.fori_loop(0, tile, issue, 0, unroll=DMA_UNROLL)
    lax.fori_loop(0, tile, drain, 0, unroll=DMA_UNROLL)


def _expert_kernel(blk_ref, nused_ref, xs_ref, wgu_ref, wd_ref, ys_ref):
    b = pl.program_id(0)

    @pl.when(b < nused_ref[0])
    def _():
        x = _from_row_tiles(xs_ref, MOE_ROWS).astype(BF16)
        gu = jnp.dot(x, wgu_ref[0], preferred_element_type=F32)
        a, u = gu[:, :D_EXPERT], gu[:, D_EXPERT:]
        hidden = (a * jax.nn.sigmoid(a) * u).astype(BF16)
        _to_row_tiles(ys_ref, jnp.dot(hidden, wd_ref[0], preferred_element_type=F32))

    @pl.when(b >= nused_ref[0])
    def _():
        ys_ref[...] = jnp.zeros(ys_ref.shape, F32)


def _combine_kernel(dest_ref, dest_next_ref, ys_ref, x_ref, w_ref, g_ref, y_ref, buf, sem, *, tile, final_norm):
    step = pl.program_id(0)
    slot = lax.rem(step, 2)

    def issue(d_ref, slot_):
        def body(t, carry):
            _tile_copy(ys_ref, d_ref[0, 0, t], buf.at[slot_, 0], t * ROW_SUBLANES, sem.at[slot_]).start()
            _tile_copy(ys_ref, d_ref[0, 0, tile + t], buf.at[slot_, 1], t * ROW_SUBLANES, sem.at[slot_]).start()
            return carry
        lax.fori_loop(0, tile, body, 0, unroll=DMA_UNROLL)

    @pl.when(step == 0)
    def _():
        issue(dest_ref, 0)

    @pl.when(step + 1 < pl.num_programs(0))
    def _():
        issue(dest_next_ref, 1 - slot)

    def drain(t, carry):
        _tile_copy(ys_ref, 0, buf.at[slot, 0], 0, sem.at[slot]).wait()
        _tile_copy(ys_ref, 0, buf.at[slot, 1], 0, sem.at[slot]).wait()
        return carry

    lax.fori_loop(0, tile, drain, 0, unroll=DMA_UNROLL)
    w = w_ref[...]
    y = x_ref[...] + (w[:, 0:1] * _from_row_tiles(buf.at[slot, 0], tile)
                      + w[:, 1:2] * _from_row_tiles(buf.at[slot, 1], tile))
    if final_norm:
        y = _rms_rows(y, g_ref[...])
    y_ref[...] = y


def _moe(x, g_ffn, wrT, br, wgu, wd, g_final, final_norm):
    n_tok = x.shape[0]
    tile = min(TOK_TILE, n_tok)
    n_tiles = n_tok // tile
    n_blocks = (2 * n_tok + MOE_ROWS - 1) // MOE_ROWS + N_EXPERTS
    n_blocks_padded = (n_blocks + 127) // 128 * 128
    n_rows = n_blocks * MOE_ROWS

    ri, rw, cnt = pl.pallas_call(
        _router_kernel,
        grid=(n_tiles,),
        in_specs=[
            pl.BlockSpec((tile, D_MODEL), lambda i: (i, 0)),
            _const_spec((1, D_MODEL)),
            _const_spec((ROUTER_LANES, D_MODEL)),
            _const_spec((ROUTER_LANES, 1)),
        ],
        out_specs=[
            pl.BlockSpec((8, tile), lambda i: (0, i)),
            pl.BlockSpec((8, tile), lambda i: (0, i)),
            _const_spec((N_EXPERTS, 128)),
        ],
        out_shape=[
            jax.ShapeDtypeStruct((8, n_tok), jnp.int32),
            jax.ShapeDtypeStruct((8, n_tok), F32),
            jax.ShapeDtypeStruct((N_EXPERTS, 128), F32),
        ],
        scratch_shapes=[pltpu.VMEM((N_EXPERTS, 1), F32)],
        compiler_params=_cparams(1),
        name="moe_router",
    )(x, g_ffn, wrT, br)

    dest, blk, nused = pl.pallas_call(
        functools.partial(_dest_kernel, n_blocks_padded=n_blocks_padded),
        grid=(n_tiles,),
        in_specs=[_const_spec((N_EXPERTS, 128)), pl.BlockSpec((8, tile), lambda i: (0, i))],
        out_specs=[
            pl.BlockSpec((8, tile), lambda i: (0, i)),
            _const_spec((8, n_blocks_padded)),
            _const_spec((8, 128)),
        ],
        out_shape=[
            jax.ShapeDtypeStruct((8, n_tok), jnp.int32),
            jax.ShapeDtypeStruct((8, n_blocks_padded), jnp.int32),
            jax.ShapeDtypeStruct((8, 128), jnp.int32),
        ],
        compiler_params=_cparams(1),
        name="moe_dest",
    )(cnt, ri)

    dest_tiles = dest[0:2].reshape(2, n_tiles, tile).transpose(1, 0, 2).reshape(n_tiles, 1, 2 * tile)
    dest_spec = pl.BlockSpec((1, 1, 2 * tile), lambda i: (i, 0, 0), memory_space=pltpu.SMEM)
    any_spec = pl.BlockSpec(memory_space=pl.ANY)

    xs = pl.pallas_call(
        functools.partial(_dispatch_kernel, tile=tile),
        grid=(n_tiles,),
        in_specs=[dest_spec, pl.BlockSpec((tile, D_MODEL), lambda i: (i, 0)), _const_spec((1, D_MODEL)), any_spec],
        out_specs=any_spec,
        out_shape=jax.ShapeDtypeStruct((n_rows * ROW_SUBLANES, 128), F32),
        scratch_shapes=[pltpu.VMEM((tile * ROW_SUBLANES, 128), F32), pltpu.SemaphoreType.DMA(())],
        input_output_aliases={3: 0},
        compiler_params=_cparams(1),
        name="moe_dispatch",
    )(dest_tiles, x, g_ffn, jnp.zeros((n_rows * ROW_SUBLANES, 128), F32))

    blk_rows = MOE_ROWS * ROW_SUBLANES
    ys = pl.pallas_call(
        _expert_kernel,
        grid_spec=pltpu.PrefetchScalarGridSpec(
            num_scalar_prefetch=2,
            grid=(n_blocks,),
            in_specs=[
                pl.BlockSpec((blk_rows, 128), lambda b, blk, nu: (b, 0)),
                pl.BlockSpec((1, D_MODEL, 2 * D_EXPERT), lambda b, blk, nu: (blk[b], 0, 0)),
                pl.BlockSpec((1, D_EXPERT, D_MODEL), lambda b, blk, nu: (blk[b], 0, 0)),
            ],
            out_specs=pl.BlockSpec((blk_rows, 128), lambda b, blk, nu: (b, 0)),
        ),
        out_shape=jax.ShapeDtypeStruct((n_rows * ROW_SUBLANES, 128), F32),
        compiler_params=_cparams(1),
        name="moe_experts",
    )(blk[0, :n_blocks], nused[0, :1], xs, wgu, wd)

    ctile = min(256, n_tok)
    c_tiles = n_tok // ctile
    dest_c = dest[0:2].reshape(2, c_tiles, ctile).transpose(1, 0, 2).reshape(c_tiles, 1, 2 * ctile)
    return pl.pallas_call(
        functools.partial(_combine_kernel, tile=ctile, final_norm=final_norm),
        grid=(c_tiles,),
        in_specs=[
            pl.BlockSpec((1, 1, 2 * ctile), lambda i: (i, 0, 0), memory_space=pltpu.SMEM),
            pl.BlockSpec((1, 1, 2 * ctile), lambda i: (jnp.minimum(i + 1, c_tiles - 1), 0, 0),
                         memory_space=pltpu.SMEM),
            any_spec,
            pl.BlockSpec((ctile, D_MODEL), lambda i: (i, 0)),
            pl.BlockSpec((ctile, 2), lambda i: (i, 0)),
            _const_spec((1, D_MODEL)),
        ],
        out_specs=pl.BlockSpec((ctile, D_MODEL), lambda i: (i, 0)),
        out_shape=jax.ShapeDtypeStruct((n_tok, D_MODEL), F32),
        scratch_shapes=[pltpu.VMEM((2, 2, ctile * ROW_SUBLANES, 128), F32), pltpu.SemaphoreType.DMA((2,))],
        compiler_params=_cparams(1),
        name="moe_combine",
    )(dest_c, dest_c, ys, x, rw[0:2].T, g_final)


def _rope_tables(n_tokens):
    rows = n_tokens // GRID_W
    row_idx = jnp.repeat(jnp.arange(rows, dtype=F32), GRID_W)
    col_idx = jnp.tile(jnp.arange(GRID_W, dtype=F32), rows)

    def angles(rot_dim):
        n_freq = rot_dim // 4
        inv_freq = ROPE_THETA ** (-jnp.arange(n_freq, dtype=F32) / n_freq)
        return jnp.concatenate([row_idx[:, None] * inv_freq, col_idx[:, None] * inv_freq], axis=-1)

    ag, am = angles(GQA_HD), angles(MLA_ROPE)
    cg, sg, cm, sm = jnp.cos(ag), jnp.sin(ag), jnp.cos(am), jnp.sin(am)
    hr = MLA_ROPE // 2
    zl = jnp.zeros((n_tokens, MLA_NOPE), F32)
    zr = jnp.zeros((n_tokens, HEAD_PAD - MLA_NOPE - MLA_ROPE), F32)
    del hr
    return dict(
        cT_g=cg.T, sT_g=sg.T,
        cF_g=jnp.concatenate([cg, cg], axis=1), sF_g=jnp.concatenate([-sg, sg], axis=1),
        cT_m=cm.T, sT_m=sm.T,
        cP_m=jnp.concatenate([zl, cm, cm, zr], axis=1), sP_m=jnp.concatenate([zl, -sm, sm, zr], axis=1),
    )


def _layer_params(l, norm_mix, norm_mem, w_in, gqa_q_norm, gqa_k_norm, mla_q_a_norm, mla_kv_a_norm, mla_w_qb,
                  mla_w_kvb, mem_w_kv, w_out, norm_ffn, w_group, b_group, w_expert, b_expert, w_gate_up, w_down):
    splits = []
    off = 0
    for width in (GQA_HEADS * GQA_HD, GQA_KV_HEADS * GQA_HD, GQA_KV_HEADS * GQA_HD, MLA_Q_LORA, MLA_KV_LORA,
                  MLA_ROPE, MEM_HEADS * MEM_HD, N_BRANCH * D_MODEL):
        splits.append(w_in[l][:, off:off + width])
        off += width
    wq_g, wk_g, wv_g, wq_a, wkv_a, wk_r, wq_c, w_gate = splits
    hr = MLA_ROPE // 2
    pad_l = jnp.zeros((D_MODEL, MLA_NOPE), F32)
    pad_r = jnp.zeros((D_MODEL, HEAD_PAD - MLA_NOPE - MLA_ROPE), F32)
    wk_r_swapped = jnp.concatenate([wk_r[:, hr:], wk_r[:, :hr]], axis=1)
    wkr2 = jnp.concatenate([pad_l, wk_r, pad_r, pad_l, wk_r_swapped, pad_r], axis=1)
    qk = MLA_NOPE + MLA_ROPE
    wqb = jnp.pad(mla_w_qb[l].reshape(MLA_Q_LORA, MLA_HEADS, qk), ((0, 0), (0, 0), (0, HEAD_PAD - qk)))
    wkvb = mla_w_kvb[l].reshape(MLA_KV_LORA, MLA_HEADS, MLA_NOPE + MLA_V)
    wkn = jnp.pad(wkvb[:, :, :MLA_NOPE], ((0, 0), (0, 0), (0, HEAD_PAD - MLA_NOPE)))
    wv_m = wkvb[:, :, MLA_NOPE:]
    n_router = N_GROUPS + N_EXPERTS
    return dict(
        g_mix=norm_mix[l][None, :], g_mem=norm_mem[l][None, :], g_ffn=norm_ffn[l][None, :],
        wqT_g=wq_g.T.astype(BF16), wk_g=wk_g.astype(BF16), wvT_g=wv_g.T.astype(BF16),
        gq=gqa_q_norm[l][:, None], gk=gqa_k_norm[l][None, :],
        wqaT=wq_a.T.astype(BF16), gqa=mla_q_a_norm[l][:, None],
        wqbT=wqb.reshape(MLA_Q_LORA, MLA_HEADS * HEAD_PAD).T.astype(BF16),
        wkva=wkv_a.astype(BF16), gkva=mla_kv_a_norm[l][None, :],
        wkn=wkn.reshape(MLA_KV_LORA, MLA_HEADS * HEAD_PAD).astype(BF16), wkr2=wkr2.astype(BF16),
        wvT_m=wv_m.reshape(MLA_KV_LORA, MLA_HEADS * MLA_V).T.astype(BF16),
        wmiscT=jnp.concatenate([wq_c, w_gate], axis=1).T.astype(BF16),
        wmk=mem_w_kv[l][:, :MEM_HEADS * MEM_HD].astype(BF16),
        wmvT=mem_w_kv[l][:, MEM_HEADS * MEM_HD:].T.astype(BF16),
        w_out=w_out[l].astype(BF16),
        wrT=jnp.concatenate([w_group[l].T, w_expert[l].T, jnp.zeros((ROUTER_LANES - n_router, D_MODEL), F32)], axis=0),
        br=jnp.concatenate([b_group[l], b_expert[l], jnp.zeros((ROUTER_LANES - n_router,), F32)])[:, None],
        wgu=w_gate_up[l].astype(BF16), wd=w_down[l].astype(BF16),
    )


def _mixer(x, mem, p, tabs):
    b, s, _ = x.shape
    tile = min(TOK_TILE, s)
    n_t = s // tile
    grid = (b, n_t)
    x_spec = pl.BlockSpec((1, tile, D_MODEL), lambda bi, i: (bi, i, 0))
    colT = lambda rows: pl.BlockSpec((rows, tile), lambda bi, i: (0, i))
    rowF = pl.BlockSpec((tile, HEAD_PAD), lambda bi, i: (i, 0))
    headT = lambda heads: pl.BlockSpec((1, heads, 1, HEAD_PAD, tile), lambda bi, i: (bi, 0, i, 0, 0))
    keyF = lambda heads: pl.BlockSpec((1, heads, tile, HEAD_PAD), lambda bi, i: (bi, 0, i, 0))
    valT = lambda heads: pl.BlockSpec((1, heads, 1, V_ROWS, tile), lambda bi, i: (bi, 0, i, 0, 0))

    qg, kg, vg = pl.pallas_call(
        _proj_gqa_kernel,
        grid=grid,
        in_specs=[x_spec, _const_spec((1, D_MODEL)), _const_spec(p["wqT_g"].shape), _const_spec(p["wk_g"].shape),
                  _const_spec(p["wvT_g"].shape), _const_spec((GQA_HD, 1)), _const_spec((1, GQA_HD)),
                  colT(GQA_HD // 2), colT(GQA_HD // 2), rowF, rowF],
        out_specs=[headT(GQA_HEADS), keyF(GQA_KV_HEADS), valT(GQA_KV_HEADS)],
        out_shape=[jax.ShapeDtypeStruct((b, GQA_HEADS, n_t, HEAD_PAD, tile), BF16),
                   jax.ShapeDtypeStruct((b, GQA_KV_HEADS, s, HEAD_PAD), BF16),
                   jax.ShapeDtypeStruct((b, GQA_KV_HEADS, n_t, V_ROWS, tile), BF16)],
        compiler_params=_cparams(2),
        name="proj_gqa",
    )(x, p["g_mix"], p["wqT_g"], p["wk_g"], p["wvT_g"], p["gq"], p["gk"],
      tabs["cT_g"], tabs["sT_g"], tabs["cF_g"], tabs["sF_g"])

    qm, km, vm = pl.pallas_call(
        _proj_mla_kernel,
        grid=grid,
        in_specs=[x_spec, _const_spec((1, D_MODEL)), _const_spec(p["wqaT"].shape), _const_spec((MLA_Q_LORA, 1)),
                  _const_spec(p["wqbT"].shape), _const_spec(p["wkva"].shape), _const_spec((1, MLA_KV_LORA)),
                  _const_spec(p["wkn"].shape), _const_spec(p["wkr2"].shape), _const_spec(p["wvT_m"].shape),
                  colT(MLA_ROPE // 2), colT(MLA_ROPE // 2), rowF, rowF],
        out_specs=[headT(MLA_HEADS), keyF(MLA_HEADS), valT(MLA_HEADS)],
        out_shape=[jax.ShapeDtypeStruct((b, MLA_HEADS, n_t, HEAD_PAD, tile), BF16),
                   jax.ShapeDtypeStruct((b, MLA_HEADS, s, HEAD_PAD), BF16),
                   jax.ShapeDtypeStruct((b, MLA_HEADS, n_t, V_ROWS, tile), BF16)],
        compiler_params=_cparams(2),
        name="proj_mla",
    )(x, p["g_mix"], p["wqaT"], p["gqa"], p["wqbT"], p["wkva"], p["gkva"], p["wkn"], p["wkr2"], p["wvT_m"],
      tabs["cT_m"], tabs["sT_m"], tabs["cP_m"], tabs["sP_m"])

    featT = lambda rows: pl.BlockSpec((1, rows, tile), lambda bi, i: (bi, 0, i))
    qc, gates = pl.pallas_call(
        _proj_misc_kernel,
        grid=grid,
        in_specs=[x_spec, _const_spec((1, D_MODEL)), _const_spec(p["wmiscT"].shape)],
        out_specs=[featT(D_MODEL), featT(N_BRANCH * D_MODEL)],
        out_shape=[jax.ShapeDtypeStruct((b, D_MODEL, s), BF16),
                   jax.ShapeDtypeStruct((b, N_BRANCH * D_MODEL, s), BF16)],
        compiler_params=_cparams(2),
        name="proj_misc",
    )(x, p["g_mix"], p["wmiscT"])

    n_mem = mem.shape[1]
    mk, mvT = pl.pallas_call(
        _mem_kv_kernel,
        grid=(b,),
        in_specs=[pl.BlockSpec((1, n_mem, D_MODEL), lambda bi: (bi, 0, 0)), _const_spec((1, D_MODEL)),
                  _const_spec(p["wmk"].shape), _const_spec(p["wmvT"].shape)],
        out_specs=[pl.BlockSpec((1, n_mem, D_MODEL), lambda bi: (bi, 0, 0)),
                   pl.BlockSpec((1, D_MODEL, n_mem), lambda bi: (bi, 0, 0))],
        out_shape=[jax.ShapeDtypeStruct((b, n_mem, D_MODEL), BF16), jax.ShapeDtypeStruct((b, D_MODEL, n_mem), BF16)],
        compiler_params=_cparams(1),
        name="mem_kv",
    )(mem, p["g_mem"], p["wmk"], p["wmvT"])

    oa = _attention(qg, kg, vg, GQA_HEADS // GQA_KV_HEADS)
    ob = _attention(qm, km, vm, 1)
    oc = pl.pallas_call(
        _mem_attn_kernel,
        grid=grid,
        in_specs=[featT(D_MODEL), pl.BlockSpec((1, n_mem, D_MODEL), lambda bi, i: (bi, 0, 0)),
                  pl.BlockSpec((1, D_MODEL, n_mem), lambda bi, i: (bi, 0, 0))],
        out_specs=featT(D_MODEL),
        out_shape=jax.ShapeDtypeStruct((b, D_MODEL, s), BF16),
        compiler_params=_cparams(2),
        name="mem_attention",
    )(qc, mk, mvT)

    return pl.pallas_call(
        _merge_kernel,
        grid=grid,
        in_specs=[headT(GQA_HEADS), headT(MLA_HEADS), featT(D_MODEL), featT(N_BRANCH * D_MODEL), x_spec,
                  _const_spec((D_MODEL, D_MODEL))],
        out_specs=x_spec,
        out_shape=jax.ShapeDtypeStruct(x.shape, F32),
        compiler_params=_cparams(2),
        name="merge_out",
    )(oa, ob, oc, gates, x, p["w_out"])


def _trunk(x, mem, layers, g_final):
    b, s, d = x.shape
    tabs = _rope_tables(s)
    for l, p in enumerate(layers):
        x = _mixer(x, mem, p, tabs)
        x = _moe(x.reshape(b * s, d), p["g_ffn"], p["wrT"], p["br"], p["wgu"], p["wd"], g_final,
                 final_norm=(l == len(layers) - 1)).reshape(b, s, d)
    return x


def kernel(x_prompt, x_sample, mem_prompt, mem_sample, norm_mix, norm_mem, w_in, gqa_q_norm, gqa_k_norm, mla_q_a_norm, mla_kv_a_norm, mla_w_qb, mla_w_kvb, mem_w_kv, w_out, norm_ffn, w_group, b_group, w_expert, b_expert, w_gate_up, w_down, norm_final):
    depth = w_in.shape[0]
    layers = [_layer_params(l, norm_mix, norm_mem, w_in, gqa_q_norm, gqa_k_norm, mla_q_a_norm, mla_kv_a_norm,
                            mla_w_qb, mla_w_kvb, mem_w_kv, w_out, norm_ffn, w_group, b_group, w_expert, b_expert,
                            w_gate_up, w_down) for l in range(depth)]
    g_final = norm_final[None, :]
    return (_trunk(x_prompt, mem_prompt, layers, g_final), _trunk(x_sample, mem_sample, layers, g_final))
```

```python
import functools
import math

import jax
import jax.numpy as jnp
from jax import lax
from jax.experimental import pallas as pl
from jax.experimental.pallas import tpu as pltpu

F32 = jnp.float32
BF16 = jnp.bfloat16

D_MODEL = 1024
EPS = 1e-6
GRID_W = 64
ROPE_THETA = 10000.0
GQA_HEADS, GQA_KV_HEADS, GQA_HD = 8, 2, 128
MLA_HEADS, MLA_Q_LORA, MLA_KV_LORA = 8, 384, 256
MLA_NOPE, MLA_ROPE, MLA_V = 64, 32, 128
MEM_HEADS, MEM_HD = 4, 256
N_BRANCH = 3
N_GROUPS, EXPERTS_PER_GROUP, D_EXPERT = 8, 8, 256
N_EXPERTS = N_GROUPS * EXPERTS_PER_GROUP
HEAD_PAD = 128
BF16_SUBLANES = 16
V_ROWS = HEAD_PAD + BF16_SUBLANES
LOG2E = 1.4426950408889634

TOK_TILE = 512
ATT_Q_COLS = 2048
ATT_CHUNKS_PER_BODY = 4
MOE_ROWS = 256
ROUTER_LANES = 128
ROW_SUBLANES = D_MODEL // 128
DMA_UNROLL = 8
VMEM_LIMIT = 56 * 1024 * 1024


def _cparams(n_axes, flags=None):
    return pltpu.CompilerParams(dimension_semantics=("arbitrary",) * n_axes, vmem_limit_bytes=VMEM_LIMIT,
                                flags=flags)


def _nt(a, b):
    return lax.dot_general(a, b, (((1,), (1,)), ((), ())), preferred_element_type=F32)


def _rms_rows(x, g):
    return x * lax.rsqrt(jnp.mean(x * x, axis=-1, keepdims=True) + EPS) * g


def _rms_cols(x, g):
    return x * lax.rsqrt(jnp.mean(x * x, axis=0, keepdims=True) + EPS) * g


def _const_spec(shape):
    return pl.BlockSpec(shape, lambda *_: (0,) * len(shape))


def _store_values(v_ref, vT):
    t = vT.shape[1]
    v_ref[0:HEAD_PAD, :] = vT.astype(BF16)
    ones_row = lax.broadcasted_iota(jnp.int32, (BF16_SUBLANES, t), 0) == 0
    v_ref[HEAD_PAD:V_ROWS, :] = jnp.where(ones_row, 1.0, 0.0).astype(BF16)


def _proj_gqa_kernel(x_ref, g_ref, wqT_ref, wk_ref, wvT_ref, gq_ref, gk_ref, cT_ref, sT_ref, cF_ref, sF_ref,
                     q_ref, k_ref, v_ref):
    h = _rms_rows(x_ref[0], g_ref[...]).astype(BF16)
    c, s, gq = cT_ref[...], sT_ref[...], gq_ref[...]
    half = GQA_HD // 2
    scale = GQA_HD ** -0.5 * LOG2E
    heads_per_dot = 4
    for h0 in range(0, GQA_HEADS, heads_per_dot):
        qT = _nt(wqT_ref[h0 * GQA_HD:(h0 + heads_per_dot) * GQA_HD, :], h)
        for hd in range(heads_per_dot):
            qn = _rms_cols(qT[hd * GQA_HD:(hd + 1) * GQA_HD], gq)
            x1, x2 = qn[:half], qn[half:]
            o = jnp.concatenate([x1 * c - x2 * s, x2 * c + x1 * s], axis=0) * scale
            q_ref[0, h0 + hd, 0] = o.astype(BF16)
    k = jnp.dot(h, wk_ref[...], preferred_element_type=F32)
    for j in range(GQA_KV_HEADS):
        kn = _rms_rows(k[:, j * GQA_HD:(j + 1) * GQA_HD], gk_ref[...])
        ko = kn * cF_ref[...] + pltpu.roll(kn, half, 1) * sF_ref[...]
        k_ref[0, j] = ko.astype(BF16)
    vT = _nt(wvT_ref[...], h)
    for j in range(GQA_KV_HEADS):
        _store_values(v_ref.at[0, j, 0], vT[j * GQA_HD:(j + 1) * GQA_HD])


def _proj_mla_kernel(x_ref, g_ref, wqaT_ref, gqa_ref, wqbT_ref, wkva_ref, gkva_ref, wkn_ref, wkr_ref, wvT_ref,
                     cT_ref, sT_ref, cP_ref, sP_ref, q_ref, k_ref, v_ref):
    h = _rms_rows(x_ref[0], g_ref[...]).astype(BF16)
    qa = _rms_cols(_nt(wqaT_ref[...], h), gqa_ref[...]).astype(BF16)
    qT = jnp.dot(wqbT_ref[...], qa, preferred_element_type=F32)
    c, s = cT_ref[...], sT_ref[...]
    hr = MLA_ROPE // 2
    scale = (MLA_NOPE + MLA_ROPE) ** -0.5 * LOG2E
    for hd in range(MLA_HEADS):
        q = qT[hd * HEAD_PAD:(hd + 1) * HEAD_PAD]
        x1, x2 = q[MLA_NOPE:MLA_NOPE + hr], q[MLA_NOPE + hr:MLA_NOPE + 2 * hr]
        o = jnp.concatenate([q[:MLA_NOPE], x1 * c - x2 * s, x2 * c + x1 * s, q[MLA_NOPE + 2 * hr:]], axis=0) * scale
        q_ref[0, hd, 0] = o.astype(BF16)
    kva = jnp.dot(h, wkva_ref[...], preferred_element_type=F32)
    kvn = _rms_rows(kva, gkva_ref[...]).astype(BF16)
    kr2 = jnp.dot(h, wkr_ref[...], preferred_element_type=F32)
    kr = kr2[:, :HEAD_PAD] * cP_ref[...] + kr2[:, HEAD_PAD:] * sP_ref[...]
    kn = jnp.dot(kvn, wkn_ref[...], preferred_element_type=F32)
    for hd in range(MLA_HEADS):
        k_ref[0, hd] = (kn[:, hd * HEAD_PAD:(hd + 1) * HEAD_PAD] + kr).astype(BF16)
    vT = _nt(wvT_ref[...], kvn)
    for hd in range(MLA_HEADS):
        _store_values(v_ref.at[0, hd, 0], vT[hd * MLA_V:(hd + 1) * MLA_V])


def _proj_misc_kernel(x_ref, g_ref, wT_ref, qc_ref, gate_ref):
    h = _rms_rows(x_ref[0], g_ref[...]).astype(BF16)
    scale = MEM_HD ** -0.5 * LOG2E
    rows = 512
    n_q = D_MODEL // rows
    for ci in range((D_MODEL + N_BRANCH * D_MODEL) // rows):
        y = _nt(wT_ref[ci * rows:(ci + 1) * rows, :], h)
        if ci < n_q:
            qc_ref[0, ci * rows:(ci + 1) * rows, :] = (y * scale).astype(BF16)
        else:
            r0 = (ci - n_q) * rows
            gate_ref[0, r0:r0 + rows, :] = jax.nn.sigmoid(y).astype(BF16)


def _mem_kv_kernel(mem_ref, g_ref, wk_ref, wvT_ref, mk_ref, mvT_ref):
    h = _rms_rows(mem_ref[0], g_ref[...]).astype(BF16)
    mk_ref[0] = jnp.dot(h, wk_ref[...], preferred_element_type=F32).astype(BF16)
    mvT_ref[0] = _nt(wvT_ref[...], h).astype(BF16)


def _attn_kernel(q_ref, k_ref, v_ref, o_ref, m_sc, acc_sc, s_sc, mc_sc, *, chunks_per_head, n_kv):
    chunk = TOK_TILE
    n_sub = m_sc.shape[0]

    m_sc[...] = jnp.full(m_sc.shape, -jnp.inf, F32)
    acc_sc[...] = jnp.zeros(acc_sc.shape, F32)

    def scores(c, u, slot):
        k = k_ref[0, 0, pl.ds(pl.multiple_of(c * chunk, chunk), chunk), :]
        s = jnp.dot(k, q_ref[0, u // chunks_per_head, u % chunks_per_head],
                    preferred_element_type=F32)
        s_sc[slot, u] = s
        mc_sc[slot, u] = jnp.max(s, axis=0, keepdims=True)

    def softmax_values(c, u, slot):
        m_prev = m_sc[u]
        m_cur = jnp.maximum(m_prev, mc_sc[slot, u])
        alpha = jnp.exp2(m_prev - m_cur)
        p = jnp.exp2(s_sc[slot, u] - m_cur)
        acc_sc[u] = alpha * acc_sc[u] + jnp.dot(v_ref[0, 0, c], p.astype(BF16), preferred_element_type=F32)
        m_sc[u] = m_cur

    def stage(c_scores, c_values):
        for u in range(n_sub):
            if c_scores is not None:
                scores(c_scores, u, c_scores % 2)
            if c_values is not None:
                softmax_values(c_values, u, c_values % 2)

    per_body = ATT_CHUNKS_PER_BODY if (n_kv - 1) // ATT_CHUNKS_PER_BODY >= 2 else 2
    n_bodies = (n_kv - 1) // per_body
    stage(0, None)

    def body(i, carry):
        for t in range(per_body):
            for u in range(n_sub):
                scores(per_body * i + t + 1, u, (t + 1) % 2)
                softmax_values(per_body * i + t, u, t % 2)
        return carry

    lax.fori_loop(0, n_bodies, body, 0)
    for c in range(per_body * n_bodies, n_kv - 1):
        stage(c + 1, c)
    stage(None, n_kv - 1)

    for u in range(n_sub):
        o_ref[0, u // chunks_per_head, u % chunks_per_head] = (
            acc_sc[u, 0:HEAD_PAD] / acc_sc[u, HEAD_PAD:HEAD_PAD + 1]).astype(BF16)


def _attention(qT, k, vT, group):
    b, h, n_kv, hd, chunk = qT.shape
    hkv = k.shape[1]
    s = n_kv * chunk
    assert h == hkv * group and chunk == TOK_TILE
    chunks_per_head = min(ATT_Q_COLS // group, s) // chunk
    n_sub = group * chunks_per_head
    kern = functools.partial(_attn_kernel, chunks_per_head=chunks_per_head, n_kv=n_kv)
    q_spec = pl.BlockSpec((1, group, chunks_per_head, hd, chunk), lambda bi, j, qi: (bi, j, qi, 0, 0))
    return pl.pallas_call(
        kern,
        grid=(b, hkv, n_kv // chunks_per_head),
        in_specs=[
            q_spec,
            pl.BlockSpec((1, 1, s, hd), lambda bi, j, qi: (bi, j, 0, 0)),
            pl.BlockSpec((1, 1, n_kv, V_ROWS, chunk), lambda bi, j, qi: (bi, j, 0, 0, 0)),
        ],
        out_specs=q_spec,
        out_shape=jax.ShapeDtypeStruct(qT.shape, BF16),
        scratch_shapes=[
            pltpu.VMEM((n_sub, 1, chunk), F32),
            pltpu.VMEM((n_sub, V_ROWS, chunk), F32),
            pltpu.VMEM((2, n_sub, chunk, chunk), F32),
            pltpu.VMEM((2, n_sub, 1, chunk), F32),
        ],
        compiler_params=_cparams(3),
        name="attention",
    )(qT, k, vT)


def _mem_attn_kernel(qc_ref, mk_ref, mvT_ref, o_ref):
    rows = [slice(hd * MEM_HD, (hd + 1) * MEM_HD) for hd in range(MEM_HEADS)]
    score = lambda r: jnp.dot(mk_ref[0, :, r], qc_ref[0, r, :], preferred_element_type=F32)
    s_next = score(rows[0])
    for hd in range(MEM_HEADS):
        r, s = rows[hd], s_next
        if hd + 1 < MEM_HEADS:
            s_next = score(rows[hd + 1])
        p = jnp.exp2(s - jnp.max(s, axis=0, keepdims=True))
        l = jnp.sum(p, axis=0, keepdims=True)
        o = jnp.dot(mvT_ref[0, r, :], p.astype(BF16), preferred_element_type=F32)
        o_ref[0, r, :] = (o / l).astype(BF16)


def _merge_kernel(oa_ref, ob_ref, oc_ref, gate_ref, x_ref, w_ref, y_ref):
    t = x_ref.shape[1]
    oa = oa_ref[0].reshape(D_MODEL, t)
    ob = ob_ref[0].reshape(D_MODEL, t)
    merged = (gate_ref[0, 0:D_MODEL, :].astype(F32) * oa.astype(F32)
              + gate_ref[0, D_MODEL:2 * D_MODEL, :].astype(F32) * ob.astype(F32)
              + gate_ref[0, 2 * D_MODEL:, :].astype(F32) * oc_ref[0].astype(F32)).astype(BF16)
    out = lax.dot_general(merged, w_ref[...], (((0,), (0,)), ((), ())), preferred_element_type=F32)
    y_ref[0] = x_ref[0] + out


def _first_index_of_max(v, vmax, iota, n):
    return jnp.min(jnp.where(v == vmax, iota, n), axis=0, keepdims=True)


def _router_kernel(x_ref, g_ref, wT_ref, b_ref, ri_ref, rw_ref, cnt_ref, carry_sc):
    step = pl.program_id(0)

    @pl.when(step == 0)
    def _():
        carry_sc[...] = jnp.zeros(carry_sc.shape, F32)

    h = _rms_rows(x_ref[...], g_ref[...])
    t = h.shape[0]
    logits = lax.dot_general(wT_ref[...], h, (((1,), (1,)), ((), ())), precision=lax.Precision.HIGHEST,
                             preferred_element_type=F32) + b_ref[...]
    iota_g = lax.broadcasted_iota(jnp.int32, (N_GROUPS, t), 0)
    gl = logits[0:N_GROUPS]
    gmax = jnp.max(gl, axis=0, keepdims=True)
    gidx = _first_index_of_max(gl, gmax, iota_g, N_GROUPS)
    gw = 1.0 / jnp.sum(jnp.exp(gl - gmax), axis=0, keepdims=True)
    sel = jnp.zeros((EXPERTS_PER_GROUP, t), F32)
    for g in range(N_GROUPS):
        r0 = N_GROUPS + g * EXPERTS_PER_GROUP
        sel = sel + jnp.where(gidx == g, logits[r0:r0 + EXPERTS_PER_GROUP], 0.0)
    v1 = jnp.max(sel, axis=0, keepdims=True)
    i1 = _first_index_of_max(sel, v1, iota_g, EXPERTS_PER_GROUP)
    sel2 = jnp.where(iota_g == i1, -jnp.inf, sel)
    v2 = jnp.max(sel2, axis=0, keepdims=True)
    i2 = _first_index_of_max(sel2, v2, iota_g, EXPERTS_PER_GROUP)
    e = jnp.exp(v2 - v1)
    w1 = gw / (1.0 + e)
    w2 = gw * e / (1.0 + e)
    e1 = gidx * EXPERTS_PER_GROUP + i1
    e2 = gidx * EXPERTS_PER_GROUP + i2

    iota_e = lax.broadcasted_iota(jnp.int32, (N_EXPERTS, t), 0)
    hit1 = iota_e == e1
    hit2 = iota_e == e2
    hits = jnp.where(hit1 | hit2, 1.0, 0.0)
    before = (lax.broadcasted_iota(jnp.int32, (t, t), 0) < lax.broadcasted_iota(jnp.int32, (t, t), 1))
    prefix = jnp.dot(hits.astype(BF16), jnp.where(before, 1.0, 0.0).astype(BF16),
                     preferred_element_type=F32) + carry_sc[...]
    r1 = jnp.sum(jnp.where(hit1, prefix, 0.0), axis=0, keepdims=True)
    r2 = jnp.sum(jnp.where(hit2, prefix, 0.0), axis=0, keepdims=True)
    carry_sc[...] = carry_sc[...] + jnp.sum(hits, axis=1, keepdims=True)

    ri_ref[...] = jnp.zeros(ri_ref.shape, jnp.int32)
    ri_ref[0:1, :] = e1
    ri_ref[1:2, :] = e2
    ri_ref[2:3, :] = r1.astype(jnp.int32)
    ri_ref[3:4, :] = r2.astype(jnp.int32)
    rw_ref[...] = jnp.zeros(rw_ref.shape, F32)
    rw_ref[0:1, :] = w1
    rw_ref[1:2, :] = w2
    cnt_ref[...] = jnp.broadcast_to(carry_sc[...], cnt_ref.shape)


def _dest_kernel(cnt_ref, ri_ref, dest_ref, blk_ref, nused_ref, padinfo_ref, *, n_blocks_padded):
    cnt = cnt_ref[...]
    pad = jnp.floor((cnt + (MOE_ROWS - 1)) * (1.0 / MOE_ROWS)) * MOE_ROWS
    lower = jnp.where(lax.broadcasted_iota(jnp.int32, (N_EXPERTS, N_EXPERTS), 1)
                      <= lax.broadcasted_iota(jnp.int32, (N_EXPERTS, N_EXPERTS), 0), 1.0, 0.0)
    pad_end = jnp.dot(lower, pad, precision=lax.Precision.HIGHEST, preferred_element_type=F32)
    pad_start = (pad_end - pad)[:, 0:1]
    t = ri_ref.shape[1]
    iota_e = lax.broadcasted_iota(jnp.int32, (N_EXPERTS, t), 0)
    dest_ref[...] = jnp.zeros(dest_ref.shape, jnp.int32)
    for kk in range(2):
        start = jnp.sum(jnp.where(iota_e == ri_ref[kk:kk + 1, :], pad_start, 0.0), axis=0, keepdims=True)
        dest_ref[kk:kk + 1, :] = (start.astype(jnp.int32) + ri_ref[2 + kk:3 + kk, :]) * ROW_SUBLANES
    blk_start = (lax.broadcasted_iota(jnp.int32, (N_EXPERTS, n_blocks_padded), 1) * MOE_ROWS).astype(F32)
    owner = jnp.sum(jnp.where(pad_end[:, 0:1] <= blk_start, 1.0, 0.0), axis=0, keepdims=True)
    blk_ref[...] = jnp.broadcast_to(jnp.minimum(owner, N_EXPERTS - 1.0).astype(jnp.int32), blk_ref.shape)
    nused_ref[...] = jnp.broadcast_to((pad_end[N_EXPERTS - 1:N_EXPERTS, :] * (1.0 / MOE_ROWS)).astype(jnp.int32),
                                      nused_ref.shape)
    diag = (lax.broadcasted_iota(jnp.int32, (N_EXPERTS, 128), 0) == lax.broadcasted_iota(jnp.int32, (N_EXPERTS, 128), 1))
    first_pad = jnp.sum(jnp.where(diag, (pad_end - pad + cnt) * ROW_SUBLANES, 0.0), axis=0, keepdims=True)
    n_pad = jnp.sum(jnp.where(diag, pad - cnt, 0.0), axis=0, keepdims=True)
    padinfo_ref[...] = jnp.zeros(padinfo_ref.shape, jnp.int32)
    padinfo_ref[0:1, :] = first_pad.astype(jnp.int32)
    padinfo_ref[1:2, :] = n_pad.astype(jnp.int32)
    padinfo_ref[2:3, :] = (pad_end[N_EXPERTS - 1:N_EXPERTS, :] * (1.0 / MOE_ROWS)).astype(jnp.int32)


def _to_row_tiles(ref, x):
    n = x.shape[0]
    for c in range(ROW_SUBLANES):
        ref[pl.ds(c, n, stride=ROW_SUBLANES), :] = x[:, c * 128:(c + 1) * 128]


def _from_row_tiles(ref, n):
    return jnp.concatenate([ref[pl.ds(c, n, stride=ROW_SUBLANES), :] for c in range(ROW_SUBLANES)], axis=1)


def _tile_copy(src_ref, src_row8, dst_ref, dst_row8, sem):
    return pltpu.make_async_copy(src_ref.at[pl.ds(pl.multiple_of(src_row8, ROW_SUBLANES), ROW_SUBLANES)],
                                 dst_ref.at[pl.ds(pl.multiple_of(dst_row8, ROW_SUBLANES), ROW_SUBLANES)], sem)


def _dispatch_kernel(dest_ref, padinfo_ref, x_ref, g_ref, xs_ref, h_sc, zero_sc, sem, blk_sem, *, tile,
                     experts_per_step, n_blocks):
    step = pl.program_id(0)
    _to_row_tiles(h_sc, _rms_rows(x_ref[...], g_ref[...]))
    zero_sc[...] = jnp.zeros(zero_sc.shape, F32)

    def block_copy(b):
        rows = MOE_ROWS * ROW_SUBLANES
        return pltpu.make_async_copy(zero_sc, xs_ref.at[pl.ds(pl.multiple_of(b * rows, rows), rows)], blk_sem)

    for j in range(experts_per_step):
        b = padinfo_ref[2, 0] + step * experts_per_step + j

        @pl.when(b < n_blocks)
        def _():
            block_copy(b).start()

    def issue(t, carry):
        _tile_copy(h_sc, t * ROW_SUBLANES, xs_ref, dest_ref[0, 0, t], sem).start()
        _tile_copy(h_sc, t * ROW_SUBLANES, xs_ref, dest_ref[0, 0, tile + t], sem).start()
        return carry

    def drain(t, carry):
        _tile_copy(h_sc, 0, xs_ref, 0, sem).wait()
        _tile_copy(h_sc, 0, xs_ref, 0, sem).wait()
        return carry

    lax.fori_loop(0, tile, issue, 0, unroll=DMA_UNROLL)

    n_zero = jnp.int32(0)
    for j in range(experts_per_step):
        e = step * experts_per_step + j
        e_c = jnp.minimum(e, N_EXPERTS - 1)
        first = padinfo_ref[0, e_c]
        count = jnp.where(e < N_EXPERTS, padinfo_ref[1, e_c], 0)

        def zero_row(r, carry, first=first):
            _tile_copy(zero_sc, 0, xs_ref, first + r * ROW_SUBLANES, sem).start()
            return carry

        lax.fori_loop(0, count, zero_row, 0)
        n_zero = n_zero + count

    lax.fori_loop(0, tile, drain, 0, unroll=DMA_UNROLL)
    lax.fori_loop(0, n_zero, lambda r, carry: (_tile_copy(zero_sc, 0, xs_ref, 0, sem).wait(), carry)[1], 0)
    for j in range(experts_per_step):
        b = padinfo_ref[2, 0] + step * experts_per_step + j

        @pl.when(b < n_blocks)
        def _():
            block_copy(b).wait()


def _expert_kernel(blk_ref, nused_ref, xs_ref, wgu_ref, wd_ref, ys_ref):
    b = pl.program_id(0)

    @pl.when(b < nused_ref[0])
    def _():
        x = _from_row_tiles(xs_ref, MOE_ROWS).astype(BF16)
        gu = jnp.dot(x, wgu_ref[0], preferred_element_type=F32)
        a, u = gu[:, :D_EXPERT], gu[:, D_EXPERT:]
        hidden = (a * jax.nn.sigmoid(a) * u).astype(BF16)
        _to_row_tiles(ys_ref, jnp.dot(hidden, wd_ref[0], preferred_element_type=F32))

    @pl.when(b >= nused_ref[0])
    def _():
        ys_ref[...] = jnp.zeros(ys_ref.shape, F32)


def _combine_kernel(dest_ref, dest_next_ref, ys_ref, x_ref, w_ref, g_ref, y_ref, buf, sem, *, tile, final_norm):
    step = pl.program_id(0)
    slot = lax.rem(step, 2)

    def issue(d_ref, slot_):
        def body(t, carry):
            _tile_copy(ys_ref, d_ref[0, 0, t], buf.at[slot_, 0], t * ROW_SUBLANES, sem.at[slot_]).start()
            _tile_copy(ys_ref, d_ref[0, 0, tile + t], buf.at[slot_, 1], t * ROW_SUBLANES, sem.at[slot_]).start()
            return carry
        lax.fori_loop(0, tile, body, 0, unroll=DMA_UNROLL)

    @pl.when(step == 0)
    def _():
        issue(dest_ref, 0)

    @pl.when(step + 1 < pl.num_programs(0))
    def _():
        issue(dest_next_ref, 1 - slot)

    def drain(t, carry):
        _tile_copy(ys_ref, 0, buf.at[slot, 0], 0, sem.at[slot]).wait()
        _tile_copy(ys_ref, 0, buf.at[slot, 1], 0, sem.at[slot]).wait()
        return carry

    lax.fori_loop(0, tile, drain, 0, unroll=DMA_UNROLL)
    w = w_ref[...]
    y = x_ref[...] + (w[:, 0:1] * _from_row_tiles(buf.at[slot, 0], tile)
                      + w[:, 1:2] * _from_row_tiles(buf.at[slot, 1], tile))
    if final_norm:
        y = _rms_rows(y, g_ref[...])
    y_ref[...] = y


def _moe(x, g_ffn, wrT, br, wgu, wd, g_final, final_norm):
    n_tok = x.shape[0]
    tile = min(TOK_TILE, n_tok)
    n_tiles = n_tok // tile
    n_blocks = (2 * n_tok + MOE_ROWS - 1) // MOE_ROWS + N_EXPERTS
    n_blocks_padded = (n_blocks + 127) // 128 * 128
    n_rows = n_blocks * MOE_ROWS

    ri, rw, cnt = pl.pallas_call(
        _router_kernel,
        grid=(n_tiles,),
        in_specs=[
            pl.BlockSpec((tile, D_MODEL), lambda i: (i, 0)),
            _const_spec((1, D_MODEL)),
            _const_spec((ROUTER_LANES, D_MODEL)),
            _const_spec((ROUTER_LANES, 1)),
        ],
        out_specs=[
            pl.BlockSpec((8, tile), lambda i: (0, i)),
            pl.BlockSpec((8, tile), lambda i: (0, i)),
            _const_spec((N_EXPERTS, 128)),
        ],
        out_shape=[
            jax.ShapeDtypeStruct((8, n_tok), jnp.int32),
            jax.ShapeDtypeStruct((8, n_tok), F32),
            jax.ShapeDtypeStruct((N_EXPERTS, 128), F32),
        ],
        scratch_shapes=[pltpu.VMEM((N_EXPERTS, 1), F32)],
        compiler_params=_cparams(1),
        name="moe_router",
    )(x, g_ffn, wrT, br)

    dest, blk, nused, padinfo = pl.pallas_call(
        functools.partial(_dest_kernel, n_blocks_padded=n_blocks_padded),
        grid=(n_tiles,),
        in_specs=[_const_spec((N_EXPERTS, 128)), pl.BlockSpec((8, tile), lambda i: (0, i))],
        out_specs=[
            pl.BlockSpec((8, tile), lambda i: (0, i)),
            _const_spec((8, n_blocks_padded)),
            _const_spec((8, 128)),
            _const_spec((8, 128)),
        ],
        out_shape=[
            jax.ShapeDtypeStruct((8, n_tok), jnp.int32),
            jax.ShapeDtypeStruct((8, n_blocks_padded), jnp.int32),
            jax.ShapeDtypeStruct((8, 128), jnp.int32),
            jax.ShapeDtypeStruct((8, 128), jnp.int32),
        ],
        compiler_params=_cparams(1),
        name="moe_dest",
    )(cnt, ri)

    dest_tiles = dest[0:2].reshape(2, n_tiles, tile).transpose(1, 0, 2).reshape(n_tiles, 1, 2 * tile)
    dest_spec = pl.BlockSpec((1, 1, 2 * tile), lambda i: (i, 0, 0), memory_space=pltpu.SMEM)
    any_spec = pl.BlockSpec(memory_space=pl.ANY)

    xs = pl.pallas_call(
        functools.partial(_dispatch_kernel, tile=tile, experts_per_step=pl.cdiv(N_EXPERTS, n_tiles),
                          n_blocks=n_blocks),
        grid=(n_tiles,),
        in_specs=[dest_spec, pl.BlockSpec((8, 128), lambda i: (0, 0), memory_space=pltpu.SMEM),
                  pl.BlockSpec((tile, D_MODEL), lambda i: (i, 0)), _const_spec((1, D_MODEL))],
        out_specs=any_spec,
        out_shape=jax.ShapeDtypeStruct((n_rows * ROW_SUBLANES, 128), F32),
        scratch_shapes=[pltpu.VMEM((tile * ROW_SUBLANES, 128), F32),
                        pltpu.VMEM((MOE_ROWS * ROW_SUBLANES, 128), F32),
                        pltpu.SemaphoreType.DMA(()), pltpu.SemaphoreType.DMA(())],
        compiler_params=_cparams(1),
        name="moe_dispatch",
    )(dest_tiles, padinfo, x, g_ffn)

    blk_rows = MOE_ROWS * ROW_SUBLANES
    ys = pl.pallas_call(
        _expert_kernel,
        grid_spec=pltpu.PrefetchScalarGridSpec(
            num_scalar_prefetch=2,
            grid=(n_blocks,),
            in_specs=[
                pl.BlockSpec((blk_rows, 128), lambda b, blk, nu: (b, 0)),
                pl.BlockSpec((1, D_MODEL, 2 * D_EXPERT), lambda b, blk, nu: (blk[b], 0, 0)),
                pl.BlockSpec((1, D_EXPERT, D_MODEL), lambda b, blk, nu: (blk[b], 0, 0)),
            ],
            out_specs=pl.BlockSpec((blk_rows, 128), lambda b, blk, nu: (b, 0)),
        ),
        out_shape=jax.ShapeDtypeStruct((n_rows * ROW_SUBLANES, 128), F32),
        compiler_params=_cparams(1),
        name="moe_experts",
    )(blk[0, :n_blocks], nused[0, :1], xs, wgu, wd)

    ctile = min(256, n_tok)
    c_tiles = n_tok // ctile
    dest_c = dest[0:2].reshape(2, c_tiles, ctile).transpose(1, 0, 2).reshape(c_tiles, 1, 2 * ctile)
    return pl.pallas_call(
        functools.partial(_combine_kernel, tile=ctile, final_norm=final_norm),
        grid=(c_tiles,),
        in_specs=[
            pl.BlockSpec((1, 1, 2 * ctile), lambda i: (i, 0, 0), memory_space=pltpu.SMEM),
            pl.BlockSpec((1, 1, 2 * ctile), lambda i: (jnp.minimum(i + 1, c_tiles - 1), 0, 0),
                         memory_space=pltpu.SMEM),
            any_spec,
            pl.BlockSpec((ctile, D_MODEL), lambda i: (i, 0)),
            pl.BlockSpec((ctile, 2), lambda i: (i, 0)),
            _const_spec((1, D_MODEL)),
        ],
        out_specs=pl.BlockSpec((ctile, D_MODEL), lambda i: (i, 0)),
        out_shape=jax.ShapeDtypeStruct((n_tok, D_MODEL), F32),
        scratch_shapes=[pltpu.VMEM((2, 2, ctile * ROW_SUBLANES, 128), F32), pltpu.SemaphoreType.DMA((2,))],
        compiler_params=_cparams(1),
        name="moe_combine",
    )(dest_c, dest_c, ys, x, rw[0:2].T, g_final)


def _rope_tables(n_tokens):
    rows = n_tokens // GRID_W
    row_idx = jnp.repeat(jnp.arange(rows, dtype=F32), GRID_W)
    col_idx = jnp.tile(jnp.arange(GRID_W, dtype=F32), rows)

    def angles(rot_dim):
        n_freq = rot_dim // 4
        inv_freq = ROPE_THETA ** (-jnp.arange(n_freq, dtype=F32) / n_freq)
        return jnp.concatenate([row_idx[:, None] * inv_freq, col_idx[:, None] * inv_freq], axis=-1)

    ag, am = angles(GQA_HD), angles(MLA_ROPE)
    cg, sg, cm, sm = jnp.cos(ag), jnp.sin(ag), jnp.cos(am), jnp.sin(am)
    hr = MLA_ROPE // 2
    zl = jnp.zeros((n_tokens, MLA_NOPE), F32)
    zr = jnp.zeros((n_tokens, HEAD_PAD - MLA_NOPE - MLA_ROPE), F32)
    del hr
    return dict(
        cT_g=cg.T, sT_g=sg.T,
        cF_g=jnp.concatenate([cg, cg], axis=1), sF_g=jnp.concatenate([-sg, sg], axis=1),
        cT_m=cm.T, sT_m=sm.T,
        cP_m=jnp.concatenate([zl, cm, cm, zr], axis=1), sP_m=jnp.concatenate([zl, -sm, sm, zr], axis=1),
    )


def _layer_params(l, norm_mix, norm_mem, w_in, gqa_q_norm, gqa_k_norm, mla_q_a_norm, mla_kv_a_norm, mla_w_qb,
                  mla_w_kvb, mem_w_kv, w_out, norm_ffn, w_group, b_group, w_expert, b_expert, w_gate_up, w_down):
    splits = []
    off = 0
    for width in (GQA_HEADS * GQA_HD, GQA_KV_HEADS * GQA_HD, GQA_KV_HEADS * GQA_HD, MLA_Q_LORA, MLA_KV_LORA,
                  MLA_ROPE, MEM_HEADS * MEM_HD, N_BRANCH * D_MODEL):
        splits.append(w_in[l][:, off:off + width])
        off += width
    wq_g, wk_g, wv_g, wq_a, wkv_a, wk_r, wq_c, w_gate = splits
    hr = MLA_ROPE // 2
    pad_l = jnp.zeros((D_MODEL, MLA_NOPE), F32)
    pad_r = jnp.zeros((D_MODEL, HEAD_PAD - MLA_NOPE - MLA_ROPE), F32)
    wk_r_swapped = jnp.concatenate([wk_r[:, hr:], wk_r[:, :hr]], axis=1)
    wkr2 = jnp.concatenate([pad_l, wk_r, pad_r, pad_l, wk_r_swapped, pad_r], axis=1)
    qk = MLA_NOPE + MLA_ROPE
    wqb = jnp.pad(mla_w_qb[l].reshape(MLA_Q_LORA, MLA_HEADS, qk), ((0, 0), (0, 0), (0, HEAD_PAD - qk)))
    wkvb = mla_w_kvb[l].reshape(MLA_KV_LORA, MLA_HEADS, MLA_NOPE + MLA_V)
    wkn = jnp.pad(wkvb[:, :, :MLA_NOPE], ((0, 0), (0, 0), (0, HEAD_PAD - MLA_NOPE)))
    wv_m = wkvb[:, :, MLA_NOPE:]
    n_router = N_GROUPS + N_EXPERTS
    return dict(
        g_mix=norm_mix[l][None, :], g_mem=norm_mem[l][None, :], g_ffn=norm_ffn[l][None, :],
        wqT_g=wq_g.T.astype(BF16), wk_g=wk_g.astype(BF16), wvT_g=wv_g.T.astype(BF16),
        gq=gqa_q_norm[l][:, None], gk=gqa_k_norm[l][None, :],
        wqaT=wq_a.T.astype(BF16), gqa=mla_q_a_norm[l][:, None],
        wqbT=wqb.reshape(MLA_Q_LORA, MLA_HEADS * HEAD_PAD).T.astype(BF16),
        wkva=wkv_a.astype(BF16), gkva=mla_kv_a_norm[l][None, :],
        wkn=wkn.reshape(MLA_KV_LORA, MLA_HEADS * HEAD_PAD).astype(BF16), wkr2=wkr2.astype(BF16),
        wvT_m=wv_m.reshape(MLA_KV_LORA, MLA_HEADS * MLA_V).T.astype(BF16),
        wmiscT=jnp.concatenate([wq_c, w_gate], axis=1).T.astype(BF16),
        wmk=mem_w_kv[l][:, :MEM_HEADS * MEM_HD].astype(BF16),
        wmvT=mem_w_kv[l][:, MEM_HEADS * MEM_HD:].T.astype(BF16),
        w_out=w_out[l].astype(BF16),
        wrT=jnp.concatenate([w_group[l].T, w_expert[l].T, jnp.zeros((ROUTER_LANES - n_router, D_MODEL), F32)], axis=0),
        br=jnp.concatenate([b_group[l], b_expert[l], jnp.zeros((ROUTER_LANES - n_router,), F32)])[:, None],
        wgu=w_gate_up[l].astype(BF16), wd=w_down[l].astype(BF16),
    )


def _mixer(x, mem, p, tabs):
    b, s, _ = x.shape
    tile = min(TOK_TILE, s)
    n_t = s // tile
    grid = (b, n_t)
    x_spec = pl.BlockSpec((1, tile, D_MODEL), lambda bi, i: (bi, i, 0))
    colT = lambda rows: pl.BlockSpec((rows, tile), lambda bi, i: (0, i))
    rowF = pl.BlockSpec((tile, HEAD_PAD), lambda bi, i: (i, 0))
    headT = lambda heads: pl.BlockSpec((1, heads, 1, HEAD_PAD, tile), lambda bi, i: (bi, 0, i, 0, 0))
    keyF = lambda heads: pl.BlockSpec((1, heads, tile, HEAD_PAD), lambda bi, i: (bi, 0, i, 0))
    valT = lambda heads: pl.BlockSpec((1, heads, 1, V_ROWS, tile), lambda bi, i: (bi, 0, i, 0, 0))

    qg, kg, vg = pl.pallas_call(
        _proj_gqa_kernel,
        grid=grid,
        in_specs=[x_spec, _const_spec((1, D_MODEL)), _const_spec(p["wqT_g"].shape), _const_spec(p["wk_g"].shape),
                  _const_spec(p["wvT_g"].shape), _const_spec((GQA_HD, 1)), _const_spec((1, GQA_HD)),
                  colT(GQA_HD // 2), colT(GQA_HD // 2), rowF, rowF],
        out_specs=[headT(GQA_HEADS), keyF(GQA_KV_HEADS), valT(GQA_KV_HEADS)],
        out_shape=[jax.ShapeDtypeStruct((b, GQA_HEADS, n_t, HEAD_PAD, tile), BF16),
                   jax.ShapeDtypeStruct((b, GQA_KV_HEADS, s, HEAD_PAD), BF16),
                   jax.ShapeDtypeStruct((b, GQA_KV_HEADS, n_t, V_ROWS, tile), BF16)],
        compiler_params=_cparams(2),
        name="proj_gqa",
    )(x, p["g_mix"], p["wqT_g"], p["wk_g"], p["wvT_g"], p["gq"], p["gk"],
      tabs["cT_g"], tabs["sT_g"], tabs["cF_g"], tabs["sF_g"])

    qm, km, vm = pl.pallas_call(
        _proj_mla_kernel,
        grid=grid,
        in_specs=[x_spec, _const_spec((1, D_MODEL)), _const_spec(p["wqaT"].shape), _const_spec((MLA_Q_LORA, 1)),
                  _const_spec(p["wqbT"].shape), _const_spec(p["wkva"].shape), _const_spec((1, MLA_KV_LORA)),
                  _const_spec(p["wkn"].shape), _const_spec(p["wkr2"].shape), _const_spec(p["wvT_m"].shape),
                  colT(MLA_ROPE // 2), colT(MLA_ROPE // 2), rowF, rowF],
        out_specs=[headT(MLA_HEADS), keyF(MLA_HEADS), valT(MLA_HEADS)],
        out_shape=[jax.ShapeDtypeStruct((b, MLA_HEADS, n_t, HEAD_PAD, tile), BF16),
                   jax.ShapeDtypeStruct((b, MLA_HEADS, s, HEAD_PAD), BF16),
                   jax.ShapeDtypeStruct((b, MLA_HEADS, n_t, V_ROWS, tile), BF16)],
        compiler_params=_cparams(2),
        name="proj_mla",
    )(x, p["g_mix"], p["wqaT"], p["gqa"], p["wqbT"], p["wkva"], p["gkva"], p["wkn"], p["wkr2"], p["wvT_m"],
      tabs["cT_m"], tabs["sT_m"], tabs["cP_m"], tabs["sP_m"])

    featT = lambda rows: pl.BlockSpec((1, rows, tile), lambda bi, i: (bi, 0, i))
    qc, gates = pl.pallas_call(
        _proj_misc_kernel,
        grid=grid,
        in_specs=[x_spec, _const_spec((1, D_MODEL)), _const_spec(p["wmiscT"].shape)],
        out_specs=[featT(D_MODEL), featT(N_BRANCH * D_MODEL)],
        out_shape=[jax.ShapeDtypeStruct((b, D_MODEL, s), BF16),
                   jax.ShapeDtypeStruct((b, N_BRANCH * D_MODEL, s), BF16)],
        compiler_params=_cparams(2),
        name="proj_misc",
    )(x, p["g_mix"], p["wmiscT"])

    n_mem = mem.shape[1]
    mk, mvT = pl.pallas_call(
        _mem_kv_kernel,
        grid=(b,),
        in_specs=[pl.BlockSpec((1, n_mem, D_MODEL), lambda bi: (bi, 0, 0)), _const_spec((1, D_MODEL)),
                  _const_spec(p["wmk"].shape), _const_spec(p["wmvT"].shape)],
        out_specs=[pl.BlockSpec((1, n_mem, D_MODEL), lambda bi: (bi, 0, 0)),
                   pl.BlockSpec((1, D_MODEL, n_mem), lambda bi: (bi, 0, 0))],
        out_shape=[jax.ShapeDtypeStruct((b, n_mem, D_MODEL), BF16), jax.ShapeDtypeStruct((b, D_MODEL, n_mem), BF16)],
        compiler_params=_cparams(1),
        name="mem_kv",
    )(mem, p["g_mem"], p["wmk"], p["wmvT"])

    oa = _attention(qg, kg, vg, GQA_HEADS // GQA_KV_HEADS)
    ob = _attention(qm, km, vm, 1)
    oc = pl.pallas_call(
        _mem_attn_kernel,
        grid=grid,
        in_specs=[featT(D_MODEL), pl.BlockSpec((1, n_mem, D_MODEL), lambda bi, i: (bi, 0, 0)),
                  pl.BlockSpec((1, D_MODEL, n_mem), lambda bi, i: (bi, 0, 0))],
        out_specs=featT(D_MODEL),
        out_shape=jax.ShapeDtypeStruct((b, D_MODEL, s), BF16),
        compiler_params=_cparams(2),
        name="mem_attention",
    )(qc, mk, mvT)

    return pl.pallas_call(
        _merge_kernel,
        grid=grid,
        in_specs=[headT(GQA_HEADS), headT(MLA_HEADS), featT(D_MODEL), featT(N_BRANCH * D_MODEL), x_spec,
                  _const_spec((D_MODEL, D_MODEL))],
        out_specs=x_spec,
        out_shape=jax.ShapeDtypeStruct(x.shape, F32),
        compiler_params=_cparams(2),
        name="merge_out",
    )(oa, ob, oc, gates, x, p["w_out"])


def _trunk(x, mem, layers, g_final):
    b, s, d = x.shape
    tabs = _rope_tables(s)
    for l, p in enumerate(layers):
        x = _mixer(x, mem, p, tabs)
        x = _moe(x.reshape(b * s, d), p["g_ffn"], p["wrT"], p["br"], p["wgu"], p["wd"], g_final,
                 final_norm=(l == len(layers) - 1)).reshape(b, s, d)
    return x


def kernel(x_prompt, x_sample, mem_prompt, mem_sample, norm_mix, norm_mem, w_in, gqa_q_norm, gqa_k_norm, mla_q_a_norm, mla_kv_a_norm, mla_w_qb, mla_w_kvb, mem_w_kv, w_out, norm_ffn, w_group, b_group, w_expert, b_expert, w_gate_up, w_down, norm_final):
    depth = w_in.shape[0]
    layers = [_layer_params(l, norm_mix, norm_mem, w_in, gqa_q_norm, gqa_k_norm, mla_q_a_norm, mla_kv_a_norm,
                            mla_w_qb, mla_w_kvb, mem_w_kv, w_out, norm_ffn, w_group, b_group, w_expert, b_expert,
                            w_gate_up, w_down) for l in range(depth)]
    g_final = norm_final[None, :]
    return (_trunk(x_prompt, mem_prompt, layers, g_final), _trunk(x_sample, mem_sample, layers, g_final))
```

```python
import functools
import math

import jax
import jax.numpy as jnp
from jax import lax
from jax.experimental import pallas as pl
from jax.experimental.pallas import tpu as pltpu

F32 = jnp.float32
BF16 = jnp.bfloat16

D_MODEL = 1024
EPS = 1e-6
GRID_W = 64
ROPE_THETA = 10000.0
GQA_HEADS, GQA_KV_HEADS, GQA_HD = 8, 2, 128
MLA_HEADS, MLA_Q_LORA, MLA_KV_LORA = 8, 384, 256
MLA_NOPE, MLA_ROPE, MLA_V = 64, 32, 128
MEM_HEADS, MEM_HD = 4, 256
N_BRANCH = 3
N_GROUPS, EXPERTS_PER_GROUP, D_EXPERT = 8, 8, 256
N_EXPERTS = N_GROUPS * EXPERTS_PER_GROUP
HEAD_PAD = 128
BF16_SUBLANES = 16
V_ROWS = HEAD_PAD + BF16_SUBLANES
LOG2E = 1.4426950408889634

TOK_TILE = 512
ATT_Q_COLS = 2048
ATT_CHUNKS_PER_BODY = 4
MOE_ROWS = 256
ROUTER_LANES = 128
ROW_SUBLANES = D_MODEL // 128
DMA_UNROLL = 8
VMEM_LIMIT = 56 * 1024 * 1024


def _cparams(n_axes, flags=None):
    return pltpu.CompilerParams(dimension_semantics=("arbitrary",) * n_axes, vmem_limit_bytes=VMEM_LIMIT,
                                flags=flags)


def _nt(a, b):
    return lax.dot_general(a, b, (((1,), (1,)), ((), ())), preferred_element_type=F32)


def _rms_rows(x, g):
    return x * lax.rsqrt(jnp.mean(x * x, axis=-1, keepdims=True) + EPS) * g


def _rms_cols(x, g):
    return x * lax.rsqrt(jnp.mean(x * x, axis=0, keepdims=True) + EPS) * g


def _const_spec(shape):
    return pl.BlockSpec(shape, lambda *_: (0,) * len(shape))


def _store_values(v_ref, vT):
    t = vT.shape[1]
    v_ref[0:HEAD_PAD, :] = vT.astype(BF16)
    ones_row = lax.broadcasted_iota(jnp.int32, (BF16_SUBLANES, t), 0) == 0
    v_ref[HEAD_PAD:V_ROWS, :] = jnp.where(ones_row, 1.0, 0.0).astype(BF16)


def _proj_gqa_kernel(x_ref, g_ref, wqT_ref, wk_ref, wvT_ref, gq_ref, gk_ref, cT_ref, sT_ref, cF_ref, sF_ref,
                     q_ref, k_ref, v_ref):
    h = _rms_rows(x_ref[0], g_ref[...]).astype(BF16)
    c, s, gq = cT_ref[...], sT_ref[...], gq_ref[...]
    half = GQA_HD // 2
    scale = GQA_HD ** -0.5 * LOG2E
    heads_per_dot = 4
    k = jnp.dot(h, wk_ref[...], preferred_element_type=F32)
    q_pieces = []
    for h0 in range(0, GQA_HEADS, heads_per_dot):
        q_pieces.append(_nt(wqT_ref[h0 * GQA_HD:(h0 + heads_per_dot) * GQA_HD, :], h))
        if h0 == 0:
            for j in range(GQA_KV_HEADS):
                kn = _rms_rows(k[:, j * GQA_HD:(j + 1) * GQA_HD], gk_ref[...])
                ko = kn * cF_ref[...] + pltpu.roll(kn, half, 1) * sF_ref[...]
                k_ref[0, j] = ko.astype(BF16)
    vT = _nt(wvT_ref[...], h)
    for piece, qT in enumerate(q_pieces):
        for hd in range(heads_per_dot):
            qn = _rms_cols(qT[hd * GQA_HD:(hd + 1) * GQA_HD], gq)
            x1, x2 = qn[:half], qn[half:]
            o = jnp.concatenate([x1 * c - x2 * s, x2 * c + x1 * s], axis=0) * scale
            q_ref[0, piece * heads_per_dot + hd, 0] = o.astype(BF16)
    for j in range(GQA_KV_HEADS):
        _store_values(v_ref.at[0, j, 0], vT[j * GQA_HD:(j + 1) * GQA_HD])


def _proj_mla_kernel(x_ref, g_ref, wqaT_ref, gqa_ref, wqbT_ref, wkva_ref, gkva_ref, wkn_ref, wkr_ref, wvT_ref,
                     cT_ref, sT_ref, cP_ref, sP_ref, q_ref, k_ref, v_ref):
    h = _rms_rows(x_ref[0], g_ref[...]).astype(BF16)
    qa_raw = _nt(wqaT_ref[...], h)
    kva = jnp.dot(h, wkva_ref[...], preferred_element_type=F32)
    kr2 = jnp.dot(h, wkr_ref[...], preferred_element_type=F32)
    qa = _rms_cols(qa_raw, gqa_ref[...]).astype(BF16)
    qT = jnp.dot(wqbT_ref[...], qa, preferred_element_type=F32)
    kvn = _rms_rows(kva, gkva_ref[...]).astype(BF16)
    kn = jnp.dot(kvn, wkn_ref[...], preferred_element_type=F32)
    kr = kr2[:, :HEAD_PAD] * cP_ref[...] + kr2[:, HEAD_PAD:] * sP_ref[...]
    c, s = cT_ref[...], sT_ref[...]
    hr = MLA_ROPE // 2
    scale = (MLA_NOPE + MLA_ROPE) ** -0.5 * LOG2E
    for hd in range(MLA_HEADS):
        q = qT[hd * HEAD_PAD:(hd + 1) * HEAD_PAD]
        x1, x2 = q[MLA_NOPE:MLA_NOPE + hr], q[MLA_NOPE + hr:MLA_NOPE + 2 * hr]
        o = jnp.concatenate([q[:MLA_NOPE], x1 * c - x2 * s, x2 * c + x1 * s, q[MLA_NOPE + 2 * hr:]], axis=0) * scale
        q_ref[0, hd, 0] = o.astype(BF16)
    vT = _nt(wvT_ref[...], kvn)
    for hd in range(MLA_HEADS):
        k_ref[0, hd] = (kn[:, hd * HEAD_PAD:(hd + 1) * HEAD_PAD] + kr).astype(BF16)
    for hd in range(MLA_HEADS):
        _store_values(v_ref.at[0, hd, 0], vT[hd * MLA_V:(hd + 1) * MLA_V])


def _proj_misc_kernel(x_ref, g_ref, wT_ref, qc_ref, gate_ref):
    h = _rms_rows(x_ref[0], g_ref[...]).astype(BF16)
    scale = MEM_HD ** -0.5 * LOG2E
    rows = 512
    n_q = D_MODEL // rows
    for ci in range((D_MODEL + N_BRANCH * D_MODEL) // rows):
        y = _nt(wT_ref[ci * rows:(ci + 1) * rows, :], h)
        if ci < n_q:
            qc_ref[0, ci * rows:(ci + 1) * rows, :] = (y * scale).astype(BF16)
        else:
            r0 = (ci - n_q) * rows
            gate_ref[0, r0:r0 + rows, :] = jax.nn.sigmoid(y).astype(BF16)


def _mem_kv_kernel(mem_ref, g_ref, wk_ref, wvT_ref, mk_ref, mvT_ref):
    h = _rms_rows(mem_ref[0], g_ref[...]).astype(BF16)
    mk_ref[0] = jnp.dot(h, wk_ref[...], preferred_element_type=F32).astype(BF16)
    mvT_ref[0] = _nt(wvT_ref[...], h).astype(BF16)


def _attn_kernel(q_ref, k_ref, v_ref, o_ref, m_sc, acc_sc, s_sc, mc_sc, *, chunks_per_head, n_kv):
    chunk = TOK_TILE
    n_sub = m_sc.shape[0]

    m_sc[...] = jnp.full(m_sc.shape, -jnp.inf, F32)
    acc_sc[...] = jnp.zeros(acc_sc.shape, F32)

    def scores(c, u, slot):
        k = k_ref[0, 0, pl.ds(pl.multiple_of(c * chunk, chunk), chunk), :]
        s = jnp.dot(k, q_ref[0, u // chunks_per_head, u % chunks_per_head],
                    preferred_element_type=F32)
        s_sc[slot, u] = s
        mc_sc[slot, u] = jnp.max(s, axis=0, keepdims=True)

    def softmax_values(c, u, slot):
        m_prev = m_sc[u]
        m_cur = jnp.maximum(m_prev, mc_sc[slot, u])
        alpha = jnp.exp2(m_prev - m_cur)
        p = jnp.exp2(s_sc[slot, u] - m_cur)
        acc_sc[u] = alpha * acc_sc[u] + jnp.dot(v_ref[0, 0, c], p.astype(BF16), preferred_element_type=F32)
        m_sc[u] = m_cur

    def stage(c_scores, c_values):
        for u in range(n_sub):
            if c_scores is not None:
                scores(c_scores, u, c_scores % 2)
            if c_values is not None:
                softmax_values(c_values, u, c_values % 2)

    per_body = ATT_CHUNKS_PER_BODY if (n_kv - 1) // ATT_CHUNKS_PER_BODY >= 2 else 2
    n_bodies = (n_kv - 1) // per_body
    stage(0, None)

    def body(i, carry):
        for t in range(per_body):
            for u in range(n_sub):
                scores(per_body * i + t + 1, u, (t + 1) % 2)
                softmax_values(per_body * i + t, u, t % 2)
        return carry

    lax.fori_loop(0, n_bodies, body, 0)
    for c in range(per_body * n_bodies, n_kv - 1):
        stage(c + 1, c)
    stage(None, n_kv - 1)

    for u in range(n_sub):
        o_ref[0, u // chunks_per_head, u % chunks_per_head] = (
            acc_sc[u, 0:HEAD_PAD] / acc_sc[u, HEAD_PAD:HEAD_PAD + 1]).astype(BF16)


def _attention(qT, k, vT, group):
    b, h, n_kv, hd, chunk = qT.shape
    hkv = k.shape[1]
    s = n_kv * chunk
    assert h == hkv * group and chunk == TOK_TILE
    chunks_per_head = min(ATT_Q_COLS // group, s) // chunk
    n_sub = group * chunks_per_head
    kern = functools.partial(_attn_kernel, chunks_per_head=chunks_per_head, n_kv=n_kv)
    q_spec = pl.BlockSpec((1, group, chunks_per_head, hd, chunk), lambda bi, j, qi: (bi, j, qi, 0, 0))
    return pl.pallas_call(
        kern,
        grid=(b, hkv, n_kv // chunks_per_head),
        in_specs=[
            q_spec,
            pl.BlockSpec((1, 1, s, hd), lambda bi, j, qi: (bi, j, 0, 0)),
            pl.BlockSpec((1, 1, n_kv, V_ROWS, chunk), lambda bi, j, qi: (bi, j, 0, 0, 0)),
        ],
        out_specs=q_spec,
        out_shape=jax.ShapeDtypeStruct(qT.shape, BF16),
        scratch_shapes=[
            pltpu.VMEM((n_sub, 1, chunk), F32),
            pltpu.VMEM((n_sub, V_ROWS, chunk), F32),
            pltpu.VMEM((2, n_sub, chunk, chunk), F32),
            pltpu.VMEM((2, n_sub, 1, chunk), F32),
        ],
        compiler_params=_cparams(3),
        name="attention",
    )(qT, k, vT)


def _mem_attn_kernel(qc_ref, mk_ref, mvT_ref, o_ref):
    rows = [slice(hd * MEM_HD, (hd + 1) * MEM_HD) for hd in range(MEM_HEADS)]
    score = lambda r: jnp.dot(mk_ref[0, :, r], qc_ref[0, r, :], preferred_element_type=F32)
    s_next = score(rows[0])
    for hd in range(MEM_HEADS):
        r, s = rows[hd], s_next
        if hd + 1 < MEM_HEADS:
            s_next = score(rows[hd + 1])
        p = jnp.exp2(s - jnp.max(s, axis=0, keepdims=True))
        l = jnp.sum(p, axis=0, keepdims=True)
        o = jnp.dot(mvT_ref[0, r, :], p.astype(BF16), preferred_element_type=F32)
        o_ref[0, r, :] = (o / l).astype(BF16)


def _merge_kernel(oa_ref, ob_ref, oc_ref, gate_ref, x_ref, w_ref, y_ref):
    t = x_ref.shape[1]
    oa = oa_ref[0].reshape(D_MODEL, t)
    ob = ob_ref[0].reshape(D_MODEL, t)
    merged = (gate_ref[0, 0:D_MODEL, :].astype(F32) * oa.astype(F32)
              + gate_ref[0, D_MODEL:2 * D_MODEL, :].astype(F32) * ob.astype(F32)
              + gate_ref[0, 2 * D_MODEL:, :].astype(F32) * oc_ref[0].astype(F32)).astype(BF16)
    out = lax.dot_general(merged, w_ref[...], (((0,), (0,)), ((), ())), preferred_element_type=F32)
    y_ref[0] = x_ref[0] + out


def _first_index_of_max(v, vmax, iota, n):
    return jnp.min(jnp.where(v == vmax, iota, n), axis=0, keepdims=True)


def _router_kernel(x_ref, g_ref, wT_ref, b_ref, ri_ref, rw_ref, cnt_ref, carry_sc, before_sc):
    step = pl.program_id(0)
    t = x_ref.shape[0]

    @pl.when(step == 0)
    def _():
        carry_sc[...] = jnp.zeros(carry_sc.shape, F32)
        before = lax.broadcasted_iota(jnp.int32, (t, t), 0) < lax.broadcasted_iota(jnp.int32, (t, t), 1)
        before_sc[...] = jnp.where(before, 1.0, 0.0).astype(BF16)

    h = _rms_rows(x_ref[...], g_ref[...])
    h_hi = h.astype(BF16)
    h_lo = (h - h_hi.astype(F32)).astype(BF16)
    w_hi, w_lo = wT_ref[0], wT_ref[1]
    logits = _nt(w_hi, h_hi) + _nt(w_hi, h_lo) + _nt(w_lo, h_hi) + b_ref[...]
    iota_g = lax.broadcasted_iota(jnp.int32, (N_GROUPS, t), 0)
    gl = logits[0:N_GROUPS]
    gmax = jnp.max(gl, axis=0, keepdims=True)
    gidx = _first_index_of_max(gl, gmax, iota_g, N_GROUPS)
    gw = 1.0 / jnp.sum(jnp.exp(gl - gmax), axis=0, keepdims=True)
    sel = jnp.zeros((EXPERTS_PER_GROUP, t), F32)
    for g in range(N_GROUPS):
        r0 = N_GROUPS + g * EXPERTS_PER_GROUP
        sel = sel + jnp.where(gidx == g, logits[r0:r0 + EXPERTS_PER_GROUP], 0.0)
    v1 = jnp.max(sel, axis=0, keepdims=True)
    i1 = _first_index_of_max(sel, v1, iota_g, EXPERTS_PER_GROUP)
    sel2 = jnp.where(iota_g == i1, -jnp.inf, sel)
    v2 = jnp.max(sel2, axis=0, keepdims=True)
    i2 = _first_index_of_max(sel2, v2, iota_g, EXPERTS_PER_GROUP)
    e = jnp.exp(v2 - v1)
    w1 = gw / (1.0 + e)
    w2 = gw * e / (1.0 + e)
    e1 = gidx * EXPERTS_PER_GROUP + i1
    e2 = gidx * EXPERTS_PER_GROUP + i2

    iota_e = lax.broadcasted_iota(jnp.int32, (N_EXPERTS, t), 0)
    hit1 = iota_e == e1
    hit2 = iota_e == e2
    hits = jnp.where(hit1 | hit2, 1.0, 0.0)
    prefix = jnp.dot(hits.astype(BF16), before_sc[...],
                     preferred_element_type=F32) + carry_sc[...]
    r1 = jnp.sum(jnp.where(hit1, prefix, 0.0), axis=0, keepdims=True)
    r2 = jnp.sum(jnp.where(hit2, prefix, 0.0), axis=0, keepdims=True)
    carry_sc[...] = carry_sc[...] + jnp.sum(hits, axis=1, keepdims=True)

    ri_ref[...] = jnp.zeros(ri_ref.shape, jnp.int32)
    ri_ref[0:1, :] = e1
    ri_ref[1:2, :] = e2
    ri_ref[2:3, :] = r1.astype(jnp.int32)
    ri_ref[3:4, :] = r2.astype(jnp.int32)
    rw_ref[...] = jnp.zeros(rw_ref.shape, F32)
    rw_ref[0:1, :] = w1
    rw_ref[1:2, :] = w2
    cnt_ref[...] = jnp.broadcast_to(carry_sc[...], cnt_ref.shape)


def _dest_kernel(cnt_ref, ri_ref, dest_ref, blk_ref, nused_ref, padinfo_ref, *, n_blocks_padded):
    cnt = cnt_ref[...]
    pad = jnp.floor((cnt + (MOE_ROWS - 1)) * (1.0 / MOE_ROWS)) * MOE_ROWS
    lower = jnp.where(lax.broadcasted_iota(jnp.int32, (N_EXPERTS, N_EXPERTS), 1)
                      <= lax.broadcasted_iota(jnp.int32, (N_EXPERTS, N_EXPERTS), 0), 1.0, 0.0)
    pad_end = jnp.dot(lower, pad, precision=lax.Precision.HIGHEST, preferred_element_type=F32)
    pad_start = (pad_end - pad)[:, 0:1]
    t = ri_ref.shape[1]
    iota_e = lax.broadcasted_iota(jnp.int32, (N_EXPERTS, t), 0)
    dest_ref[...] = jnp.zeros(dest_ref.shape, jnp.int32)
    for kk in range(2):
        start = jnp.sum(jnp.where(iota_e == ri_ref[kk:kk + 1, :], pad_start, 0.0), axis=0, keepdims=True)
        dest_ref[kk:kk + 1, :] = (start.astype(jnp.int32) + ri_ref[2 + kk:3 + kk, :]) * ROW_SUBLANES
    blk_start = (lax.broadcasted_iota(jnp.int32, (N_EXPERTS, n_blocks_padded), 1) * MOE_ROWS).astype(F32)
    owner = jnp.sum(jnp.where(pad_end[:, 0:1] <= blk_start, 1.0, 0.0), axis=0, keepdims=True)
    blk_ref[...] = jnp.broadcast_to(jnp.minimum(owner, N_EXPERTS - 1.0).astype(jnp.int32), blk_ref.shape)
    nused_ref[...] = jnp.broadcast_to((pad_end[N_EXPERTS - 1:N_EXPERTS, :] * (1.0 / MOE_ROWS)).astype(jnp.int32),
                                      nused_ref.shape)
    diag = (lax.broadcasted_iota(jnp.int32, (N_EXPERTS, 128), 0) == lax.broadcasted_iota(jnp.int32, (N_EXPERTS, 128), 1))
    first_pad = jnp.sum(jnp.where(diag, (pad_end - pad + cnt) * ROW_SUBLANES, 0.0), axis=0, keepdims=True)
    n_pad = jnp.sum(jnp.where(diag, pad - cnt, 0.0), axis=0, keepdims=True)
    padinfo_ref[...] = jnp.zeros(padinfo_ref.shape, jnp.int32)
    padinfo_ref[0:1, :] = first_pad.astype(jnp.int32)
    padinfo_ref[1:2, :] = n_pad.astype(jnp.int32)
    padinfo_ref[2:3, :] = (pad_end[N_EXPERTS - 1:N_EXPERTS, :] * (1.0 / MOE_ROWS)).astype(jnp.int32)


def _to_row_tiles(ref, x):
    n = x.shape[0]
    for c in range(ROW_SUBLANES):
        ref[pl.ds(c, n, stride=ROW_SUBLANES), :] = x[:, c * 128:(c + 1) * 128]


def _from_row_tiles(ref, n):
    return jnp.concatenate([ref[pl.ds(c, n, stride=ROW_SUBLANES), :] for c in range(ROW_SUBLANES)], axis=1)


def _tile_copy(src_ref, src_row8, dst_ref, dst_row8, sem):
    return pltpu.make_async_copy(src_ref.at[pl.ds(pl.multiple_of(src_row8, ROW_SUBLANES), ROW_SUBLANES)],
                                 dst_ref.at[pl.ds(pl.multiple_of(dst_row8, ROW_SUBLANES), ROW_SUBLANES)], sem)


def _dispatch_kernel(dest_ref, padinfo_ref, x_ref, g_ref, xs_ref, h_sc, zero_sc, sem, blk_sem, *, tile,
                     experts_per_step, n_blocks):
    step = pl.program_id(0)
    _to_row_tiles(h_sc, _rms_rows(x_ref[...], g_ref[...]))
    zero_sc[...] = jnp.zeros(zero_sc.shape, F32)

    def block_copy(b):
        rows = MOE_ROWS * ROW_SUBLANES
        return pltpu.make_async_copy(zero_sc, xs_ref.at[pl.ds(pl.multiple_of(b * rows, rows), rows)], blk_sem)

    for j in range(experts_per_step):
        b = padinfo_ref[2, 0] + step * experts_per_step + j

        @pl.when(b < n_blocks)
        def _():
            block_copy(b).start()

    def issue(t, carry):
        _tile_copy(h_sc, t * ROW_SUBLANES, xs_ref, dest_ref[0, 0, t], sem).start()
        _tile_copy(h_sc, t * ROW_SUBLANES, xs_ref, dest_ref[0, 0, tile + t], sem).start()
        return carry

    def drain(t, carry):
        _tile_copy(h_sc, 0, xs_ref, 0, sem).wait()
        _tile_copy(h_sc, 0, xs_ref, 0, sem).wait()
        return carry

    lax.fori_loop(0, tile, issue, 0, unroll=DMA_UNROLL)

    n_zero = jnp.int32(0)
    for j in range(experts_per_step):
        e = step * experts_per_step + j
        e_c = jnp.minimum(e, N_EXPERTS - 1)
        first = padinfo_ref[0, e_c]
        count = jnp.where(e < N_EXPERTS, padinfo_ref[1, e_c], 0)

        def zero_row(r, carry, first=first):
            _tile_copy(zero_sc, 0, xs_ref, first + r * ROW_SUBLANES, sem).start()
            return carry

        lax.fori_loop(0, count, zero_row, 0)
        n_zero = n_zero + count

    lax.fori_loop(0, tile, drain, 0, unroll=DMA_UNROLL)
    lax.fori_loop(0, n_zero, lambda r, carry: (_tile_copy(zero_sc, 0, xs_ref, 0, sem).wait(), carry)[1], 0)
    for j in range(experts_per_step):
        b = padinfo_ref[2, 0] + step * experts_per_step + j

        @pl.when(b < n_blocks)
        def _():
            block_copy(b).wait()


def _expert_kernel(blk_ref, nused_ref, xs_ref, wgu_ref, wd_ref, ys_ref):
    b = pl.program_id(0)

    @pl.when(b < nused_ref[0])
    def _():
        x = _from_row_tiles(xs_ref, MOE_ROWS).astype(BF16)
        gu = jnp.dot(x, wgu_ref[0], preferred_element_type=F32)
        a, u = gu[:, :D_EXPERT], gu[:, D_EXPERT:]
        hidden = (a * jax.nn.sigmoid(a) * u).astype(BF16)
        _to_row_tiles(ys_ref, jnp.dot(hidden, wd_ref[0], preferred_element_type=F32))

    @pl.when(b >= nused_ref[0])
    def _():
        ys_ref[...] = jnp.zeros(ys_ref.shape, F32)


def _combine_kernel(dest_ref, dest_next_ref, ys_ref, x_ref, w_ref, g_ref, y_ref, buf, sem, *, tile, final_norm):
    step = pl.program_id(0)
    slot = lax.rem(step, 2)

    def issue(d_ref, slot_):
        def body(t, carry):
            _tile_copy(ys_ref, d_ref[0, 0, t], buf.at[slot_, 0], t * ROW_SUBLANES, sem.at[slot_]).start()
            _tile_copy(ys_ref, d_ref[0, 0, tile + t], buf.at[slot_, 1], t * ROW_SUBLANES, sem.at[slot_]).start()
            return carry
        lax.fori_loop(0, tile, body, 0, unroll=DMA_UNROLL)

    @pl.when(step == 0)
    def _():
        issue(dest_ref, 0)

    @pl.when(step + 1 < pl.num_programs(0))
    def _():
        issue(dest_next_ref, 1 - slot)

    def drain(t, carry):
        _tile_copy(ys_ref, 0, buf.at[slot, 0], 0, sem.at[slot]).wait()
        _tile_copy(ys_ref, 0, buf.at[slot, 1], 0, sem.at[slot]).wait()
        return carry

    lax.fori_loop(0, tile, drain, 0, unroll=DMA_UNROLL)
    w = w_ref[...]
    y = x_ref[...] + (w[:, 0:1] * _from_row_tiles(buf.at[slot, 0], tile)
                      + w[:, 1:2] * _from_row_tiles(buf.at[slot, 1], tile))
    if final_norm:
        y = _rms_rows(y, g_ref[...])
    y_ref[...] = y


def _moe(x, g_ffn, wrT, br, wgu, wd, g_final, final_norm):
    n_tok = x.shape[0]
    tile = min(TOK_TILE, n_tok)
    n_tiles = n_tok // tile
    n_blocks = (2 * n_tok + MOE_ROWS - 1) // MOE_ROWS + N_EXPERTS
    n_blocks_padded = (n_blocks + 127) // 128 * 128
    n_rows = n_blocks * MOE_ROWS

    ri, rw, cnt = pl.pallas_call(
        _router_kernel,
        grid=(n_tiles,),
        in_specs=[
            pl.BlockSpec((tile, D_MODEL), lambda i: (i, 0)),
            _const_spec((1, D_MODEL)),
            _const_spec((2, ROUTER_LANES, D_MODEL)),
            _const_spec((ROUTER_LANES, 1)),
        ],
        out_specs=[
            pl.BlockSpec((8, tile), lambda i: (0, i)),
            pl.BlockSpec((8, tile), lambda i: (0, i)),
            _const_spec((N_EXPERTS, 128)),
        ],
        out_shape=[
            jax.ShapeDtypeStruct((8, n_tok), jnp.int32),
            jax.ShapeDtypeStruct((8, n_tok), F32),
            jax.ShapeDtypeStruct((N_EXPERTS, 128), F32),
        ],
        scratch_shapes=[pltpu.VMEM((N_EXPERTS, 1), F32), pltpu.VMEM((tile, tile), BF16)],
        compiler_params=_cparams(1),
        name="moe_router",
    )(x, g_ffn, wrT, br)

    dest, blk, nused, padinfo = pl.pallas_call(
        functools.partial(_dest_kernel, n_blocks_padded=n_blocks_padded),
        grid=(n_tiles,),
        in_specs=[_const_spec((N_EXPERTS, 128)), pl.BlockSpec((8, tile), lambda i: (0, i))],
        out_specs=[
            pl.BlockSpec((8, tile), lambda i: (0, i)),
            _const_spec((8, n_blocks_padded)),
            _const_spec((8, 128)),
            _const_spec((8, 128)),
        ],
        out_shape=[
            jax.ShapeDtypeStruct((8, n_tok), jnp.int32),
            jax.ShapeDtypeStruct((8, n_blocks_padded), jnp.int32),
            jax.ShapeDtypeStruct((8, 128), jnp.int32),
            jax.ShapeDtypeStruct((8, 128), jnp.int32),
        ],
        compiler_params=_cparams(1),
        name="moe_dest",
    )(cnt, ri)

    dest_tiles = dest[0:2].reshape(2, n_tiles, tile).transpose(1, 0, 2).reshape(n_tiles, 1, 2 * tile)
    dest_spec = pl.BlockSpec((1, 1, 2 * tile), lambda i: (i, 0, 0), memory_space=pltpu.SMEM)
    any_spec = pl.BlockSpec(memory_space=pl.ANY)

    xs = pl.pallas_call(
        functools.partial(_dispatch_kernel, tile=tile, experts_per_step=pl.cdiv(N_EXPERTS, n_tiles),
                          n_blocks=n_blocks),
        grid=(n_tiles,),
        in_specs=[dest_spec, pl.BlockSpec((8, 128), lambda i: (0, 0), memory_space=pltpu.SMEM),
                  pl.BlockSpec((tile, D_MODEL), lambda i: (i, 0)), _const_spec((1, D_MODEL))],
        out_specs=any_spec,
        out_shape=jax.ShapeDtypeStruct((n_rows * ROW_SUBLANES, 128), F32),
        scratch_shapes=[pltpu.VMEM((tile * ROW_SUBLANES, 128), F32),
                        pltpu.VMEM((MOE_ROWS * ROW_SUBLANES, 128), F32),
                        pltpu.SemaphoreType.DMA(()), pltpu.SemaphoreType.DMA(())],
        compiler_params=_cparams(1),
        name="moe_dispatch",
    )(dest_tiles, padinfo, x, g_ffn)

    blk_rows = MOE_ROWS * ROW_SUBLANES
    ys = pl.pallas_call(
        _expert_kernel,
        grid_spec=pltpu.PrefetchScalarGridSpec(
            num_scalar_prefetch=2,
            grid=(n_blocks,),
            in_specs=[
                pl.BlockSpec((blk_rows, 128), lambda b, blk, nu: (b, 0)),
                pl.BlockSpec((1, D_MODEL, 2 * D_EXPERT), lambda b, blk, nu: (blk[b], 0, 0)),
                pl.BlockSpec((1, D_EXPERT, D_MODEL), lambda b, blk, nu: (blk[b], 0, 0)),
            ],
            out_specs=pl.BlockSpec((blk_rows, 128), lambda b, blk, nu: (b, 0)),
        ),
        out_shape=jax.ShapeDtypeStruct((n_rows * ROW_SUBLANES, 128), F32),
        compiler_params=_cparams(1),
        name="moe_experts",
    )(blk[0, :n_blocks], nused[0, :1], xs, wgu, wd)

    ctile = min(256, n_tok)
    c_tiles = n_tok // ctile
    dest_c = dest[0:2].reshape(2, c_tiles, ctile).transpose(1, 0, 2).reshape(c_tiles, 1, 2 * ctile)
    return pl.pallas_call(
        functools.partial(_combine_kernel, tile=ctile, final_norm=final_norm),
        grid=(c_tiles,),
        in_specs=[
            pl.BlockSpec((1, 1, 2 * ctile), lambda i: (i, 0, 0), memory_space=pltpu.SMEM),
            pl.BlockSpec((1, 1, 2 * ctile), lambda i: (jnp.minimum(i + 1, c_tiles - 1), 0, 0),
                         memory_space=pltpu.SMEM),
            any_spec,
            pl.BlockSpec((ctile, D_MODEL), lambda i: (i, 0)),
            pl.BlockSpec((ctile, 2), lambda i: (i, 0)),
            _const_spec((1, D_MODEL)),
        ],
        out_specs=pl.BlockSpec((ctile, D_MODEL), lambda i: (i, 0)),
        out_shape=jax.ShapeDtypeStruct((n_tok, D_MODEL), F32),
        scratch_shapes=[pltpu.VMEM((2, 2, ctile * ROW_SUBLANES, 128), F32), pltpu.SemaphoreType.DMA((2,))],
        compiler_params=_cparams(1),
        name="moe_combine",
    )(dest_c, dest_c, ys, x, rw[0:2].T, g_final)


def _rope_tables(n_tokens):
    rows = n_tokens // GRID_W
    row_idx = jnp.repeat(jnp.arange(rows, dtype=F32), GRID_W)
    col_idx = jnp.tile(jnp.arange(GRID_W, dtype=F32), rows)

    def angles(rot_dim):
        n_freq = rot_dim // 4
        inv_freq = ROPE_THETA ** (-jnp.arange(n_freq, dtype=F32) / n_freq)
        return jnp.concatenate([row_idx[:, None] * inv_freq, col_idx[:, None] * inv_freq], axis=-1)

    ag, am = angles(GQA_HD), angles(MLA_ROPE)
    cg, sg, cm, sm = jnp.cos(ag), jnp.sin(ag), jnp.cos(am), jnp.sin(am)
    hr = MLA_ROPE // 2
    zl = jnp.zeros((n_tokens, MLA_NOPE), F32)
    zr = jnp.zeros((n_tokens, HEAD_PAD - MLA_NOPE - MLA_ROPE), F32)
    del hr
    return dict(
        cT_g=cg.T, sT_g=sg.T,
        cF_g=jnp.concatenate([cg, cg], axis=1), sF_g=jnp.concatenate([-sg, sg], axis=1),
        cT_m=cm.T, sT_m=sm.T,
        cP_m=jnp.concatenate([zl, cm, cm, zr], axis=1), sP_m=jnp.concatenate([zl, -sm, sm, zr], axis=1),
    )


def _hi_lo(w):
    hi = w.astype(BF16)
    return jnp.stack([hi, (w - hi.astype(F32)).astype(BF16)])


def _layer_params(l, norm_mix, norm_mem, w_in, gqa_q_norm, gqa_k_norm, mla_q_a_norm, mla_kv_a_norm, mla_w_qb,
                  mla_w_kvb, mem_w_kv, w_out, norm_ffn, w_group, b_group, w_expert, b_expert, w_gate_up, w_down):
    splits = []
    off = 0
    for width in (GQA_HEADS * GQA_HD, GQA_KV_HEADS * GQA_HD, GQA_KV_HEADS * GQA_HD, MLA_Q_LORA, MLA_KV_LORA,
                  MLA_ROPE, MEM_HEADS * MEM_HD, N_BRANCH * D_MODEL):
        splits.append(w_in[l][:, off:off + width])
        off += width
    wq_g, wk_g, wv_g, wq_a, wkv_a, wk_r, wq_c, w_gate = splits
    hr = MLA_ROPE // 2
    pad_l = jnp.zeros((D_MODEL, MLA_NOPE), F32)
    pad_r = jnp.zeros((D_MODEL, HEAD_PAD - MLA_NOPE - MLA_ROPE), F32)
    wk_r_swapped = jnp.concatenate([wk_r[:, hr:], wk_r[:, :hr]], axis=1)
    wkr2 = jnp.concatenate([pad_l, wk_r, pad_r, pad_l, wk_r_swapped, pad_r], axis=1)
    qk = MLA_NOPE + MLA_ROPE
    wqb = jnp.pad(mla_w_qb[l].reshape(MLA_Q_LORA, MLA_HEADS, qk), ((0, 0), (0, 0), (0, HEAD_PAD - qk)))
    wkvb = mla_w_kvb[l].reshape(MLA_KV_LORA, MLA_HEADS, MLA_NOPE + MLA_V)
    wkn = jnp.pad(wkvb[:, :, :MLA_NOPE], ((0, 0), (0, 0), (0, HEAD_PAD - MLA_NOPE)))
    wv_m = wkvb[:, :, MLA_NOPE:]
    n_router = N_GROUPS + N_EXPERTS
    return dict(
        g_mix=norm_mix[l][None, :], g_mem=norm_mem[l][None, :], g_ffn=norm_ffn[l][None, :],
        wqT_g=wq_g.T.astype(BF16), wk_g=wk_g.astype(BF16), wvT_g=wv_g.T.astype(BF16),
        gq=gqa_q_norm[l][:, None], gk=gqa_k_norm[l][None, :],
        wqaT=wq_a.T.astype(BF16), gqa=mla_q_a_norm[l][:, None],
        wqbT=wqb.reshape(MLA_Q_LORA, MLA_HEADS * HEAD_PAD).T.astype(BF16),
        wkva=wkv_a.astype(BF16), gkva=mla_kv_a_norm[l][None, :],
        wkn=wkn.reshape(MLA_KV_LORA, MLA_HEADS * HEAD_PAD).astype(BF16), wkr2=wkr2.astype(BF16),
        wvT_m=wv_m.reshape(MLA_KV_LORA, MLA_HEADS * MLA_V).T.astype(BF16),
        wmiscT=jnp.concatenate([wq_c, w_gate], axis=1).T.astype(BF16),
        wmk=mem_w_kv[l][:, :MEM_HEADS * MEM_HD].astype(BF16),
        wmvT=mem_w_kv[l][:, MEM_HEADS * MEM_HD:].T.astype(BF16),
        w_out=w_out[l].astype(BF16),
        wrT=_hi_lo(jnp.concatenate([w_group[l].T, w_expert[l].T,
                                    jnp.zeros((ROUTER_LANES - n_router, D_MODEL), F32)], axis=0)),
        br=jnp.concatenate([b_group[l], b_expert[l], jnp.zeros((ROUTER_LANES - n_router,), F32)])[:, None],
        wgu=w_gate_up[l].astype(BF16), wd=w_down[l].astype(BF16),
    )


def _mixer(x, mem, p, tabs):
    b, s, _ = x.shape
    tile = min(TOK_TILE, s)
    n_t = s // tile
    grid = (b, n_t)
    x_spec = pl.BlockSpec((1, tile, D_MODEL), lambda bi, i: (bi, i, 0))
    colT = lambda rows: pl.BlockSpec((rows, tile), lambda bi, i: (0, i))
    rowF = pl.BlockSpec((tile, HEAD_PAD), lambda bi, i: (i, 0))
    headT = lambda heads: pl.BlockSpec((1, heads, 1, HEAD_PAD, tile), lambda bi, i: (bi, 0, i, 0, 0))
    keyF = lambda heads: pl.BlockSpec((1, heads, tile, HEAD_PAD), lambda bi, i: (bi, 0, i, 0))
    valT = lambda heads: pl.BlockSpec((1, heads, 1, V_ROWS, tile), lambda bi, i: (bi, 0, i, 0, 0))

    qg, kg, vg = pl.pallas_call(
        _proj_gqa_kernel,
        grid=grid,
        in_specs=[x_spec, _const_spec((1, D_MODEL)), _const_spec(p["wqT_g"].shape), _const_spec(p["wk_g"].shape),
                  _const_spec(p["wvT_g"].shape), _const_spec((GQA_HD, 1)), _const_spec((1, GQA_HD)),
                  colT(GQA_HD // 2), colT(GQA_HD // 2), rowF, rowF],
        out_specs=[headT(GQA_HEADS), keyF(GQA_KV_HEADS), valT(GQA_KV_HEADS)],
        out_shape=[jax.ShapeDtypeStruct((b, GQA_HEADS, n_t, HEAD_PAD, tile), BF16),
                   jax.ShapeDtypeStruct((b, GQA_KV_HEADS, s, HEAD_PAD), BF16),
                   jax.ShapeDtypeStruct((b, GQA_KV_HEADS, n_t, V_ROWS, tile), BF16)],
        compiler_params=_cparams(2),
        name="proj_gqa",
    )(x, p["g_mix"], p["wqT_g"], p["wk_g"], p["wvT_g"], p["gq"], p["gk"],
      tabs["cT_g"], tabs["sT_g"], tabs["cF_g"], tabs["sF_g"])

    qm, km, vm = pl.pallas_call(
        _proj_mla_kernel,
        grid=grid,
        in_specs=[x_spec, _const_spec((1, D_MODEL)), _const_spec(p["wqaT"].shape), _const_spec((MLA_Q_LORA, 1)),
                  _const_spec(p["wqbT"].shape), _const_spec(p["wkva"].shape), _const_spec((1, MLA_KV_LORA)),
                  _const_spec(p["wkn"].shape), _const_spec(p["wkr2"].shape), _const_spec(p["wvT_m"].shape),
                  colT(MLA_ROPE // 2), colT(MLA_ROPE // 2), rowF, rowF],
        out_specs=[headT(MLA_HEADS), keyF(MLA_HEADS), valT(MLA_HEADS)],
        out_shape=[jax.ShapeDtypeStruct((b, MLA_HEADS, n_t, HEAD_PAD, tile), BF16),
                   jax.ShapeDtypeStruct((b, MLA_HEADS, s, HEAD_PAD), BF16),
                   jax.ShapeDtypeStruct((b, MLA_HEADS, n_t, V_ROWS, tile), BF16)],
        compiler_params=_cparams(2),
        name="proj_mla",
    )(x, p["g_mix"], p["wqaT"], p["gqa"], p["wqbT"], p["wkva"], p["gkva"], p["wkn"], p["wkr2"], p["wvT_m"],
      tabs["cT_m"], tabs["sT_m"], tabs["cP_m"], tabs["sP_m"])

    featT = lambda rows: pl.BlockSpec((1, rows, tile), lambda bi, i: (bi, 0, i))
    qc, gates = pl.pallas_call(
        _proj_misc_kernel,
        grid=grid,
        in_specs=[x_spec, _const_spec((1, D_MODEL)), _const_spec(p["wmiscT"].shape)],
        out_specs=[featT(D_MODEL), featT(N_BRANCH * D_MODEL)],
        out_shape=[jax.ShapeDtypeStruct((b, D_MODEL, s), BF16),
                   jax.ShapeDtypeStruct((b, N_BRANCH * D_MODEL, s), BF16)],
        compiler_params=_cparams(2),
        name="proj_misc",
    )(x, p["g_mix"], p["wmiscT"])

    n_mem = mem.shape[1]
    mk, mvT = pl.pallas_call(
        _mem_kv_kernel,
        grid=(b,),
        in_specs=[pl.BlockSpec((1, n_mem, D_MODEL), lambda bi: (bi, 0, 0)), _const_spec((1, D_MODEL)),
                  _const_spec(p["wmk"].shape), _const_spec(p["wmvT"].shape)],
        out_specs=[pl.BlockSpec((1, n_mem, D_MODEL), lambda bi: (bi, 0, 0)),
                   pl.BlockSpec((1, D_MODEL, n_mem), lambda bi: (bi, 0, 0))],
        out_shape=[jax.ShapeDtypeStruct((b, n_mem, D_MODEL), BF16), jax.ShapeDtypeStruct((b, D_MODEL, n_mem), BF16)],
        compiler_params=_cparams(1),
        name="mem_kv",
    )(mem, p["g_mem"], p["wmk"], p["wmvT"])

    oa = _attention(qg, kg, vg, GQA_HEADS // GQA_KV_HEADS)
    ob = _attention(qm, km, vm, 1)
    oc = pl.pallas_call(
        _mem_attn_kernel,
        grid=grid,
        in_specs=[featT(D_MODEL), pl.BlockSpec((1, n_mem, D_MODEL), lambda bi, i: (bi, 0, 0)),
                  pl.BlockSpec((1, D_MODEL, n_mem), lambda bi, i: (bi, 0, 0))],
        out_specs=featT(D_MODEL),
        out_shape=jax.ShapeDtypeStruct((b, D_MODEL, s), BF16),
        compiler_params=_cparams(2),
        name="mem_attention",
    )(qc, mk, mvT)

    return pl.pallas_call(
        _merge_kernel,
        grid=grid,
        in_specs=[headT(GQA_HEADS), headT(MLA_HEADS), featT(D_MODEL), featT(N_BRANCH * D_MODEL), x_spec,
                  _const_spec((D_MODEL, D_MODEL))],
        out_specs=x_spec,
        out_shape=jax.ShapeDtypeStruct(x.shape, F32),
        compiler_params=_cparams(2),
        name="merge_out",
    )(oa, ob, oc, gates, x, p["w_out"])


def _trunk(x, mem, layers, g_final):
    b, s, d = x.shape
    tabs = _rope_tables(s)
    for l, p in enumerate(layers):
        x = _mixer(x, mem, p, tabs)
        x = _moe(x.reshape(b * s, d), p["g_ffn"], p["wrT"], p["br"], p["wgu"], p["wd"], g_final,
                 final_norm=(l == len(layers) - 1)).reshape(b, s, d)
    return x


def kernel(x_prompt, x_sample, mem_prompt, mem_sample, norm_mix, norm_mem, w_in, gqa_q_norm, gqa_k_norm, mla_q_a_norm, mla_kv_a_norm, mla_w_qb, mla_w_kvb, mem_w_kv, w_out, norm_ffn, w_group, b_group, w_expert, b_expert, w_gate_up, w_down, norm_final):
    depth = w_in.shape[0]
    layers = [_layer_params(l, norm_mix, norm_mem, w_in, gqa_q_norm, gqa_k_norm, mla_q_a_norm, mla_kv_a_norm,
                            mla_w_qb, mla_w_kvb, mem_w_kv, w_out, norm_ffn, w_group, b_group, w_expert, b_expert,
                            w_gate_up, w_down) for l in range(depth)]
    g_final = norm_final[None, :]
    return (_trunk(x_prompt, mem_prompt, layers, g_final), _trunk(x_sample, mem_sample, layers, g_final))
```

```python
import functools
import math

import jax
import jax.numpy as jnp
from jax import lax
from jax.experimental import pallas as pl
from jax.experimental.pallas import tpu as pltpu

F32 = jnp.float32
BF16 = jnp.bfloat16

D_MODEL = 1024
EPS = 1e-6
GRID_W = 64
ROPE_THETA = 10000.0
GQA_HEADS, GQA_KV_HEADS, GQA_HD = 8, 2, 128
MLA_HEADS, MLA_Q_LORA, MLA_KV_LORA = 8, 384, 256
MLA_NOPE, MLA_ROPE, MLA_V = 64, 32, 128
MEM_HEADS, MEM_HD = 4, 256
N_BRANCH = 3
N_GROUPS, EXPERTS_PER_GROUP, D_EXPERT = 8, 8, 256
N_EXPERTS = N_GROUPS * EXPERTS_PER_GROUP
HEAD_PAD = 128
BF16_SUBLANES = 16
V_ROWS = HEAD_PAD + BF16_SUBLANES
LOG2E = 1.4426950408889634

TOK_TILE = 512
ATT_Q_COLS = 2048
ATT_CHUNKS_PER_BODY = 4
MOE_ROWS = 256
ROUTER_LANES = 128
ROW_SUBLANES = D_MODEL // 128
DMA_UNROLL = 8
VMEM_LIMIT = 56 * 1024 * 1024


def _cparams(n_axes, flags=None):
    return pltpu.CompilerParams(dimension_semantics=("arbitrary",) * n_axes, vmem_limit_bytes=VMEM_LIMIT,
                                flags=flags)


def _nt(a, b):
    return lax.dot_general(a, b, (((1,), (1,)), ((), ())), preferred_element_type=F32)


def _rms_rows(x, g):
    return x * lax.rsqrt(jnp.mean(x * x, axis=-1, keepdims=True) + EPS) * g


def _rms_cols(x, g):
    return x * lax.rsqrt(jnp.mean(x * x, axis=0, keepdims=True) + EPS) * g


def _const_spec(shape):
    return pl.BlockSpec(shape, lambda *_: (0,) * len(shape))


def _store_values(v_ref, vT):
    t = vT.shape[1]
    v_ref[0:HEAD_PAD, :] = vT.astype(BF16)
    ones_row = lax.broadcasted_iota(jnp.int32, (BF16_SUBLANES, t), 0) == 0
    v_ref[HEAD_PAD:V_ROWS, :] = jnp.where(ones_row, 1.0, 0.0).astype(BF16)


def _proj_gqa_kernel(x_ref, g_ref, wqT_ref, wk_ref, wvT_ref, gq_ref, gk_ref, cT_ref, sT_ref, cF_ref, sF_ref,
                     q_ref, k_ref, v_ref):
    h = _rms_rows(x_ref[0], g_ref[...]).astype(BF16)
    c, s, gq = cT_ref[...], sT_ref[...], gq_ref[...]
    half = GQA_HD // 2
    scale = GQA_HD ** -0.5 * LOG2E
    heads_per_dot = 4
    k = jnp.dot(h, wk_ref[...], preferred_element_type=F32)
    q_pieces = []
    for h0 in range(0, GQA_HEADS, heads_per_dot):
        q_pieces.append(_nt(wqT_ref[h0 * GQA_HD:(h0 + heads_per_dot) * GQA_HD, :], h))
        if h0 == 0:
            for j in range(GQA_KV_HEADS):
                kn = _rms_rows(k[:, j * GQA_HD:(j + 1) * GQA_HD], gk_ref[...])
                ko = kn * cF_ref[...] + pltpu.roll(kn, half, 1) * sF_ref[...]
                k_ref[0, j] = ko.astype(BF16)
    vT = _nt(wvT_ref[...], h)
    for piece, qT in enumerate(q_pieces):
        for hd in range(heads_per_dot):
            qn = _rms_cols(qT[hd * GQA_HD:(hd + 1) * GQA_HD], gq)
            x1, x2 = qn[:half], qn[half:]
            o = jnp.concatenate([x1 * c - x2 * s, x2 * c + x1 * s], axis=0) * scale
            q_ref[0, piece * heads_per_dot + hd, 0] = o.astype(BF16)
    for j in range(GQA_KV_HEADS):
        _store_values(v_ref.at[0, j, 0], vT[j * GQA_HD:(j + 1) * GQA_HD])


def _proj_mla_kernel(x_ref, g_ref, wqaT_ref, gqa_ref, wqbT_ref, wkva_ref, gkva_ref, wkn_ref, wkr_ref, wvT_ref,
                     cT_ref, sT_ref, cP_ref, sP_ref, q_ref, k_ref, v_ref):
    h = _rms_rows(x_ref[0], g_ref[...]).astype(BF16)
    qa_raw = _nt(wqaT_ref[...], h)
    kva = jnp.dot(h, wkva_ref[...], preferred_element_type=F32)
    kr2 = jnp.dot(h, wkr_ref[...], preferred_element_type=F32)
    qa = _rms_cols(qa_raw, gqa_ref[...]).astype(BF16)
    qT = jnp.dot(wqbT_ref[...], qa, preferred_element_type=F32)
    kvn = _rms_rows(kva, gkva_ref[...]).astype(BF16)
    kn = jnp.dot(kvn, wkn_ref[...], preferred_element_type=F32)
    kr = kr2[:, :HEAD_PAD] * cP_ref[...] + kr2[:, HEAD_PAD:] * sP_ref[...]
    c, s = cT_ref[...], sT_ref[...]
    hr = MLA_ROPE // 2
    scale = (MLA_NOPE + MLA_ROPE) ** -0.5 * LOG2E
    for hd in range(MLA_HEADS):
        q = qT[hd * HEAD_PAD:(hd + 1) * HEAD_PAD]
        x1, x2 = q[MLA_NOPE:MLA_NOPE + hr], q[MLA_NOPE + hr:MLA_NOPE + 2 * hr]
        o = jnp.concatenate([q[:MLA_NOPE], x1 * c - x2 * s, x2 * c + x1 * s, q[MLA_NOPE + 2 * hr:]], axis=0) * scale
        q_ref[0, hd, 0] = o.astype(BF16)
    vT = _nt(wvT_ref[...], kvn)
    for hd in range(MLA_HEADS):
        k_ref[0, hd] = (kn[:, hd * HEAD_PAD:(hd + 1) * HEAD_PAD] + kr).astype(BF16)
    for hd in range(MLA_HEADS):
        _store_values(v_ref.at[0, hd, 0], vT[hd * MLA_V:(hd + 1) * MLA_V])


def _mem_kv_kernel(mem_ref, g_ref, wk_ref, wvT_ref, mk_ref, mvT_ref):
    h = _rms_rows(mem_ref[0], g_ref[...]).astype(BF16)
    mk_ref[0] = jnp.dot(h, wk_ref[...], preferred_element_type=F32).astype(BF16)
    mvT_ref[0] = _nt(wvT_ref[...], h).astype(BF16)


def _attn_kernel(q_ref, k_ref, v_ref, o_ref, m_sc, acc_sc, s_sc, mc_sc, *, chunks_per_head, n_kv):
    chunk = TOK_TILE
    n_sub = m_sc.shape[0]

    m_sc[...] = jnp.full(m_sc.shape, -jnp.inf, F32)
    acc_sc[...] = jnp.zeros(acc_sc.shape, F32)

    def scores(c, u, slot):
        k = k_ref[0, 0, pl.ds(pl.multiple_of(c * chunk, chunk), chunk), :]
        s = jnp.dot(k, q_ref[0, u // chunks_per_head, u % chunks_per_head],
                    preferred_element_type=F32)
        s_sc[slot, u] = s
        mc_sc[slot, u] = jnp.max(s, axis=0, keepdims=True)

    def softmax_values(c, u, slot):
        m_prev = m_sc[u]
        m_cur = jnp.maximum(m_prev, mc_sc[slot, u])
        alpha = jnp.exp2(m_prev - m_cur)
        p = jnp.exp2(s_sc[slot, u] - m_cur)
        acc_sc[u] = alpha * acc_sc[u] + jnp.dot(v_ref[0, 0, c], p.astype(BF16), preferred_element_type=F32)
        m_sc[u] = m_cur

    def stage(c_scores, c_values):
        for u in range(n_sub):
            if c_scores is not None:
                scores(c_scores, u, c_scores % 2)
            if c_values is not None:
                softmax_values(c_values, u, c_values % 2)

    per_body = ATT_CHUNKS_PER_BODY if (n_kv - 1) // ATT_CHUNKS_PER_BODY >= 2 else 2
    n_bodies = (n_kv - 1) // per_body
    stage(0, None)

    def body(i, carry):
        for t in range(per_body):
            for u in range(n_sub):
                scores(per_body * i + t + 1, u, (t + 1) % 2)
                softmax_values(per_body * i + t, u, t % 2)
        return carry

    lax.fori_loop(0, n_bodies, body, 0)
    for c in range(per_body * n_bodies, n_kv - 1):
        stage(c + 1, c)
    stage(None, n_kv - 1)

    for u in range(n_sub):
        o_ref[0, u // chunks_per_head, u % chunks_per_head] = (
            acc_sc[u, 0:HEAD_PAD] / acc_sc[u, HEAD_PAD:HEAD_PAD + 1]).astype(BF16)


def _attention(qT, k, vT, group):
    b, h, n_kv, hd, chunk = qT.shape
    hkv = k.shape[1]
    s = n_kv * chunk
    assert h == hkv * group and chunk == TOK_TILE
    chunks_per_head = min(ATT_Q_COLS // group, s) // chunk
    n_sub = group * chunks_per_head
    kern = functools.partial(_attn_kernel, chunks_per_head=chunks_per_head, n_kv=n_kv)
    q_spec = pl.BlockSpec((1, group, chunks_per_head, hd, chunk), lambda bi, j, qi: (bi, j, qi, 0, 0))
    return pl.pallas_call(
        kern,
        grid=(b, hkv, n_kv // chunks_per_head),
        in_specs=[
            q_spec,
            pl.BlockSpec((1, 1, s, hd), lambda bi, j, qi: (bi, j, 0, 0)),
            pl.BlockSpec((1, 1, n_kv, V_ROWS, chunk), lambda bi, j, qi: (bi, j, 0, 0, 0)),
        ],
        out_specs=q_spec,
        out_shape=jax.ShapeDtypeStruct(qT.shape, BF16),
        scratch_shapes=[
            pltpu.VMEM((n_sub, 1, chunk), F32),
            pltpu.VMEM((n_sub, V_ROWS, chunk), F32),
            pltpu.VMEM((2, n_sub, chunk, chunk), F32),
            pltpu.VMEM((2, n_sub, 1, chunk), F32),
        ],
        compiler_params=_cparams(3),
        name="attention",
    )(qT, k, vT)


def _mixer_tail_kernel(x_ref, g_ref, oa_ref, ob_ref, mk_ref, mvT_ref, wqcT_ref, wgT_ref, w_ref, y_ref):
    t = x_ref.shape[1]
    x = x_ref[0]
    h = _rms_rows(x, g_ref[...]).astype(BF16)
    scale = MEM_HD ** -0.5 * LOG2E
    piece = 2 * MEM_HD
    out = None
    for p0 in range(0, D_MODEL, piece):
        rows = slice(p0, p0 + piece)
        heads = [slice(p0, p0 + MEM_HD), slice(p0 + MEM_HD, p0 + piece)]
        gate_logits = lambda b: _nt(wgT_ref[b * D_MODEL + p0:b * D_MODEL + p0 + piece, :], h)
        q = [(_nt(wqcT_ref[r, :], h) * scale).astype(BF16) for r in heads]
        s = []
        g = []
        for i, r in enumerate(heads):
            s.append(jnp.dot(mk_ref[0, :, r], q[i], preferred_element_type=F32))
            g.append(gate_logits(i))
        oc = []
        for i, r in enumerate(heads):
            pr = jnp.exp2(s[i] - jnp.max(s[i], axis=0, keepdims=True))
            l = jnp.sum(pr, axis=0, keepdims=True)
            oc.append(jnp.dot(mvT_ref[0, r, :], pr.astype(BF16), preferred_element_type=F32) / l)
            if i == 0:
                g.append(gate_logits(2))
        oa = oa_ref[0].reshape(D_MODEL, t)[rows].astype(F32)
        ob = ob_ref[0].reshape(D_MODEL, t)[rows].astype(F32)
        merged = (jax.nn.sigmoid(g[0]) * oa + jax.nn.sigmoid(g[1]) * ob
                  + jax.nn.sigmoid(g[2]) * jnp.concatenate(oc, axis=0)).astype(BF16)
        part = lax.dot_general(merged, w_ref[rows, :], (((0,), (0,)), ((), ())), preferred_element_type=F32)
        out = part if out is None else out + part
    y_ref[0] = x + out


def _first_index_of_max(v, vmax, iota, n):
    return jnp.min(jnp.where(v == vmax, iota, n), axis=0, keepdims=True)


def _router_kernel(x_ref, g_ref, wT_ref, b_ref, ri_ref, rw_ref, cnt_ref, carry_sc, before_sc):
    step = pl.program_id(0)
    t = x_ref.shape[0]

    @pl.when(step == 0)
    def _():
        carry_sc[...] = jnp.zeros(carry_sc.shape, F32)
        before = lax.broadcasted_iota(jnp.int32, (t, t), 0) < lax.broadcasted_iota(jnp.int32, (t, t), 1)
        before_sc[...] = jnp.where(before, 1.0, 0.0).astype(BF16)

    h = _rms_rows(x_ref[...], g_ref[...])
    h_hi = h.astype(BF16)
    h_lo = (h - h_hi.astype(F32)).astype(BF16)
    w_hi, w_lo = wT_ref[0], wT_ref[1]
    logits = _nt(w_hi, h_hi) + _nt(w_hi, h_lo) + _nt(w_lo, h_hi) + b_ref[...]
    iota_g = lax.broadcasted_iota(jnp.int32, (N_GROUPS, t), 0)
    gl = logits[0:N_GROUPS]
    gmax = jnp.max(gl, axis=0, keepdims=True)
    gidx = _first_index_of_max(gl, gmax, iota_g, N_GROUPS)
    gw = 1.0 / jnp.sum(jnp.exp(gl - gmax), axis=0, keepdims=True)
    sel = jnp.zeros((EXPERTS_PER_GROUP, t), F32)
    for g in range(N_GROUPS):
        r0 = N_GROUPS + g * EXPERTS_PER_GROUP
        sel = sel + jnp.where(gidx == g, logits[r0:r0 + EXPERTS_PER_GROUP], 0.0)
    v1 = jnp.max(sel, axis=0, keepdims=True)
    i1 = _first_index_of_max(sel, v1, iota_g, EXPERTS_PER_GROUP)
    sel2 = jnp.where(iota_g == i1, -jnp.inf, sel)
    v2 = jnp.max(sel2, axis=0, keepdims=True)
    i2 = _first_index_of_max(sel2, v2, iota_g, EXPERTS_PER_GROUP)
    e = jnp.exp(v2 - v1)
    w1 = gw / (1.0 + e)
    w2 = gw * e / (1.0 + e)
    e1 = gidx * EXPERTS_PER_GROUP + i1
    e2 = gidx * EXPERTS_PER_GROUP + i2

    iota_e = lax.broadcasted_iota(jnp.int32, (N_EXPERTS, t), 0)
    hit1 = iota_e == e1
    hit2 = iota_e == e2
    hits = jnp.where(hit1 | hit2, 1.0, 0.0)
    prefix = jnp.dot(hits.astype(BF16), before_sc[...],
                     preferred_element_type=F32) + carry_sc[...]
    r1 = jnp.sum(jnp.where(hit1, prefix, 0.0), axis=0, keepdims=True)
    r2 = jnp.sum(jnp.where(hit2, prefix, 0.0), axis=0, keepdims=True)
    carry_sc[...] = carry_sc[...] + jnp.sum(hits, axis=1, keepdims=True)

    ri_ref[...] = jnp.zeros(ri_ref.shape, jnp.int32)
    ri_ref[0:1, :] = e1
    ri_ref[1:2, :] = e2
    ri_ref[2:3, :] = r1.astype(jnp.int32)
    ri_ref[3:4, :] = r2.astype(jnp.int32)
    rw_ref[...] = jnp.zeros(rw_ref.shape, F32)
    rw_ref[0:1, :] = w1
    rw_ref[1:2, :] = w2
    cnt_ref[...] = jnp.broadcast_to(carry_sc[...], cnt_ref.shape)


def _dest_kernel(cnt_ref, ri_ref, dest_ref, blk_ref, nused_ref, padinfo_ref, *, n_blocks_padded):
    cnt = cnt_ref[...]
    pad = jnp.floor((cnt + (MOE_ROWS - 1)) * (1.0 / MOE_ROWS)) * MOE_ROWS
    lower = jnp.where(lax.broadcasted_iota(jnp.int32, (N_EXPERTS, N_EXPERTS), 1)
                      <= lax.broadcasted_iota(jnp.int32, (N_EXPERTS, N_EXPERTS), 0), 1.0, 0.0)
    pad_end = jnp.dot(lower, pad, precision=lax.Precision.HIGHEST, preferred_element_type=F32)
    pad_start = (pad_end - pad)[:, 0:1]
    t = ri_ref.shape[1]
    iota_e = lax.broadcasted_iota(jnp.int32, (N_EXPERTS, t), 0)
    dest_ref[...] = jnp.zeros(dest_ref.shape, jnp.int32)
    for kk in range(2):
        start = jnp.sum(jnp.where(iota_e == ri_ref[kk:kk + 1, :], pad_start, 0.0), axis=0, keepdims=True)
        dest_ref[kk:kk + 1, :] = (start.astype(jnp.int32) + ri_ref[2 + kk:3 + kk, :]) * ROW_SUBLANES
    blk_start = (lax.broadcasted_iota(jnp.int32, (N_EXPERTS, n_blocks_padded), 1) * MOE_ROWS).astype(F32)
    owner = jnp.sum(jnp.where(pad_end[:, 0:1] <= blk_start, 1.0, 0.0), axis=0, keepdims=True)
    blk_ref[...] = jnp.broadcast_to(jnp.minimum(owner, N_EXPERTS - 1.0).astype(jnp.int32), blk_ref.shape)
    nused_ref[...] = jnp.broadcast_to((pad_end[N_EXPERTS - 1:N_EXPERTS, :] * (1.0 / MOE_ROWS)).astype(jnp.int32),
                                      nused_ref.shape)
    diag = (lax.broadcasted_iota(jnp.int32, (N_EXPERTS, 128), 0) == lax.broadcasted_iota(jnp.int32, (N_EXPERTS, 128), 1))
    first_pad = jnp.sum(jnp.where(diag, (pad_end - pad + cnt) * ROW_SUBLANES, 0.0), axis=0, keepdims=True)
    n_pad = jnp.sum(jnp.where(diag, pad - cnt, 0.0), axis=0, keepdims=True)
    padinfo_ref[...] = jnp.zeros(padinfo_ref.shape, jnp.int32)
    padinfo_ref[0:1, :] = first_pad.astype(jnp.int32)
    padinfo_ref[1:2, :] = n_pad.astype(jnp.int32)
    padinfo_ref[2:3, :] = (pad_end[N_EXPERTS - 1:N_EXPERTS, :] * (1.0 / MOE_ROWS)).astype(jnp.int32)


def _to_row_tiles(ref, x):
    n = x.shape[0]
    for c in range(ROW_SUBLANES):
        ref[pl.ds(c, n, stride=ROW_SUBLANES), :] = x[:, c * 128:(c + 1) * 128]


def _from_row_tiles(ref, n):
    return jnp.concatenate([ref[pl.ds(c, n, stride=ROW_SUBLANES), :] for c in range(ROW_SUBLANES)], axis=1)


def _tile_copy(src_ref, src_row8, dst_ref, dst_row8, sem):
    return pltpu.make_async_copy(src_ref.at[pl.ds(pl.multiple_of(src_row8, ROW_SUBLANES), ROW_SUBLANES)],
                                 dst_ref.at[pl.ds(pl.multiple_of(dst_row8, ROW_SUBLANES), ROW_SUBLANES)], sem)


def _dispatch_kernel(dest_ref, padinfo_ref, x_ref, g_ref, xs_ref, h_sc, zero_sc, sem, blk_sem, *, tile,
                     experts_per_step, n_blocks):
    step = pl.program_id(0)
    _to_row_tiles(h_sc, _rms_rows(x_ref[...], g_ref[...]))
    zero_sc[...] = jnp.zeros(zero_sc.shape, F32)

    def block_copy(b):
        rows = MOE_ROWS * ROW_SUBLANES
        return pltpu.make_async_copy(zero_sc, xs_ref.at[pl.ds(pl.multiple_of(b * rows, rows), rows)], blk_sem)

    for j in range(experts_per_step):
        b = padinfo_ref[2, 0] + step * experts_per_step + j

        @pl.when(b < n_blocks)
        def _():
            block_copy(b).start(priority=1)

    def issue(t, carry):
        _tile_copy(h_sc, t * ROW_SUBLANES, xs_ref, dest_ref[0, 0, t], sem).start()
        _tile_copy(h_sc, t * ROW_SUBLANES, xs_ref, dest_ref[0, 0, tile + t], sem).start()
        return carry

    def drain(t, carry):
        _tile_copy(h_sc, 0, xs_ref, 0, sem).wait()
        _tile_copy(h_sc, 0, xs_ref, 0, sem).wait()
        return carry

    lax.fori_loop(0, tile, issue, 0, unroll=DMA_UNROLL)

    n_zero = jnp.int32(0)
    for j in range(experts_per_step):
        e = step * experts_per_step + j
        e_c = jnp.minimum(e, N_EXPERTS - 1)
        first = padinfo_ref[0, e_c]
        count = jnp.where(e < N_EXPERTS, padinfo_ref[1, e_c], 0)

        def zero_row(r, carry, first=first):
            _tile_copy(zero_sc, 0, xs_ref, first + r * ROW_SUBLANES, sem).start()
            return carry

        lax.fori_loop(0, count, zero_row, 0)
        n_zero = n_zero + count

    lax.fori_loop(0, tile, drain, 0, unroll=DMA_UNROLL)
    lax.fori_loop(0, n_zero, lambda r, carry: (_tile_copy(zero_sc, 0, xs_ref, 0, sem).wait(), carry)[1], 0)
    for j in range(experts_per_step):
        b = padinfo_ref[2, 0] + step * experts_per_step + j

        @pl.when(b < n_blocks)
        def _():
            block_copy(b).wait()


def _expert_kernel(blk_ref, nused_ref, xs_ref, wgu_ref, wd_ref, ys_ref):
    b = pl.program_id(0)

    @pl.when(b < nused_ref[0])
    def _():
        x = _from_row_tiles(xs_ref, MOE_ROWS).astype(BF16)
        gu = jnp.dot(x, wgu_ref[0], preferred_element_type=F32)
        a, u = gu[:, :D_EXPERT], gu[:, D_EXPERT:]
        hidden = (a * jax.nn.sigmoid(a) * u).astype(BF16)
        _to_row_tiles(ys_ref, jnp.dot(hidden, wd_ref[0], preferred_element_type=F32))

    @pl.when(b >= nused_ref[0])
    def _():
        ys_ref[...] = jnp.zeros(ys_ref.shape, F32)


def _combine_kernel(dest_ref, dest_next_ref, ys_ref, x_ref, w_ref, g_ref, y_ref, buf, sem, *, tile, final_norm):
    step = pl.program_id(0)
    slot = lax.rem(step, 2)

    def issue(d_ref, slot_):
        def body(t, carry):
            _tile_copy(ys_ref, d_ref[0, 0, t], buf.at[slot_, 0], t * ROW_SUBLANES, sem.at[slot_]).start()
            _tile_copy(ys_ref, d_ref[0, 0, tile + t], buf.at[slot_, 1], t * ROW_SUBLANES, sem.at[slot_]).start()
            return carry
        lax.fori_loop(0, tile, body, 0, unroll=DMA_UNROLL)

    @pl.when(step == 0)
    def _():
        issue(dest_ref, 0)

    @pl.when(step + 1 < pl.num_programs(0))
    def _():
        issue(dest_next_ref, 1 - slot)

    def drain(t, carry):
        _tile_copy(ys_ref, 0, buf.at[slot, 0], 0, sem.at[slot]).wait()
        _tile_copy(ys_ref, 0, buf.at[slot, 1], 0, sem.at[slot]).wait()
        return carry

    lax.fori_loop(0, tile, drain, 0, unroll=DMA_UNROLL)
    w = w_ref[...]
    y = x_ref[...] + (w[:, 0:1] * _from_row_tiles(buf.at[slot, 0], tile)
                      + w[:, 1:2] * _from_row_tiles(buf.at[slot, 1], tile))
    if final_norm:
        y = _rms_rows(y, g_ref[...])
    y_ref[...] = y


def _moe(x, g_ffn, wrT, br, wgu, wd, g_final, final_norm):
    n_tok = x.shape[0]
    tile = min(TOK_TILE, n_tok)
    n_tiles = n_tok // tile
    n_blocks = (2 * n_tok + MOE_ROWS - 1) // MOE_ROWS + N_EXPERTS
    n_blocks_padded = (n_blocks + 127) // 128 * 128
    n_rows = n_blocks * MOE_ROWS

    ri, rw, cnt = pl.pallas_call(
        _router_kernel,
        grid=(n_tiles,),
        in_specs=[
            pl.BlockSpec((tile, D_MODEL), lambda i: (i, 0)),
            _const_spec((1, D_MODEL)),
            _const_spec((2, ROUTER_LANES, D_MODEL)),
            _const_spec((ROUTER_LANES, 1)),
        ],
        out_specs=[
            pl.BlockSpec((8, tile), lambda i: (0, i)),
            pl.BlockSpec((8, tile), lambda i: (0, i)),
            _const_spec((N_EXPERTS, 128)),
        ],
        out_shape=[
            jax.ShapeDtypeStruct((8, n_tok), jnp.int32),
            jax.ShapeDtypeStruct((8, n_tok), F32),
            jax.ShapeDtypeStruct((N_EXPERTS, 128), F32),
        ],
        scratch_shapes=[pltpu.VMEM((N_EXPERTS, 1), F32), pltpu.VMEM((tile, tile), BF16)],
        compiler_params=_cparams(1),
        name="moe_router",
    )(x, g_ffn, wrT, br)

    dest, blk, nused, padinfo = pl.pallas_call(
        functools.partial(_dest_kernel, n_blocks_padded=n_blocks_padded),
        grid=(n_tiles,),
        in_specs=[_const_spec((N_EXPERTS, 128)), pl.BlockSpec((8, tile), lambda i: (0, i))],
        out_specs=[
            pl.BlockSpec((8, tile), lambda i: (0, i)),
            _const_spec((8, n_blocks_padded)),
            _const_spec((8, 128)),
            _const_spec((8, 128)),
        ],
        out_shape=[
            jax.ShapeDtypeStruct((8, n_tok), jnp.int32),
            jax.ShapeDtypeStruct((8, n_blocks_padded), jnp.int32),
            jax.ShapeDtypeStruct((8, 128), jnp.int32),
            jax.ShapeDtypeStruct((8, 128), jnp.int32),
        ],
        compiler_params=_cparams(1),
        name="moe_dest",
    )(cnt, ri)

    dest_tiles = dest[0:2].reshape(2, n_tiles, tile).transpose(1, 0, 2).reshape(n_tiles, 1, 2 * tile)
    dest_spec = pl.BlockSpec((1, 1, 2 * tile), lambda i: (i, 0, 0), memory_space=pltpu.SMEM)
    any_spec = pl.BlockSpec(memory_space=pl.ANY)

    xs = pl.pallas_call(
        functools.partial(_dispatch_kernel, tile=tile, experts_per_step=pl.cdiv(N_EXPERTS, n_tiles),
                          n_blocks=n_blocks),
        grid=(n_tiles,),
        in_specs=[dest_spec, pl.BlockSpec((8, 128), lambda i: (0, 0), memory_space=pltpu.SMEM),
                  pl.BlockSpec((tile, D_MODEL), lambda i: (i, 0)), _const_spec((1, D_MODEL))],
        out_specs=any_spec,
        out_shape=jax.ShapeDtypeStruct((n_rows * ROW_SUBLANES, 128), F32),
        scratch_shapes=[pltpu.VMEM((tile * ROW_SUBLANES, 128), F32),
                        pltpu.VMEM((MOE_ROWS * ROW_SUBLANES, 128), F32),
                        pltpu.SemaphoreType.DMA(()), pltpu.SemaphoreType.DMA(())],
        compiler_params=_cparams(1),
        name="moe_dispatch",
    )(dest_tiles, padinfo, x, g_ffn)

    blk_rows = MOE_ROWS * ROW_SUBLANES
    ys = pl.pallas_call(
        _expert_kernel,
        grid_spec=pltpu.PrefetchScalarGridSpec(
            num_scalar_prefetch=2,
            grid=(n_blocks,),
            in_specs=[
                pl.BlockSpec((blk_rows, 128), lambda b, blk, nu: (b, 0)),
                pl.BlockSpec((1, D_MODEL, 2 * D_EXPERT), lambda b, blk, nu: (blk[b], 0, 0)),
                pl.BlockSpec((1, D_EXPERT, D_MODEL), lambda b, blk, nu: (blk[b], 0, 0)),
            ],
            out_specs=pl.BlockSpec((blk_rows, 128), lambda b, blk, nu: (b, 0)),
        ),
        out_shape=jax.ShapeDtypeStruct((n_rows * ROW_SUBLANES, 128), F32),
        compiler_params=_cparams(1),
        name="moe_experts",
    )(blk[0, :n_blocks], nused[0, :1], xs, wgu, wd)

    ctile = min(256, n_tok)
    c_tiles = n_tok // ctile
    dest_c = dest[0:2].reshape(2, c_tiles, ctile).transpose(1, 0, 2).reshape(c_tiles, 1, 2 * ctile)
    return pl.pallas_call(
        functools.partial(_combine_kernel, tile=ctile, final_norm=final_norm),
        grid=(c_tiles,),
        in_specs=[
            pl.BlockSpec((1, 1, 2 * ctile), lambda i: (i, 0, 0), memory_space=pltpu.SMEM),
            pl.BlockSpec((1, 1, 2 * ctile), lambda i: (jnp.minimum(i + 1, c_tiles - 1), 0, 0),
                         memory_space=pltpu.SMEM),
            any_spec,
            pl.BlockSpec((ctile, D_MODEL), lambda i: (i, 0)),
            pl.BlockSpec((ctile, 2), lambda i: (i, 0)),
            _const_spec((1, D_MODEL)),
        ],
        out_specs=pl.BlockSpec((ctile, D_MODEL), lambda i: (i, 0)),
        out_shape=jax.ShapeDtypeStruct((n_tok, D_MODEL), F32),
        scratch_shapes=[pltpu.VMEM((2, 2, ctile * ROW_SUBLANES, 128), F32), pltpu.SemaphoreType.DMA((2,))],
        compiler_params=_cparams(1),
        name="moe_combine",
    )(dest_c, dest_c, ys, x, rw[0:2].T, g_final)


def _rope_tables(n_tokens):
    rows = n_tokens // GRID_W
    row_idx = jnp.repeat(jnp.arange(rows, dtype=F32), GRID_W)
    col_idx = jnp.tile(jnp.arange(GRID_W, dtype=F32), rows)

    def angles(rot_dim):
        n_freq = rot_dim // 4
        inv_freq = ROPE_THETA ** (-jnp.arange(n_freq, dtype=F32) / n_freq)
        return jnp.concatenate([row_idx[:, None] * inv_freq, col_idx[:, None] * inv_freq], axis=-1)

    ag, am = angles(GQA_HD), angles(MLA_ROPE)
    cg, sg, cm, sm = jnp.cos(ag), jnp.sin(ag), jnp.cos(am), jnp.sin(am)
    hr = MLA_ROPE // 2
    zl = jnp.zeros((n_tokens, MLA_NOPE), F32)
    zr = jnp.zeros((n_tokens, HEAD_PAD - MLA_NOPE - MLA_ROPE), F32)
    del hr
    return dict(
        cT_g=cg.T, sT_g=sg.T,
        cF_g=jnp.concatenate([cg, cg], axis=1), sF_g=jnp.concatenate([-sg, sg], axis=1),
        cT_m=cm.T, sT_m=sm.T,
        cP_m=jnp.concatenate([zl, cm, cm, zr], axis=1), sP_m=jnp.concatenate([zl, -sm, sm, zr], axis=1),
    )


def _hi_lo(w):
    hi = w.astype(BF16)
    return jnp.stack([hi, (w - hi.astype(F32)).astype(BF16)])


def _layer_params(l, norm_mix, norm_mem, w_in, gqa_q_norm, gqa_k_norm, mla_q_a_norm, mla_kv_a_norm, mla_w_qb,
                  mla_w_kvb, mem_w_kv, w_out, norm_ffn, w_group, b_group, w_expert, b_expert, w_gate_up, w_down):
    splits = []
    off = 0
    for width in (GQA_HEADS * GQA_HD, GQA_KV_HEADS * GQA_HD, GQA_KV_HEADS * GQA_HD, MLA_Q_LORA, MLA_KV_LORA,
                  MLA_ROPE, MEM_HEADS * MEM_HD, N_BRANCH * D_MODEL):
        splits.append(w_in[l][:, off:off + width])
        off += width
    wq_g, wk_g, wv_g, wq_a, wkv_a, wk_r, wq_c, w_gate = splits
    hr = MLA_ROPE // 2
    pad_l = jnp.zeros((D_MODEL, MLA_NOPE), F32)
    pad_r = jnp.zeros((D_MODEL, HEAD_PAD - MLA_NOPE - MLA_ROPE), F32)
    wk_r_swapped = jnp.concatenate([wk_r[:, hr:], wk_r[:, :hr]], axis=1)
    wkr2 = jnp.concatenate([pad_l, wk_r, pad_r, pad_l, wk_r_swapped, pad_r], axis=1)
    qk = MLA_NOPE + MLA_ROPE
    wqb = jnp.pad(mla_w_qb[l].reshape(MLA_Q_LORA, MLA_HEADS, qk), ((0, 0), (0, 0), (0, HEAD_PAD - qk)))
    wkvb = mla_w_kvb[l].reshape(MLA_KV_LORA, MLA_HEADS, MLA_NOPE + MLA_V)
    wkn = jnp.pad(wkvb[:, :, :MLA_NOPE], ((0, 0), (0, 0), (0, HEAD_PAD - MLA_NOPE)))
    wv_m = wkvb[:, :, MLA_NOPE:]
    n_router = N_GROUPS + N_EXPERTS
    return dict(
        g_mix=norm_mix[l][None, :], g_mem=norm_mem[l][None, :], g_ffn=norm_ffn[l][None, :],
        wqT_g=wq_g.T.astype(BF16), wk_g=wk_g.astype(BF16), wvT_g=wv_g.T.astype(BF16),
        gq=gqa_q_norm[l][:, None], gk=gqa_k_norm[l][None, :],
        wqaT=wq_a.T.astype(BF16), gqa=mla_q_a_norm[l][:, None],
        wqbT=wqb.reshape(MLA_Q_LORA, MLA_HEADS * HEAD_PAD).T.astype(BF16),
        wkva=wkv_a.astype(BF16), gkva=mla_kv_a_norm[l][None, :],
        wkn=wkn.reshape(MLA_KV_LORA, MLA_HEADS * HEAD_PAD).astype(BF16), wkr2=wkr2.astype(BF16),
        wvT_m=wv_m.reshape(MLA_KV_LORA, MLA_HEADS * MLA_V).T.astype(BF16),
        wqcT=wq_c.T.astype(BF16), wgT=w_gate.T.astype(BF16),
        wmk=mem_w_kv[l][:, :MEM_HEADS * MEM_HD].astype(BF16),
        wmvT=mem_w_kv[l][:, MEM_HEADS * MEM_HD:].T.astype(BF16),
        w_out=w_out[l].astype(BF16),
        wrT=_hi_lo(jnp.concatenate([w_group[l].T, w_expert[l].T,
                                    jnp.zeros((ROUTER_LANES - n_router, D_MODEL), F32)], axis=0)),
        br=jnp.concatenate([b_group[l], b_expert[l], jnp.zeros((ROUTER_LANES - n_router,), F32)])[:, None],
        wgu=w_gate_up[l].astype(BF16), wd=w_down[l].astype(BF16),
    )


def _mixer(x, mem, p, tabs):
    b, s, _ = x.shape
    tile = min(TOK_TILE, s)
    n_t = s // tile
    grid = (b, n_t)
    x_spec = pl.BlockSpec((1, tile, D_MODEL), lambda bi, i: (bi, i, 0))
    colT = lambda rows: pl.BlockSpec((rows, tile), lambda bi, i: (0, i))
    rowF = pl.BlockSpec((tile, HEAD_PAD), lambda bi, i: (i, 0))
    headT = lambda heads: pl.BlockSpec((1, heads, 1, HEAD_PAD, tile), lambda bi, i: (bi, 0, i, 0, 0))
    keyF = lambda heads: pl.BlockSpec((1, heads, tile, HEAD_PAD), lambda bi, i: (bi, 0, i, 0))
    valT = lambda heads: pl.BlockSpec((1, heads, 1, V_ROWS, tile), lambda bi, i: (bi, 0, i, 0, 0))

    qg, kg, vg = pl.pallas_call(
        _proj_gqa_kernel,
        grid=grid,
        in_specs=[x_spec, _const_spec((1, D_MODEL)), _const_spec(p["wqT_g"].shape), _const_spec(p["wk_g"].shape),
                  _const_spec(p["wvT_g"].shape), _const_spec((GQA_HD, 1)), _const_spec((1, GQA_HD)),
                  colT(GQA_HD // 2), colT(GQA_HD // 2), rowF, rowF],
        out_specs=[headT(GQA_HEADS), keyF(GQA_KV_HEADS), valT(GQA_KV_HEADS)],
        out_shape=[jax.ShapeDtypeStruct((b, GQA_HEADS, n_t, HEAD_PAD, tile), BF16),
                   jax.ShapeDtypeStruct((b, GQA_KV_HEADS, s, HEAD_PAD), BF16),
                   jax.ShapeDtypeStruct((b, GQA_KV_HEADS, n_t, V_ROWS, tile), BF16)],
        compiler_params=_cparams(2),
        name="proj_gqa",
    )(x, p["g_mix"], p["wqT_g"], p["wk_g"], p["wvT_g"], p["gq"], p["gk"],
      tabs["cT_g"], tabs["sT_g"], tabs["cF_g"], tabs["sF_g"])

    qm, km, vm = pl.pallas_call(
        _proj_mla_kernel,
        grid=grid,
        in_specs=[x_spec, _const_spec((1, D_MODEL)), _const_spec(p["wqaT"].shape), _const_spec((MLA_Q_LORA, 1)),
                  _const_spec(p["wqbT"].shape), _const_spec(p["wkva"].shape), _const_spec((1, MLA_KV_LORA)),
                  _const_spec(p["wkn"].shape), _const_spec(p["wkr2"].shape), _const_spec(p["wvT_m"].shape),
                  colT(MLA_ROPE // 2), colT(MLA_ROPE // 2), rowF, rowF],
        out_specs=[headT(MLA_HEADS), keyF(MLA_HEADS), valT(MLA_HEADS)],
        out_shape=[jax.ShapeDtypeStruct((b, MLA_HEADS, n_t, HEAD_PAD, tile), BF16),
                   jax.ShapeDtypeStruct((b, MLA_HEADS, s, HEAD_PAD), BF16),
                   jax.ShapeDtypeStruct((b, MLA_HEADS, n_t, V_ROWS, tile), BF16)],
        compiler_params=_cparams(2),
        name="proj_mla",
    )(x, p["g_mix"], p["wqaT"], p["gqa"], p["wqbT"], p["wkva"], p["gkva"], p["wkn"], p["wkr2"], p["wvT_m"],
      tabs["cT_m"], tabs["sT_m"], tabs["cP_m"], tabs["sP_m"])

    n_mem = mem.shape[1]
    mk, mvT = pl.pallas_call(
        _mem_kv_kernel,
        grid=(b,),
        in_specs=[pl.BlockSpec((1, n_mem, D_MODEL), lambda bi: (bi, 0, 0)), _const_spec((1, D_MODEL)),
                  _const_spec(p["wmk"].shape), _const_spec(p["wmvT"].shape)],
        out_specs=[pl.BlockSpec((1, n_mem, D_MODEL), lambda bi: (bi, 0, 0)),
                   pl.BlockSpec((1, D_MODEL, n_mem), lambda bi: (bi, 0, 0))],
        out_shape=[jax.ShapeDtypeStruct((b, n_mem, D_MODEL), BF16), jax.ShapeDtypeStruct((b, D_MODEL, n_mem), BF16)],
        compiler_params=_cparams(1),
        name="mem_kv",
    )(mem, p["g_mem"], p["wmk"], p["wmvT"])

    oa = _attention(qg, kg, vg, GQA_HEADS // GQA_KV_HEADS)
    ob = _attention(qm, km, vm, 1)
    return pl.pallas_call(
        _mixer_tail_kernel,
        grid=grid,
        in_specs=[x_spec, _const_spec((1, D_MODEL)), headT(GQA_HEADS), headT(MLA_HEADS),
                  pl.BlockSpec((1, n_mem, D_MODEL), lambda bi, i: (bi, 0, 0)),
                  pl.BlockSpec((1, D_MODEL, n_mem), lambda bi, i: (bi, 0, 0)),
                  _const_spec(p["wqcT"].shape), _const_spec(p["wgT"].shape), _const_spec((D_MODEL, D_MODEL))],
        out_specs=x_spec,
        out_shape=jax.ShapeDtypeStruct(x.shape, F32),
        compiler_params=_cparams(2),
        name="mixer_tail",
    )(x, p["g_mix"], oa, ob, mk, mvT, p["wqcT"], p["wgT"], p["w_out"])


def _trunk(x, mem, layers, g_final):
    b, s, d = x.shape
    tabs = _rope_tables(s)
    for l, p in enumerate(layers):
        x = _mixer(x, mem, p, tabs)
        x = _moe(x.reshape(b * s, d), p["g_ffn"], p["wrT"], p["br"], p["wgu"], p["wd"], g_final,
                 final_norm=(l == len(layers) - 1)).reshape(b, s, d)
    return x


def kernel(x_prompt, x_sample, mem_prompt, mem_sample, norm_mix, norm_mem, w_in, gqa_q_norm, gqa_k_norm, mla_q_a_norm, mla_kv_a_norm, mla_w_qb, mla_w_kvb, mem_w_kv, w_out, norm_ffn, w_group, b_group, w_expert, b_expert, w_gate_up, w_down, norm_final):
    depth = w_in.shape[0]
    layers = [_layer_params(l, norm_mix, norm_mem, w_in, gqa_q_norm, gqa_k_norm, mla_q_a_norm, mla_kv_a_norm,
                            mla_w_qb, mla_w_kvb, mem_w_kv, w_out, norm_ffn, w_group, b_group, w_expert, b_expert,
                            w_gate_up, w_down) for l in range(depth)]
    g_final = norm_final[None, :]
    return (_trunk(x_prompt, mem_prompt, layers, g_final), _trunk(x_sample, mem_sample, layers, g_final))
```

```python
import functools
import math

import jax
import jax.numpy as jnp
from jax import lax
from jax.experimental import pallas as pl
from jax.experimental.pallas import tpu as pltpu

F32 = jnp.float32
BF16 = jnp.bfloat16

D_MODEL = 1024
EPS = 1e-6
GRID_W = 64
ROPE_THETA = 10000.0
GQA_HEADS, GQA_KV_HEADS, GQA_HD = 8, 2, 128
MLA_HEADS, MLA_Q_LORA, MLA_KV_LORA = 8, 384, 256
MLA_NOPE, MLA_ROPE, MLA_V = 64, 32, 128
MEM_HEADS, MEM_HD = 4, 256
N_BRANCH = 3
N_GROUPS, EXPERTS_PER_GROUP, D_EXPERT = 8, 8, 256
N_EXPERTS = N_GROUPS * EXPERTS_PER_GROUP
HEAD_PAD = 128
BF16_SUBLANES = 16
V_ROWS = HEAD_PAD + BF16_SUBLANES
LOG2E = 1.4426950408889634

TOK_TILE = 512
ATT_Q_COLS = 2048
ATT_CHUNKS_PER_BODY = 4
MOE_ROWS = 256
ROUTER_LANES = 128
ROW_SUBLANES = D_MODEL // 128
DMA_UNROLL = 8
VMEM_LIMIT = 56 * 1024 * 1024


def _cparams(n_axes, flags=None):
    return pltpu.CompilerParams(dimension_semantics=("arbitrary",) * n_axes, vmem_limit_bytes=VMEM_LIMIT,
                                flags=flags)


def _nt(a, b):
    return lax.dot_general(a, b, (((1,), (1,)), ((), ())), preferred_element_type=F32)


def _rms_rows(x, g):
    return x * lax.rsqrt(jnp.mean(x * x, axis=-1, keepdims=True) + EPS) * g


def _rms_cols(x, g):
    return x * lax.rsqrt(jnp.mean(x * x, axis=0, keepdims=True) + EPS) * g


def _const_spec(shape):
    return pl.BlockSpec(shape, lambda *_: (0,) * len(shape))


def _store_values(v_ref, vT):
    t = vT.shape[1]
    v_ref[0:HEAD_PAD, :] = vT.astype(BF16)
    ones_row = lax.broadcasted_iota(jnp.int32, (BF16_SUBLANES, t), 0) == 0
    v_ref[HEAD_PAD:V_ROWS, :] = jnp.where(ones_row, 1.0, 0.0).astype(BF16)


def _proj_gqa_kernel(x_ref, g_ref, wqT_ref, wk_ref, wvT_ref, gq_ref, gk_ref, cT_ref, sT_ref, cF_ref, sF_ref,
                     q_ref, k_ref, v_ref):
    h = _rms_rows(x_ref[0], g_ref[...]).astype(BF16)
    c, s, gq = cT_ref[...], sT_ref[...], gq_ref[...]
    half = GQA_HD // 2
    scale = GQA_HD ** -0.5 * LOG2E
    heads_per_dot = 4
    k = jnp.dot(h, wk_ref[...], preferred_element_type=F32)
    q_pieces = []
    for h0 in range(0, GQA_HEADS, heads_per_dot):
        q_pieces.append(_nt(wqT_ref[h0 * GQA_HD:(h0 + heads_per_dot) * GQA_HD, :], h))
        if h0 == 0:
            for j in range(GQA_KV_HEADS):
                kn = _rms_rows(k[:, j * GQA_HD:(j + 1) * GQA_HD], gk_ref[...])
                ko = kn * cF_ref[...] + pltpu.roll(kn, half, 1) * sF_ref[...]
                k_ref[0, j] = ko.astype(BF16)
    vT = _nt(wvT_ref[...], h)
    for piece, qT in enumerate(q_pieces):
        for hd in range(heads_per_dot):
            qn = _rms_cols(qT[hd * GQA_HD:(hd + 1) * GQA_HD], gq)
            x1, x2 = qn[:half], qn[half:]
            o = jnp.concatenate([x1 * c - x2 * s, x2 * c + x1 * s], axis=0) * scale
            q_ref[0, piece * heads_per_dot + hd, 0] = o.astype(BF16)
    for j in range(GQA_KV_HEADS):
        _store_values(v_ref.at[0, j, 0], vT[j * GQA_HD:(j + 1) * GQA_HD])


def _proj_mla_kernel(x_ref, g_ref, wqaT_ref, gqa_ref, wqbT_ref, wkva_ref, gkva_ref, wkn_ref, wkr_ref, wvT_ref,
                     cT_ref, sT_ref, cP_ref, sP_ref, q_ref, k_ref, v_ref):
    h = _rms_rows(x_ref[0], g_ref[...]).astype(BF16)
    qa_raw = _nt(wqaT_ref[...], h)
    kva = jnp.dot(h, wkva_ref[...], preferred_element_type=F32)
    kr2 = jnp.dot(h, wkr_ref[...], preferred_element_type=F32)
    qa = _rms_cols(qa_raw, gqa_ref[...]).astype(BF16)
    qT = jnp.dot(wqbT_ref[...], qa, preferred_element_type=F32)
    kvn = _rms_rows(kva, gkva_ref[...]).astype(BF16)
    kn = jnp.dot(kvn, wkn_ref[...], preferred_element_type=F32)
    kr = kr2[:, :HEAD_PAD] * cP_ref[...] + kr2[:, HEAD_PAD:] * sP_ref[...]
    c, s = cT_ref[...], sT_ref[...]
    hr = MLA_ROPE // 2
    scale = (MLA_NOPE + MLA_ROPE) ** -0.5 * LOG2E
    for hd in range(MLA_HEADS):
        q = qT[hd * HEAD_PAD:(hd + 1) * HEAD_PAD]
        x1, x2 = q[MLA_NOPE:MLA_NOPE + hr], q[MLA_NOPE + hr:MLA_NOPE + 2 * hr]
        o = jnp.concatenate([q[:MLA_NOPE], x1 * c - x2 * s, x2 * c + x1 * s, q[MLA_NOPE + 2 * hr:]], axis=0) * scale
        q_ref[0, hd, 0] = o.astype(BF16)
    vT = _nt(wvT_ref[...], kvn)
    for hd in range(MLA_HEADS):
        k_ref[0, hd] = (kn[:, hd * HEAD_PAD:(hd + 1) * HEAD_PAD] + kr).astype(BF16)
    for hd in range(MLA_HEADS):
        _store_values(v_ref.at[0, hd, 0], vT[hd * MLA_V:(hd + 1) * MLA_V])


def _mem_kv_kernel(mem_ref, g_ref, wk_ref, wvT_ref, mk_ref, mvT_ref):
    h = _rms_rows(mem_ref[0], g_ref[...]).astype(BF16)
    mk_ref[0] = jnp.dot(h, wk_ref[...], preferred_element_type=F32).astype(BF16)
    mvT_ref[0] = _nt(wvT_ref[...], h).astype(BF16)


def _attn_kernel(q_ref, k_ref, v_ref, o_ref, m_sc, acc_sc, s_sc, mc_sc, *, chunks_per_head, n_kv):
    chunk = TOK_TILE
    n_sub = m_sc.shape[0]

    m_sc[...] = jnp.full(m_sc.shape, -jnp.inf, F32)
    acc_sc[...] = jnp.zeros(acc_sc.shape, F32)

    def scores(c, u, slot):
        k = k_ref[0, 0, pl.ds(pl.multiple_of(c * chunk, chunk), chunk), :]
        s = jnp.dot(k, q_ref[0, u // chunks_per_head, u % chunks_per_head],
                    preferred_element_type=F32)
        s_sc[slot, u] = s
        mc_sc[slot, u] = jnp.max(s, axis=0, keepdims=True)

    def softmax_values(c, u, slot):
        m_prev = m_sc[u]
        m_cur = jnp.maximum(m_prev, mc_sc[slot, u])
        alpha = jnp.exp2(m_prev - m_cur)
        p = jnp.exp2(s_sc[slot, u] - m_cur)
        acc_sc[u] = alpha * acc_sc[u] + jnp.dot(v_ref[0, 0, c], p.astype(BF16), preferred_element_type=F32)
        m_sc[u] = m_cur

    def stage(c_scores, c_values):
        for u in range(n_sub):
            if c_scores is not None:
                scores(c_scores, u, c_scores % 2)
            if c_values is not None:
                softmax_values(c_values, u, c_values % 2)

    per_body = ATT_CHUNKS_PER_BODY if (n_kv - 1) // ATT_CHUNKS_PER_BODY >= 2 else 2
    n_bodies = (n_kv - 1) // per_body
    stage(0, None)

    def body(i, carry):
        for t in range(per_body):
            for u in range(n_sub):
                scores(per_body * i + t + 1, u, (t + 1) % 2)
                softmax_values(per_body * i + t, u, t % 2)
        return carry

    lax.fori_loop(0, n_bodies, body, 0)
    for c in range(per_body * n_bodies, n_kv - 1):
        stage(c + 1, c)
    stage(None, n_kv - 1)

    for u in range(n_sub):
        o_ref[0, u // chunks_per_head, u % chunks_per_head] = (
            acc_sc[u, 0:HEAD_PAD] / acc_sc[u, HEAD_PAD:HEAD_PAD + 1]).astype(BF16)


def _attention(qT, k, vT, group):
    b, h, n_kv, hd, chunk = qT.shape
    hkv = k.shape[1]
    s = n_kv * chunk
    assert h == hkv * group and chunk == TOK_TILE
    chunks_per_head = min(ATT_Q_COLS // group, s) // chunk
    n_sub = group * chunks_per_head
    kern = functools.partial(_attn_kernel, chunks_per_head=chunks_per_head, n_kv=n_kv)
    q_spec = pl.BlockSpec((1, group, chunks_per_head, hd, chunk), lambda bi, j, qi: (bi, j, qi, 0, 0))
    return pl.pallas_call(
        kern,
        grid=(b, hkv, n_kv // chunks_per_head),
        in_specs=[
            q_spec,
            pl.BlockSpec((1, 1, s, hd), lambda bi, j, qi: (bi, j, 0, 0)),
            pl.BlockSpec((1, 1, n_kv, V_ROWS, chunk), lambda bi, j, qi: (bi, j, 0, 0, 0)),
        ],
        out_specs=q_spec,
        out_shape=jax.ShapeDtypeStruct(qT.shape, BF16),
        scratch_shapes=[
            pltpu.VMEM((n_sub, 1, chunk), F32),
            pltpu.VMEM((n_sub, V_ROWS, chunk), F32),
            pltpu.VMEM((2, n_sub, chunk, chunk), F32),
            pltpu.VMEM((2, n_sub, 1, chunk), F32),
        ],
        compiler_params=_cparams(3),
        name="attention",
    )(qT, k, vT)


def _mixer_tail_kernel(x_ref, g_ref, oa_ref, ob_ref, mk_ref, mvT_ref, wqcT_ref, wgT_ref, w_ref, y_ref):
    t = x_ref.shape[1]
    x = x_ref[0]
    h = _rms_rows(x, g_ref[...]).astype(BF16)
    scale = MEM_HD ** -0.5 * LOG2E
    piece = 2 * MEM_HD
    out = None
    for p0 in range(0, D_MODEL, piece):
        rows = slice(p0, p0 + piece)
        heads = [slice(p0, p0 + MEM_HD), slice(p0 + MEM_HD, p0 + piece)]
        gate_logits = lambda b: _nt(wgT_ref[b * D_MODEL + p0:b * D_MODEL + p0 + piece, :], h)
        q = [(_nt(wqcT_ref[r, :], h) * scale).astype(BF16) for r in heads]
        s = []
        g = []
        for i, r in enumerate(heads):
            s.append(jnp.dot(mk_ref[0, :, r], q[i], preferred_element_type=F32))
            g.append(gate_logits(i))
        oc = []
        for i, r in enumerate(heads):
            pr = jnp.exp2(s[i] - jnp.max(s[i], axis=0, keepdims=True))
            l = jnp.sum(pr, axis=0, keepdims=True)
            oc.append(jnp.dot(mvT_ref[0, r, :], pr.astype(BF16), preferred_element_type=F32) / l)
            if i == 0:
                g.append(gate_logits(2))
        oa = oa_ref[0].reshape(D_MODEL, t)[rows].astype(F32)
        ob = ob_ref[0].reshape(D_MODEL, t)[rows].astype(F32)
        merged = (jax.nn.sigmoid(g[0]) * oa + jax.nn.sigmoid(g[1]) * ob
                  + jax.nn.sigmoid(g[2]) * jnp.concatenate(oc, axis=0)).astype(BF16)
        part = lax.dot_general(merged, w_ref[rows, :], (((0,), (0,)), ((), ())), preferred_element_type=F32)
        out = part if out is None else out + part
    y_ref[0] = x + out


def _first_index_of_max(v, vmax, iota, n):
    return jnp.min(jnp.where(v == vmax, iota, n), axis=0, keepdims=True)


def _router_kernel(x_ref, g_ref, wT_ref, b_ref, ri_ref, rw_ref, cnt_ref, carry_sc, before_sc):
    step = pl.program_id(0)
    t = x_ref.shape[0]

    @pl.when(step == 0)
    def _():
        carry_sc[...] = jnp.zeros(carry_sc.shape, F32)
        before = lax.broadcasted_iota(jnp.int32, (t, t), 0) < lax.broadcasted_iota(jnp.int32, (t, t), 1)
        before_sc[...] = jnp.where(before, 1.0, 0.0).astype(BF16)

    h = _rms_rows(x_ref[...], g_ref[...])
    h_hi = h.astype(BF16)
    h_lo = (h - h_hi.astype(F32)).astype(BF16)
    w_hi, w_lo = wT_ref[0], wT_ref[1]
    logits = _nt(w_hi, h_hi) + _nt(w_hi, h_lo) + _nt(w_lo, h_hi) + b_ref[...]
    iota_g = lax.broadcasted_iota(jnp.int32, (N_GROUPS, t), 0)
    gl = logits[0:N_GROUPS]
    gmax = jnp.max(gl, axis=0, keepdims=True)
    gidx = _first_index_of_max(gl, gmax, iota_g, N_GROUPS)
    gw = 1.0 / jnp.sum(jnp.exp(gl - gmax), axis=0, keepdims=True)
    sel = jnp.zeros((EXPERTS_PER_GROUP, t), F32)
    for g in range(N_GROUPS):
        r0 = N_GROUPS + g * EXPERTS_PER_GROUP
        sel = sel + jnp.where(gidx == g, logits[r0:r0 + EXPERTS_PER_GROUP], 0.0)
    v1 = jnp.max(sel, axis=0, keepdims=True)
    i1 = _first_index_of_max(sel, v1, iota_g, EXPERTS_PER_GROUP)
    sel2 = jnp.where(iota_g == i1, -jnp.inf, sel)
    v2 = jnp.max(sel2, axis=0, keepdims=True)
    i2 = _first_index_of_max(sel2, v2, iota_g, EXPERTS_PER_GROUP)
    e = jnp.exp(v2 - v1)
    w1 = gw / (1.0 + e)
    w2 = gw * e / (1.0 + e)
    e1 = gidx * EXPERTS_PER_GROUP + i1
    e2 = gidx * EXPERTS_PER_GROUP + i2

    iota_e = lax.broadcasted_iota(jnp.int32, (N_EXPERTS, t), 0)
    hit1 = iota_e == e1
    hit2 = iota_e == e2
    hits = jnp.where(hit1 | hit2, 1.0, 0.0)
    prefix = jnp.dot(hits.astype(BF16), before_sc[...],
                     preferred_element_type=F32) + carry_sc[...]
    r1 = jnp.sum(jnp.where(hit1, prefix, 0.0), axis=0, keepdims=True)
    r2 = jnp.sum(jnp.where(hit2, prefix, 0.0), axis=0, keepdims=True)
    carry_sc[...] = carry_sc[...] + jnp.sum(hits, axis=1, keepdims=True)

    ri_ref[...] = jnp.zeros(ri_ref.shape, jnp.int32)
    ri_ref[0:1, :] = e1
    ri_ref[1:2, :] = e2
    ri_ref[2:3, :] = r1.astype(jnp.int32)
    ri_ref[3:4, :] = r2.astype(jnp.int32)
    rw_ref[...] = jnp.zeros(rw_ref.shape, F32)
    rw_ref[0:1, :] = w1
    rw_ref[1:2, :] = w2
    cnt_ref[...] = jnp.broadcast_to(carry_sc[...], cnt_ref.shape)


def _dest_kernel(cnt_ref, ri_ref, dest_ref, blk_ref, nused_ref, padinfo_ref, *, n_blocks_padded):
    cnt = cnt_ref[...]
    pad = jnp.floor((cnt + (MOE_ROWS - 1)) * (1.0 / MOE_ROWS)) * MOE_ROWS
    lower = jnp.where(lax.broadcasted_iota(jnp.int32, (N_EXPERTS, N_EXPERTS), 1)
                      <= lax.broadcasted_iota(jnp.int32, (N_EXPERTS, N_EXPERTS), 0), 1.0, 0.0)
    pad_end = jnp.dot(lower, pad, precision=lax.Precision.HIGHEST, preferred_element_type=F32)
    pad_start = (pad_end - pad)[:, 0:1]
    t = ri_ref.shape[1]
    iota_e = lax.broadcasted_iota(jnp.int32, (N_EXPERTS, t), 0)
    dest_ref[...] = jnp.zeros(dest_ref.shape, jnp.int32)
    for kk in range(2):
        start = jnp.sum(jnp.where(iota_e == ri_ref[kk:kk + 1, :], pad_start, 0.0), axis=0, keepdims=True)
        dest_ref[kk:kk + 1, :] = (start.astype(jnp.int32) + ri_ref[2 + kk:3 + kk, :]) * ROW_SUBLANES
    blk_start = (lax.broadcasted_iota(jnp.int32, (N_EXPERTS, n_blocks_padded), 1) * MOE_ROWS).astype(F32)
    owner = jnp.sum(jnp.where(pad_end[:, 0:1] <= blk_start, 1.0, 0.0), axis=0, keepdims=True)
    blk_ref[...] = jnp.broadcast_to(jnp.minimum(owner, N_EXPERTS - 1.0).astype(jnp.int32), blk_ref.shape)
    nused_ref[...] = jnp.broadcast_to((pad_end[N_EXPERTS - 1:N_EXPERTS, :] * (1.0 / MOE_ROWS)).astype(jnp.int32),
                                      nused_ref.shape)
    diag = (lax.broadcasted_iota(jnp.int32, (N_EXPERTS, 128), 0) == lax.broadcasted_iota(jnp.int32, (N_EXPERTS, 128), 1))
    first_pad = jnp.sum(jnp.where(diag, (pad_end - pad + cnt) * ROW_SUBLANES, 0.0), axis=0, keepdims=True)
    n_pad = jnp.sum(jnp.where(diag, pad - cnt, 0.0), axis=0, keepdims=True)
    padinfo_ref[...] = jnp.zeros(padinfo_ref.shape, jnp.int32)
    padinfo_ref[0:1, :] = first_pad.astype(jnp.int32)
    padinfo_ref[1:2, :] = n_pad.astype(jnp.int32)
    padinfo_ref[2:3, :] = (pad_end[N_EXPERTS - 1:N_EXPERTS, :] * (1.0 / MOE_ROWS)).astype(jnp.int32)


def _to_row_tiles(ref, x):
    n = x.shape[0]
    for c in range(ROW_SUBLANES):
        ref[pl.ds(c, n, stride=ROW_SUBLANES), :] = x[:, c * 128:(c + 1) * 128]


def _from_row_tiles(ref, n):
    return jnp.concatenate([ref[pl.ds(c, n, stride=ROW_SUBLANES), :] for c in range(ROW_SUBLANES)], axis=1)


def _tile_copy(src_ref, src_row8, dst_ref, dst_row8, sem):
    return pltpu.make_async_copy(src_ref.at[pl.ds(pl.multiple_of(src_row8, ROW_SUBLANES), ROW_SUBLANES)],
                                 dst_ref.at[pl.ds(pl.multiple_of(dst_row8, ROW_SUBLANES), ROW_SUBLANES)], sem)


def _dispatch_kernel(dest_ref, padinfo_ref, x_ref, g_ref, xs_ref, h_sc, zero_sc, sem, blk_sem, *, tile,
                     experts_per_step, n_blocks):
    step = pl.program_id(0)
    _to_row_tiles(h_sc, _rms_rows(x_ref[...], g_ref[...]))
    zero_sc[...] = jnp.zeros(zero_sc.shape, F32)

    def block_copy(b):
        rows = MOE_ROWS * ROW_SUBLANES
        return pltpu.make_async_copy(zero_sc, xs_ref.at[pl.ds(pl.multiple_of(b * rows, rows), rows)], blk_sem)

    for j in range(experts_per_step):
        b = padinfo_ref[2, 0] + step * experts_per_step + j

        @pl.when(b < n_blocks)
        def _():
            block_copy(b).start(priority=1)

    def issue(t, carry):
        _tile_copy(h_sc, t * ROW_SUBLANES, xs_ref, dest_ref[0, 0, t], sem).start()
        _tile_copy(h_sc, t * ROW_SUBLANES, xs_ref, dest_ref[0, 0, tile + t], sem).start(priority=1)
        return carry

    def drain(t, carry):
        _tile_copy(h_sc, 0, xs_ref, 0, sem).wait()
        _tile_copy(h_sc, 0, xs_ref, 0, sem).wait()
        return carry

    lax.fori_loop(0, tile, issue, 0, unroll=DMA_UNROLL)

    n_zero = jnp.int32(0)
    for j in range(experts_per_step):
        e = step * experts_per_step + j
        e_c = jnp.minimum(e, N_EXPERTS - 1)
        first = padinfo_ref[0, e_c]
        count = jnp.where(e < N_EXPERTS, padinfo_ref[1, e_c], 0)

        def zero_row(r, carry, first=first):
            _tile_copy(zero_sc, 0, xs_ref, first + r * ROW_SUBLANES, sem).start()
            return carry

        lax.fori_loop(0, count, zero_row, 0)
        n_zero = n_zero + count

    lax.fori_loop(0, tile, drain, 0, unroll=DMA_UNROLL)
    lax.fori_loop(0, n_zero, lambda r, carry: (_tile_copy(zero_sc, 0, xs_ref, 0, sem).wait(), carry)[1], 0)
    for j in range(experts_per_step):
        b = padinfo_ref[2, 0] + step * experts_per_step + j

        @pl.when(b < n_blocks)
        def _():
            block_copy(b).wait()


def _expert_kernel(blk_ref, nused_ref, xs_ref, wgu_ref, wd_ref, ys_ref):
    b = pl.program_id(0)

    @pl.when(b < nused_ref[0])
    def _():
        x = _from_row_tiles(xs_ref, MOE_ROWS).astype(BF16)
        half = MOE_ROWS // 2
        gus = [jnp.dot(x[r0:r0 + half], wgu_ref[0], preferred_element_type=F32) for r0 in (0, half)]
        ys = []
        for gu in gus:
            a, u = gu[:, :D_EXPERT], gu[:, D_EXPERT:]
            hidden = (a * jax.nn.sigmoid(a) * u).astype(BF16)
            ys.append(jnp.dot(hidden, wd_ref[0], preferred_element_type=F32))
        _to_row_tiles(ys_ref, jnp.concatenate(ys, axis=0))

    @pl.when(b >= nused_ref[0])
    def _():
        ys_ref[...] = jnp.zeros(ys_ref.shape, F32)


def _combine_kernel(dest_ref, dest_next_ref, ys_ref, x_ref, w_ref, g_ref, y_ref, buf, sem, *, tile, final_norm):
    step = pl.program_id(0)
    slot = lax.rem(step, 2)

    def issue(d_ref, slot_):
        def body(t, carry):
            _tile_copy(ys_ref, d_ref[0, 0, t], buf.at[slot_, 0], t * ROW_SUBLANES, sem.at[slot_]).start()
            _tile_copy(ys_ref, d_ref[0, 0, tile + t], buf.at[slot_, 1], t * ROW_SUBLANES,
                       sem.at[slot_]).start(priority=1)
            return carry
        lax.fori_loop(0, tile, body, 0, unroll=DMA_UNROLL)

    @pl.when(step == 0)
    def _():
        issue(dest_ref, 0)

    @pl.when(step + 1 < pl.num_programs(0))
    def _():
        issue(dest_next_ref, 1 - slot)

    def drain(t, carry):
        _tile_copy(ys_ref, 0, buf.at[slot, 0], 0, sem.at[slot]).wait()
        _tile_copy(ys_ref, 0, buf.at[slot, 1], 0, sem.at[slot]).wait()
        return carry

    lax.fori_loop(0, tile, drain, 0, unroll=DMA_UNROLL)
    w = w_ref[...]
    y = x_ref[...] + (w[:, 0:1] * _from_row_tiles(buf.at[slot, 0], tile)
                      + w[:, 1:2] * _from_row_tiles(buf.at[slot, 1], tile))
    if final_norm:
        y = _rms_rows(y, g_ref[...])
    y_ref[...] = y


def _moe(x, g_ffn, wrT, br, wgu, wd, g_final, final_norm):
    n_tok = x.shape[0]
    tile = min(TOK_TILE, n_tok)
    n_tiles = n_tok // tile
    n_blocks = (2 * n_tok + MOE_ROWS - 1) // MOE_ROWS + N_EXPERTS
    n_blocks_padded = (n_blocks + 127) // 128 * 128
    n_rows = n_blocks * MOE_ROWS

    ri, rw, cnt = pl.pallas_call(
        _router_kernel,
        grid=(n_tiles,),
        in_specs=[
            pl.BlockSpec((tile, D_MODEL), lambda i: (i, 0)),
            _const_spec((1, D_MODEL)),
            _const_spec((2, ROUTER_LANES, D_MODEL)),
            _const_spec((ROUTER_LANES, 1)),
        ],
        out_specs=[
            pl.BlockSpec((8, tile), lambda i: (0, i)),
            pl.BlockSpec((8, tile), lambda i: (0, i)),
            _const_spec((N_EXPERTS, 128)),
        ],
        out_shape=[
            jax.ShapeDtypeStruct((8, n_tok), jnp.int32),
            jax.ShapeDtypeStruct((8, n_tok), F32),
            jax.ShapeDtypeStruct((N_EXPERTS, 128), F32),
        ],
        scratch_shapes=[pltpu.VMEM((N_EXPERTS, 1), F32), pltpu.VMEM((tile, tile), BF16)],
        compiler_params=_cparams(1),
        name="moe_router",
    )(x, g_ffn, wrT, br)

    dest, blk, nused, padinfo = pl.pallas_call(
        functools.partial(_dest_kernel, n_blocks_padded=n_blocks_padded),
        grid=(n_tiles,),
        in_specs=[_const_spec((N_EXPERTS, 128)), pl.BlockSpec((8, tile), lambda i: (0, i))],
        out_specs=[
            pl.BlockSpec((8, tile), lambda i: (0, i)),
            _const_spec((8, n_blocks_padded)),
            _const_spec((8, 128)),
            _const_spec((8, 128)),
        ],
        out_shape=[
            jax.ShapeDtypeStruct((8, n_tok), jnp.int32),
            jax.ShapeDtypeStruct((8, n_blocks_padded), jnp.int32),
            jax.ShapeDtypeStruct((8, 128), jnp.int32),
            jax.ShapeDtypeStruct((8, 128), jnp.int32),
        ],
        compiler_params=_cparams(1),
        name="moe_dest",
    )(cnt, ri)

    dest_tiles = dest[0:2].reshape(2, n_tiles, tile).transpose(1, 0, 2).reshape(n_tiles, 1, 2 * tile)
    dest_spec = pl.BlockSpec((1, 1, 2 * tile), lambda i: (i, 0, 0), memory_space=pltpu.SMEM)
    any_spec = pl.BlockSpec(memory_space=pl.ANY)

    xs = pl.pallas_call(
        functools.partial(_dispatch_kernel, tile=tile, experts_per_step=pl.cdiv(N_EXPERTS, n_tiles),
                          n_blocks=n_blocks),
        grid=(n_tiles,),
        in_specs=[dest_spec, pl.BlockSpec((8, 128), lambda i: (0, 0), memory_space=pltpu.SMEM),
                  pl.BlockSpec((tile, D_MODEL), lambda i: (i, 0)), _const_spec((1, D_MODEL))],
        out_specs=any_spec,
        out_shape=jax.ShapeDtypeStruct((n_rows * ROW_SUBLANES, 128), F32),
        scratch_shapes=[pltpu.VMEM((tile * ROW_SUBLANES, 128), F32),
                        pltpu.VMEM((MOE_ROWS * ROW_SUBLANES, 128), F32),
                        pltpu.SemaphoreType.DMA(()), pltpu.SemaphoreType.DMA(())],
        compiler_params=_cparams(1),
        name="moe_dispatch",
    )(dest_tiles, padinfo, x, g_ffn)

    blk_rows = MOE_ROWS * ROW_SUBLANES
    ys = pl.pallas_call(
        _expert_kernel,
        grid_spec=pltpu.PrefetchScalarGridSpec(
            num_scalar_prefetch=2,
            grid=(n_blocks,),
            in_specs=[
                pl.BlockSpec((blk_rows, 128), lambda b, blk, nu: (b, 0)),
                pl.BlockSpec((1, D_MODEL, 2 * D_EXPERT), lambda b, blk, nu: (blk[b], 0, 0)),
                pl.BlockSpec((1, D_EXPERT, D_MODEL), lambda b, blk, nu: (blk[b], 0, 0)),
            ],
            out_specs=pl.BlockSpec((blk_rows, 128), lambda b, blk, nu: (b, 0)),
        ),
        out_shape=jax.ShapeDtypeStruct((n_rows * ROW_SUBLANES, 128), F32),
        compiler_params=_cparams(1),
        name="moe_experts",
    )(blk[0, :n_blocks], nused[0, :1], xs, wgu, wd)

    ctile = min(256, n_tok)
    c_tiles = n_tok // ctile
    dest_c = dest[0:2].reshape(2, c_tiles, ctile).transpose(1, 0, 2).reshape(c_tiles, 1, 2 * ctile)
    return pl.pallas_call(
        functools.partial(_combine_kernel, tile=ctile, final_norm=final_norm),
        grid=(c_tiles,),
        in_specs=[
            pl.BlockSpec((1, 1, 2 * ctile), lambda i: (i, 0, 0), memory_space=pltpu.SMEM),
            pl.BlockSpec((1, 1, 2 * ctile), lambda i: (jnp.minimum(i + 1, c_tiles - 1), 0, 0),
                         memory_space=pltpu.SMEM),
            any_spec,
            pl.BlockSpec((ctile, D_MODEL), lambda i: (i, 0)),
            pl.BlockSpec((ctile, 2), lambda i: (i, 0)),
            _const_spec((1, D_MODEL)),
        ],
        out_specs=pl.BlockSpec((ctile, D_MODEL), lambda i: (i, 0)),
        out_shape=jax.ShapeDtypeStruct((n_tok, D_MODEL), F32),
        scratch_shapes=[pltpu.VMEM((2, 2, ctile * ROW_SUBLANES, 128), F32), pltpu.SemaphoreType.DMA((2,))],
        compiler_params=_cparams(1),
        name="moe_combine",
    )(dest_c, dest_c, ys, x, rw[0:2].T, g_final)


def _rope_tables(n_tokens):
    rows = n_tokens // GRID_W
    row_idx = jnp.repeat(jnp.arange(rows, dtype=F32), GRID_W)
    col_idx = jnp.tile(jnp.arange(GRID_W, dtype=F32), rows)

    def angles(rot_dim):
        n_freq = rot_dim // 4
        inv_freq = ROPE_THETA ** (-jnp.arange(n_freq, dtype=F32) / n_freq)
        return jnp.concatenate([row_idx[:, None] * inv_freq, col_idx[:, None] * inv_freq], axis=-1)

    ag, am = angles(GQA_HD), angles(MLA_ROPE)
    cg, sg, cm, sm = jnp.cos(ag), jnp.sin(ag), jnp.cos(am), jnp.sin(am)
    hr = MLA_ROPE // 2
    zl = jnp.zeros((n_tokens, MLA_NOPE), F32)
    zr = jnp.zeros((n_tokens, HEAD_PAD - MLA_NOPE - MLA_ROPE), F32)
    del hr
    return dict(
        cT_g=cg.T, sT_g=sg.T,
        cF_g=jnp.concatenate([cg, cg], axis=1), sF_g=jnp.concatenate([-sg, sg], axis=1),
        cT_m=cm.T, sT_m=sm.T,
        cP_m=jnp.concatenate([zl, cm, cm, zr], axis=1), sP_m=jnp.concatenate([zl, -sm, sm, zr], axis=1),
    )


def _hi_lo(w):
    hi = w.astype(BF16)
    return jnp.stack([hi, (w - hi.astype(F32)).astype(BF16)])


def _layer_params(l, norm_mix, norm_mem, w_in, gqa_q_norm, gqa_k_norm, mla_q_a_norm, mla_kv_a_norm, mla_w_qb,
                  mla_w_kvb, mem_w_kv, w_out, norm_ffn, w_group, b_group, w_expert, b_expert, w_gate_up, w_down):
    splits = []
    off = 0
    for width in (GQA_HEADS * GQA_HD, GQA_KV_HEADS * GQA_HD, GQA_KV_HEADS * GQA_HD, MLA_Q_LORA, MLA_KV_LORA,
                  MLA_ROPE, MEM_HEADS * MEM_HD, N_BRANCH * D_MODEL):
        splits.append(w_in[l][:, off:off + width])
        off += width
    wq_g, wk_g, wv_g, wq_a, wkv_a, wk_r, wq_c, w_gate = splits
    hr = MLA_ROPE // 2
    pad_l = jnp.zeros((D_MODEL, MLA_NOPE), F32)
    pad_r = jnp.zeros((D_MODEL, HEAD_PAD - MLA_NOPE - MLA_ROPE), F32)
    wk_r_swapped = jnp.concatenate([wk_r[:, hr:], wk_r[:, :hr]], axis=1)
    wkr2 = jnp.concatenate([pad_l, wk_r, pad_r, pad_l, wk_r_swapped, pad_r], axis=1)
    qk = MLA_NOPE + MLA_ROPE
    wqb = jnp.pad(mla_w_qb[l].reshape(MLA_Q_LORA, MLA_HEADS, qk), ((0, 0), (0, 0), (0, HEAD_PAD - qk)))
    wkvb = mla_w_kvb[l].reshape(MLA_KV_LORA, MLA_HEADS, MLA_NOPE + MLA_V)
    wkn = jnp.pad(wkvb[:, :, :MLA_NOPE], ((0, 0), (0, 0), (0, HEAD_PAD - MLA_NOPE)))
    wv_m = wkvb[:, :, MLA_NOPE:]
    n_router = N_GROUPS + N_EXPERTS
    return dict(
        g_mix=norm_mix[l][None, :], g_mem=norm_mem[l][None, :], g_ffn=norm_ffn[l][None, :],
        wqT_g=wq_g.T.astype(BF16), wk_g=wk_g.astype(BF16), wvT_g=wv_g.T.astype(BF16),
        gq=gqa_q_norm[l][:, None], gk=gqa_k_norm[l][None, :],
        wqaT=wq_a.T.astype(BF16), gqa=mla_q_a_norm[l][:, None],
        wqbT=wqb.reshape(MLA_Q_LORA, MLA_HEADS * HEAD_PAD).T.astype(BF16),
        wkva=wkv_a.astype(BF16), gkva=mla_kv_a_norm[l][None, :],
        wkn=wkn.reshape(MLA_KV_LORA, MLA_HEADS * HEAD_PAD).astype(BF16), wkr2=wkr2.astype(BF16),
        wvT_m=wv_m.reshape(MLA_KV_LORA, MLA_HEADS * MLA_V).T.astype(BF16),
        wqcT=wq_c.T.astype(BF16), wgT=w_gate.T.astype(BF16),
        wmk=mem_w_kv[l][:, :MEM_HEADS * MEM_HD].astype(BF16),
        wmvT=mem_w_kv[l][:, MEM_HEADS * MEM_HD:].T.astype(BF16),
        w_out=w_out[l].astype(BF16),
        wrT=_hi_lo(jnp.concatenate([w_group[l].T, w_expert[l].T,
                                    jnp.zeros((ROUTER_LANES - n_router, D_MODEL), F32)], axis=0)),
        br=jnp.concatenate([b_group[l], b_expert[l], jnp.zeros((ROUTER_LANES - n_router,), F32)])[:, None],
        wgu=w_gate_up[l].astype(BF16), wd=w_down[l].astype(BF16),
    )


def _mixer(x, mem, p, tabs):
    b, s, _ = x.shape
    tile = min(TOK_TILE, s)
    n_t = s // tile
    grid = (b, n_t)
    x_spec = pl.BlockSpec((1, tile, D_MODEL), lambda bi, i: (bi, i, 0))
    colT = lambda rows: pl.BlockSpec((rows, tile), lambda bi, i: (0, i))
    rowF = pl.BlockSpec((tile, HEAD_PAD), lambda bi, i: (i, 0))
    headT = lambda heads: pl.BlockSpec((1, heads, 1, HEAD_PAD, tile), lambda bi, i: (bi, 0, i, 0, 0))
    keyF = lambda heads: pl.BlockSpec((1, heads, tile, HEAD_PAD), lambda bi, i: (bi, 0, i, 0))
    valT = lambda heads: pl.BlockSpec((1, heads, 1, V_ROWS, tile), lambda bi, i: (bi, 0, i, 0, 0))

    qg, kg, vg = pl.pallas_call(
        _proj_gqa_kernel,
        grid=grid,
        in_specs=[x_spec, _const_spec((1, D_MODEL)), _const_spec(p["wqT_g"].shape), _const_spec(p["wk_g"].shape),
                  _const_spec(p["wvT_g"].shape), _const_spec((GQA_HD, 1)), _const_spec((1, GQA_HD)),
                  colT(GQA_HD // 2), colT(GQA_HD // 2), rowF, rowF],
        out_specs=[headT(GQA_HEADS), keyF(GQA_KV_HEADS), valT(GQA_KV_HEADS)],
        out_shape=[jax.ShapeDtypeStruct((b, GQA_HEADS, n_t, HEAD_PAD, tile), BF16),
                   jax.ShapeDtypeStruct((b, GQA_KV_HEADS, s, HEAD_PAD), BF16),
                   jax.ShapeDtypeStruct((b, GQA_KV_HEADS, n_t, V_ROWS, tile), BF16)],
        compiler_params=_cparams(2),
        name="proj_gqa",
    )(x, p["g_mix"], p["wqT_g"], p["wk_g"], p["wvT_g"], p["gq"], p["gk"],
      tabs["cT_g"], tabs["sT_g"], tabs["cF_g"], tabs["sF_g"])

    qm, km, vm = pl.pallas_call(
        _proj_mla_kernel,
        grid=grid,
        in_specs=[x_spec, _const_spec((1, D_MODEL)), _const_spec(p["wqaT"].shape), _const_spec((MLA_Q_LORA, 1)),
                  _const_spec(p["wqbT"].shape), _const_spec(p["wkva"].shape), _const_spec((1, MLA_KV_LORA)),
                  _const_spec(p["wkn"].shape), _const_spec(p["wkr2"].shape), _const_spec(p["wvT_m"].shape),
                  colT(MLA_ROPE // 2), colT(MLA_ROPE // 2), rowF, rowF],
        out_specs=[headT(MLA_HEADS), keyF(MLA_HEADS), valT(MLA_HEADS)],
        out_shape=[jax.ShapeDtypeStruct((b, MLA_HEADS, n_t, HEAD_PAD, tile), BF16),
                   jax.ShapeDtypeStruct((b, MLA_HEADS, s, HEAD_PAD), BF16),
                   jax.ShapeDtypeStruct((b, MLA_HEADS, n_t, V_ROWS, tile), BF16)],
        compiler_params=_cparams(2),
        name="proj_mla",
    )(x, p["g_mix"], p["wqaT"], p["gqa"], p["wqbT"], p["wkva"], p["gkva"], p["wkn"], p["wkr2"], p["wvT_m"],
      tabs["cT_m"], tabs["sT_m"], tabs["cP_m"], tabs["sP_m"])

    n_mem = mem.shape[1]
    mk, mvT = pl.pallas_call(
        _mem_kv_kernel,
        grid=(b,),
        in_specs=[pl.BlockSpec((1, n_mem, D_MODEL), lambda bi: (bi, 0, 0)), _const_spec((1, D_MODEL)),
                  _const_spec(p["wmk"].shape), _const_spec(p["wmvT"].shape)],
        out_specs=[pl.BlockSpec((1, n_mem, D_MODEL), lambda bi: (bi, 0, 0)),
                   pl.BlockSpec((1, D_MODEL, n_mem), lambda bi: (bi, 0, 0))],
        out_shape=[jax.ShapeDtypeStruct((b, n_mem, D_MODEL), BF16), jax.ShapeDtypeStruct((b, D_MODEL, n_mem), BF16)],
        compiler_params=_cparams(1),
        name="mem_kv",
    )(mem, p["g_mem"], p["wmk"], p["wmvT"])

    oa = _attention(qg, kg, vg, GQA_HEADS // GQA_KV_HEADS)
    ob = _attention(qm, km, vm, 1)
    return pl.pallas_call(
        _mixer_tail_kernel,
        grid=grid,
        in_specs=[x_spec, _const_spec((1, D_MODEL)), headT(GQA_HEADS), headT(MLA_HEADS),
                  pl.BlockSpec((1, n_mem, D_MODEL), lambda bi, i: (bi, 0, 0)),
                  pl.BlockSpec((1, D_MODEL, n_mem), lambda bi, i: (bi, 0, 0)),
                  _const_spec(p["wqcT"].shape), _const_spec(p["wgT"].shape), _const_spec((D_MODEL, D_MODEL))],
        out_specs=x_spec,
        out_shape=jax.ShapeDtypeStruct(x.shape, F32),
        compiler_params=_cparams(2),
        name="mixer_tail",
    )(x, p["g_mix"], oa, ob, mk, mvT, p["wqcT"], p["wgT"], p["w_out"])


def _trunk(x, mem, layers, g_final):
    b, s, d = x.shape
    tabs = _rope_tables(s)
    for l, p in enumerate(layers):
        x = _mixer(x, mem, p, tabs)
        x = _moe(x.reshape(b * s, d), p["g_ffn"], p["wrT"], p["br"], p["wgu"], p["wd"], g_final,
                 final_norm=(l == len(layers) - 1)).reshape(b, s, d)
    return x


def kernel(x_prompt, x_sample, mem_prompt, mem_sample, norm_mix, norm_mem, w_in, gqa_q_norm, gqa_k_norm, mla_q_a_norm, mla_kv_a_norm, mla_w_qb, mla_w_kvb, mem_w_kv, w_out, norm_ffn, w_group, b_group, w_expert, b_expert, w_gate_up, w_down, norm_final):
    depth = w_in.shape[0]
    layers = [_layer_params(l, norm_mix, norm_mem, w_in, gqa_q_norm, gqa_k_norm, mla_q_a_norm, mla_kv_a_norm,
                            mla_w_qb, mla_w_kvb, mem_w_kv, w_out, norm_ffn, w_group, b_group, w_expert, b_expert,
                            w_gate_up, w_down) for l in range(depth)]
    g_final = norm_final[None, :]
    return (_trunk(x_prompt, mem_prompt, layers, g_final), _trunk(x_sample, mem_sample, layers, g_final))
```

```python
import functools
import math

import jax
import jax.numpy as jnp
from jax import lax
from jax.experimental import pallas as pl
from jax.experimental.pallas import tpu as pltpu

F32 = jnp.float32
BF16 = jnp.bfloat16

D_MODEL = 1024
EPS = 1e-6
GRID_W = 64
ROPE_THETA = 10000.0
GQA_HEADS, GQA_KV_HEADS, GQA_HD = 8, 2, 128
MLA_HEADS, MLA_Q_LORA, MLA_KV_LORA = 8, 384, 256
MLA_NOPE, MLA_ROPE, MLA_V = 64, 32, 128
MEM_HEADS, MEM_HD = 4, 256
N_BRANCH = 3
N_GROUPS, EXPERTS_PER_GROUP, D_EXPERT = 8, 8, 256
N_EXPERTS = N_GROUPS * EXPERTS_PER_GROUP
HEAD_PAD = 128
BF16_SUBLANES = 16
V_ROWS = HEAD_PAD + BF16_SUBLANES
LOG2E = 1.4426950408889634

TOK_TILE = 512
ATT_Q_COLS = 4096
ATT_CHUNKS_PER_BODY = 2
MOE_ROWS = 256
ROUTER_LANES = 128
ROW_SUBLANES = D_MODEL // 128
DMA_UNROLL = 8
VMEM_LIMIT = 56 * 1024 * 1024


def _cparams(n_axes, flags=None):
    return pltpu.CompilerParams(dimension_semantics=("arbitrary",) * n_axes, vmem_limit_bytes=VMEM_LIMIT,
                                flags=flags)


def _nt(a, b):
    return lax.dot_general(a, b, (((1,), (1,)), ((), ())), preferred_element_type=F32)


def _rms_rows(x, g):
    return x * lax.rsqrt(jnp.mean(x * x, axis=-1, keepdims=True) + EPS) * g


def _rms_cols(x, g):
    return x * lax.rsqrt(jnp.mean(x * x, axis=0, keepdims=True) + EPS) * g


def _const_spec(shape):
    return pl.BlockSpec(shape, lambda *_: (0,) * len(shape))


def _store_values(v_ref, vT):
    t = vT.shape[1]
    v_ref[0:HEAD_PAD, :] = vT.astype(BF16)
    ones_row = lax.broadcasted_iota(jnp.int32, (BF16_SUBLANES, t), 0) == 0
    v_ref[HEAD_PAD:V_ROWS, :] = jnp.where(ones_row, 1.0, 0.0).astype(BF16)


def _proj_attn_kernel(x_ref, g_ref, *refs):
    h = _rms_rows(x_ref[0], g_ref[...]).astype(BF16)
    n_g, n_m = 9, 12
    _proj_mla(h, *refs[n_g:n_g + n_m], *refs[n_g + n_m + 3:])
    _proj_gqa(h, *refs[:n_g], *refs[n_g + n_m:n_g + n_m + 3])


def _proj_gqa(h, wqT_ref, wk_ref, wvT_ref, gq_ref, gk_ref, cT_ref, sT_ref, cF_ref, sF_ref, q_ref, k_ref, v_ref):
    c, s, gq = cT_ref[...], sT_ref[...], gq_ref[...]
    half = GQA_HD // 2
    scale = GQA_HD ** -0.5 * LOG2E
    heads_per_dot = 4
    k = jnp.dot(h, wk_ref[...], preferred_element_type=F32)
    q_pieces = []
    for h0 in range(0, GQA_HEADS, heads_per_dot):
        q_pieces.append(_nt(wqT_ref[h0 * GQA_HD:(h0 + heads_per_dot) * GQA_HD, :], h))
        if h0 == 0:
            for j in range(GQA_KV_HEADS):
                kn = _rms_rows(k[:, j * GQA_HD:(j + 1) * GQA_HD], gk_ref[...])
                ko = kn * cF_ref[...] + pltpu.roll(kn, half, 1) * sF_ref[...]
                k_ref[0, j] = ko.astype(BF16)
    vT = _nt(wvT_ref[...], h)
    for piece, qT in enumerate(q_pieces):
        for hd in range(heads_per_dot):
            qn = _rms_cols(qT[hd * GQA_HD:(hd + 1) * GQA_HD], gq)
            x1, x2 = qn[:half], qn[half:]
            o = jnp.concatenate([x1 * c - x2 * s, x2 * c + x1 * s], axis=0) * scale
            q_ref[0, piece * heads_per_dot + hd, 0] = o.astype(BF16)
    for j in range(GQA_KV_HEADS):
        _store_values(v_ref.at[0, j, 0], vT[j * GQA_HD:(j + 1) * GQA_HD])


def _proj_mla(h, wqaT_ref, gqa_ref, wqbT_ref, wkva_ref, gkva_ref, wkn_ref, wkr_ref, wvT_ref,
              cT_ref, sT_ref, cP_ref, sP_ref, q_ref, k_ref, v_ref):
    qa_raw = _nt(wqaT_ref[...], h)
    kva = jnp.dot(h, wkva_ref[...], preferred_element_type=F32)
    kr2 = jnp.dot(h, wkr_ref[...], preferred_element_type=F32)
    qa = _rms_cols(qa_raw, gqa_ref[...]).astype(BF16)
    qT = jnp.dot(wqbT_ref[...], qa, preferred_element_type=F32)
    kvn = _rms_rows(kva, gkva_ref[...]).astype(BF16)
    kn = jnp.dot(kvn, wkn_ref[...], preferred_element_type=F32)
    kr = kr2[:, :HEAD_PAD] * cP_ref[...] + kr2[:, HEAD_PAD:] * sP_ref[...]
    c, s = cT_ref[...], sT_ref[...]
    hr = MLA_ROPE // 2
    scale = (MLA_NOPE + MLA_ROPE) ** -0.5 * LOG2E
    for hd in range(MLA_HEADS):
        q = qT[hd * HEAD_PAD:(hd + 1) * HEAD_PAD]
        x1, x2 = q[MLA_NOPE:MLA_NOPE + hr], q[MLA_NOPE + hr:MLA_NOPE + 2 * hr]
        o = jnp.concatenate([q[:MLA_NOPE], x1 * c - x2 * s, x2 * c + x1 * s, q[MLA_NOPE + 2 * hr:]], axis=0) * scale
        q_ref[0, hd, 0] = o.astype(BF16)
    vT = _nt(wvT_ref[...], kvn)
    for hd in range(MLA_HEADS):
        k_ref[0, hd] = (kn[:, hd * HEAD_PAD:(hd + 1) * HEAD_PAD] + kr).astype(BF16)
    for hd in range(MLA_HEADS):
        _store_values(v_ref.at[0, hd, 0], vT[hd * MLA_V:(hd + 1) * MLA_V])


def _mem_kv_kernel(mem_ref, g_ref, wk_ref, wvT_ref, mk_ref, mvT_ref):
    h = _rms_rows(mem_ref[0], g_ref[...]).astype(BF16)
    mk_ref[0] = jnp.dot(h, wk_ref[...], preferred_element_type=F32).astype(BF16)
    mvT_ref[0] = _nt(wvT_ref[...], h).astype(BF16)


def _attn_kernel(q_ref, k_ref, v_ref, o_ref, m_sc, acc_sc, s_sc, mc_sc, *, chunks_per_head, n_kv):
    chunk = TOK_TILE
    n_sub = m_sc.shape[0]

    m_sc[...] = jnp.full(m_sc.shape, -jnp.inf, F32)
    acc_sc[...] = jnp.zeros(acc_sc.shape, F32)

    def scores(c, u, slot):
        k = k_ref[0, 0, pl.ds(pl.multiple_of(c * chunk, chunk), chunk), :]
        s = jnp.dot(k, q_ref[0, u // chunks_per_head, u % chunks_per_head],
                    preferred_element_type=F32)
        s_sc[slot, u] = s
        mc_sc[slot, u] = jnp.max(s, axis=0, keepdims=True)

    def softmax_values(c, u, slot):
        m_prev = m_sc[u]
        m_cur = jnp.maximum(m_prev, mc_sc[slot, u])
        alpha = jnp.exp2(m_prev - m_cur)
        p = jnp.exp2(s_sc[slot, u] - m_cur)
        acc_sc[u] = alpha * acc_sc[u] + jnp.dot(v_ref[0, 0, c], p.astype(BF16), preferred_element_type=F32)
        m_sc[u] = m_cur

    def stage(c_scores, c_values):
        for u in range(n_sub):
            if c_scores is not None:
                scores(c_scores, u, c_scores % 2)
            if c_values is not None:
                softmax_values(c_values, u, c_values % 2)

    per_body = ATT_CHUNKS_PER_BODY if (n_kv - 1) // ATT_CHUNKS_PER_BODY >= 2 else 2
    n_bodies = (n_kv - 1) // per_body
    stage(0, None)

    def body(i, carry):
        for t in range(per_body):
            for u in range(n_sub):
                scores(per_body * i + t + 1, u, (t + 1) % 2)
                softmax_values(per_body * i + t, u, t % 2)
        return carry

    lax.fori_loop(0, n_bodies, body, 0)
    for c in range(per_body * n_bodies, n_kv - 1):
        stage(c + 1, c)
    stage(None, n_kv - 1)

    for u in range(n_sub):
        o_ref[0, u // chunks_per_head, u % chunks_per_head] = (
            acc_sc[u, 0:HEAD_PAD] / acc_sc[u, HEAD_PAD:HEAD_PAD + 1]).astype(BF16)


def _attention(qT, k, vT, group):
    b, h, n_kv, hd, chunk = qT.shape
    hkv = k.shape[1]
    s = n_kv * chunk
    assert h == hkv * group and chunk == TOK_TILE
    chunks_per_head = min(ATT_Q_COLS // group, s) // chunk
    n_sub = group * chunks_per_head
    kern = functools.partial(_attn_kernel, chunks_per_head=chunks_per_head, n_kv=n_kv)
    q_spec = pl.BlockSpec((1, group, chunks_per_head, hd, chunk), lambda bi, j, qi: (bi, j, qi, 0, 0))
    return pl.pallas_call(
        kern,
        grid=(b, hkv, n_kv // chunks_per_head),
        in_specs=[
            q_spec,
            pl.BlockSpec((1, 1, s, hd), lambda bi, j, qi: (bi, j, 0, 0)),
            pl.BlockSpec((1, 1, n_kv, V_ROWS, chunk), lambda bi, j, qi: (bi, j, 0, 0, 0)),
        ],
        out_specs=q_spec,
        out_shape=jax.ShapeDtypeStruct(qT.shape, BF16),
        scratch_shapes=[
            pltpu.VMEM((n_sub, 1, chunk), F32),
            pltpu.VMEM((n_sub, V_ROWS, chunk), F32),
            pltpu.VMEM((2, n_sub, chunk, chunk), F32),
            pltpu.VMEM((2, n_sub, 1, chunk), F32),
        ],
        compiler_params=_cparams(3),
        name="attention",
    )(qT, k, vT)


def _mixer_tail_kernel(x_ref, g_ref, oa_ref, ob_ref, mk_ref, mvT_ref, wqcT_ref, wgT_ref, w_ref, y_ref):
    t = x_ref.shape[1]
    x = x_ref[0]
    h = _rms_rows(x, g_ref[...]).astype(BF16)
    scale = MEM_HD ** -0.5 * LOG2E
    piece = 2 * MEM_HD
    out = None
    for p0 in range(0, D_MODEL, piece):
        rows = slice(p0, p0 + piece)
        heads = [slice(p0, p0 + MEM_HD), slice(p0 + MEM_HD, p0 + piece)]
        gate_logits = lambda b: _nt(wgT_ref[b * D_MODEL + p0:b * D_MODEL + p0 + piece, :], h)
        q = [(_nt(wqcT_ref[r, :], h) * scale).astype(BF16) for r in heads]
        s = []
        g = []
        for i, r in enumerate(heads):
            s.append(jnp.dot(mk_ref[0, :, r], q[i], preferred_element_type=F32))
            g.append(gate_logits(i))
        oc = []
        for i, r in enumerate(heads):
            pr = jnp.exp2(s[i] - jnp.max(s[i], axis=0, keepdims=True))
            l = jnp.sum(pr, axis=0, keepdims=True)
            oc.append(jnp.dot(mvT_ref[0, r, :], pr.astype(BF16), preferred_element_type=F32) / l)
            if i == 0:
                g.append(gate_logits(2))
        oa = oa_ref[0].reshape(D_MODEL, t)[rows].astype(F32)
        ob = ob_ref[0].reshape(D_MODEL, t)[rows].astype(F32)
        merged = (jax.nn.sigmoid(g[0]) * oa + jax.nn.sigmoid(g[1]) * ob
                  + jax.nn.sigmoid(g[2]) * jnp.concatenate(oc, axis=0)).astype(BF16)
        part = lax.dot_general(merged, w_ref[rows, :], (((0,), (0,)), ((), ())), preferred_element_type=F32)
        out = part if out is None else out + part
    y_ref[0] = x + out


def _first_index_of_max(v, vmax, iota, n):
    return jnp.min(jnp.where(v == vmax, iota, n), axis=0, keepdims=True)


def _router_kernel(x_ref, g_ref, wT_ref, b_ref, ri_ref, rw_ref, cnt_ref, carry_sc, before_sc):
    step = pl.program_id(0)
    t = x_ref.shape[0]

    @pl.when(step == 0)
    def _():
        carry_sc[...] = jnp.zeros(carry_sc.shape, F32)
        before = lax.broadcasted_iota(jnp.int32, (t, t), 0) < lax.broadcasted_iota(jnp.int32, (t, t), 1)
        before_sc[...] = jnp.where(before, 1.0, 0.0).astype(BF16)

    h = _rms_rows(x_ref[...], g_ref[...])
    h_hi = h.astype(BF16)
    h_lo = (h - h_hi.astype(F32)).astype(BF16)
    w_hi, w_lo = wT_ref[0], wT_ref[1]
    logits = _nt(w_hi, h_hi) + _nt(w_hi, h_lo) + _nt(w_lo, h_hi) + b_ref[...]
    iota_g = lax.broadcasted_iota(jnp.int32, (N_GROUPS, t), 0)
    gl = logits[0:N_GROUPS]
    gmax = jnp.max(gl, axis=0, keepdims=True)
    gidx = _first_index_of_max(gl, gmax, iota_g, N_GROUPS)
    gw = 1.0 / jnp.sum(jnp.exp(gl - gmax), axis=0, keepdims=True)
    sel = jnp.zeros((EXPERTS_PER_GROUP, t), F32)
    for g in range(N_GROUPS):
        r0 = N_GROUPS + g * EXPERTS_PER_GROUP
        sel = sel + jnp.where(gidx == g, logits[r0:r0 + EXPERTS_PER_GROUP], 0.0)
    v1 = jnp.max(sel, axis=0, keepdims=True)
    i1 = _first_index_of_max(sel, v1, iota_g, EXPERTS_PER_GROUP)
    sel2 = jnp.where(iota_g == i1, -jnp.inf, sel)
    v2 = jnp.max(sel2, axis=0, keepdims=True)
    i2 = _first_index_of_max(sel2, v2, iota_g, EXPERTS_PER_GROUP)
    e = jnp.exp(v2 - v1)
    w1 = gw / (1.0 + e)
    w2 = gw * e / (1.0 + e)
    e1 = gidx * EXPERTS_PER_GROUP + i1
    e2 = gidx * EXPERTS_PER_GROUP + i2

    iota_e = lax.broadcasted_iota(jnp.int32, (N_EXPERTS, t), 0)
    hit1 = iota_e == e1
    hit2 = iota_e == e2
    hits = jnp.where(hit1 | hit2, 1.0, 0.0)
    prefix = jnp.dot(hits.astype(BF16), before_sc[...],
                     preferred_element_type=F32) + carry_sc[...]
    r1 = jnp.sum(jnp.where(hit1, prefix, 0.0), axis=0, keepdims=True)
    r2 = jnp.sum(jnp.where(hit2, prefix, 0.0), axis=0, keepdims=True)
    carry_sc[...] = carry_sc[...] + jnp.sum(hits, axis=1, keepdims=True)

    ri_ref[...] = jnp.zeros(ri_ref.shape, jnp.int32)
    ri_ref[0:1, :] = e1
    ri_ref[1:2, :] = e2
    ri_ref[2:3, :] = r1.astype(jnp.int32)
    ri_ref[3:4, :] = r2.astype(jnp.int32)
    rw_ref[...] = jnp.zeros(rw_ref.shape, F32)
    rw_ref[0:1, :] = w1
    rw_ref[1:2, :] = w2
    cnt_ref[...] = jnp.broadcast_to(carry_sc[...], cnt_ref.shape)


def _dest_kernel(cnt_ref, ri_ref, dest_ref, blk_ref, nused_ref, padinfo_ref, *, n_blocks_padded):
    cnt = cnt_ref[...]
    pad = jnp.floor((cnt + (MOE_ROWS - 1)) * (1.0 / MOE_ROWS)) * MOE_ROWS
    lower = jnp.where(lax.broadcasted_iota(jnp.int32, (N_EXPERTS, N_EXPERTS), 1)
                      <= lax.broadcasted_iota(jnp.int32, (N_EXPERTS, N_EXPERTS), 0), 1.0, 0.0)
    pad_end = jnp.dot(lower, pad, precision=lax.Precision.HIGHEST, preferred_element_type=F32)
    pad_start = (pad_end - pad)[:, 0:1]
    t = ri_ref.shape[1]
    iota_e = lax.broadcasted_iota(jnp.int32, (N_EXPERTS, t), 0)
    dest_ref[...] = jnp.zeros(dest_ref.shape, jnp.int32)
    for kk in range(2):
        start = jnp.sum(jnp.where(iota_e == ri_ref[kk:kk + 1, :], pad_start, 0.0), axis=0, keepdims=True)
        dest_ref[kk:kk + 1, :] = (start.astype(jnp.int32) + ri_ref[2 + kk:3 + kk, :]) * ROW_SUBLANES
    blk_start = (lax.broadcasted_iota(jnp.int32, (N_EXPERTS, n_blocks_padded), 1) * MOE_ROWS).astype(F32)
    owner = jnp.sum(jnp.where(pad_end[:, 0:1] <= blk_start, 1.0, 0.0), axis=0, keepdims=True)
    blk_ref[...] = jnp.broadcast_to(jnp.minimum(owner, N_EXPERTS - 1.0).astype(jnp.int32), blk_ref.shape)
    nused_ref[...] = jnp.broadcast_to((pad_end[N_EXPERTS - 1:N_EXPERTS, :] * (1.0 / MOE_ROWS)).astype(jnp.int32),
                                      nused_ref.shape)
    diag = (lax.broadcasted_iota(jnp.int32, (N_EXPERTS, 128), 0) == lax.broadcasted_iota(jnp.int32, (N_EXPERTS, 128), 1))
    first_pad = jnp.sum(jnp.where(diag, (pad_end - pad + cnt) * ROW_SUBLANES, 0.0), axis=0, keepdims=True)
    n_pad = jnp.sum(jnp.where(diag, pad - cnt, 0.0), axis=0, keepdims=True)
    padinfo_ref[...] = jnp.zeros(padinfo_ref.shape, jnp.int32)
    padinfo_ref[0:1, :] = first_pad.astype(jnp.int32)
    padinfo_ref[1:2, :] = n_pad.astype(jnp.int32)
    padinfo_ref[2:3, :] = (pad_end[N_EXPERTS - 1:N_EXPERTS, :] * (1.0 / MOE_ROWS)).astype(jnp.int32)


def _to_row_tiles(ref, x):
    n = x.shape[0]
    for c in range(ROW_SUBLANES):
        ref[pl.ds(c, n, stride=ROW_SUBLANES), :] = x[:, c * 128:(c + 1) * 128]


def _from_row_tiles(ref, n):
    return jnp.concatenate([ref[pl.ds(c, n, stride=ROW_SUBLANES), :] for c in range(ROW_SUBLANES)], axis=1)


def _tile_copy(src_ref, src_row8, dst_ref, dst_row8, sem):
    return pltpu.make_async_copy(src_ref.at[pl.ds(pl.multiple_of(src_row8, ROW_SUBLANES), ROW_SUBLANES)],
                                 dst_ref.at[pl.ds(pl.multiple_of(dst_row8, ROW_SUBLANES), ROW_SUBLANES)], sem)


def _dispatch_kernel(dest_ref, padinfo_ref, x_ref, g_ref, xs_ref, h_sc, zero_sc, sem, blk_sem, *, tile,
                     experts_per_step, n_blocks):
    step = pl.program_id(0)
    _to_row_tiles(h_sc, _rms_rows(x_ref[...], g_ref[...]))
    zero_sc[...] = jnp.zeros(zero_sc.shape, F32)

    def block_copy(b):
        rows = MOE_ROWS * ROW_SUBLANES
        return pltpu.make_async_copy(zero_sc, xs_ref.at[pl.ds(pl.multiple_of(b * rows, rows), rows)], blk_sem)

    for j in range(experts_per_step):
        b = padinfo_ref[2, 0] + step * experts_per_step + j

        @pl.when(b < n_blocks)
        def _():
            block_copy(b).start(priority=1)

    def issue(t, carry):
        _tile_copy(h_sc, t * ROW_SUBLANES, xs_ref, dest_ref[0, 0, t], sem).start()
        _tile_copy(h_sc, t * ROW_SUBLANES, xs_ref, dest_ref[0, 0, tile + t], sem).start(priority=1)
        return carry

    def drain(t, carry):
        _tile_copy(h_sc, 0, xs_ref, 0, sem).wait()
        _tile_copy(h_sc, 0, xs_ref, 0, sem).wait()
        return carry

    lax.fori_loop(0, tile, issue, 0, unroll=DMA_UNROLL)

    n_zero = jnp.int32(0)
    for j in range(experts_per_step):
        e = step * experts_per_step + j
        e_c = jnp.minimum(e, N_EXPERTS - 1)
        first = padinfo_ref[0, e_c]
        count = jnp.where(e < N_EXPERTS, padinfo_ref[1, e_c], 0)

        def zero_row(r, carry, first=first):
            _tile_copy(zero_sc, 0, xs_ref, first + r * ROW_SUBLANES, sem).start()
            return carry

        lax.fori_loop(0, count, zero_row, 0)
        n_zero = n_zero + count

    lax.fori_loop(0, tile, drain, 0, unroll=DMA_UNROLL)
    lax.fori_loop(0, n_zero, lambda r, carry: (_tile_copy(zero_sc, 0, xs_ref, 0, sem).wait(), carry)[1], 0)
    for j in range(experts_per_step):
        b = padinfo_ref[2, 0] + step * experts_per_step + j

        @pl.when(b < n_blocks)
        def _():
            block_copy(b).wait()


def _expert_kernel(blk_ref, nused_ref, xs_ref, wgu_ref, wd_ref, ys_ref):
    b = pl.program_id(0)

    @pl.when(b < nused_ref[0])
    def _():
        x = _from_row_tiles(xs_ref, MOE_ROWS).astype(BF16)
        gu = jnp.dot(x, wgu_ref[0], preferred_element_type=F32)
        a, u = gu[:, :D_EXPERT], gu[:, D_EXPERT:]
        hidden = (a * jax.nn.sigmoid(a) * u).astype(BF16)
        _to_row_tiles(ys_ref, jnp.dot(hidden, wd_ref[0], preferred_element_type=F32))

    @pl.when(b >= nused_ref[0])
    def _():
        ys_ref[...] = jnp.zeros(ys_ref.shape, F32)


def _combine_kernel(dest_ref, dest_next_ref, ys_ref, x_ref, w_ref, g_ref, y_ref, buf, sem, *, tile, final_norm):
    step = pl.program_id(0)
    slot = lax.rem(step, 2)

    def issue(d_ref, slot_):
        def body(t, carry):
            _tile_copy(ys_ref, d_ref[0, 0, t], buf.at[slot_, 0], t * ROW_SUBLANES, sem.at[slot_]).start()
            _tile_copy(ys_ref, d_ref[0, 0, tile + t], buf.at[slot_, 1], t * ROW_SUBLANES,
                       sem.at[slot_]).start(priority=1)
            return carry
        lax.fori_loop(0, tile, body, 0, unroll=DMA_UNROLL)

    @pl.when(step == 0)
    def _():
        issue(dest_ref, 0)

    @pl.when(step + 1 < pl.num_programs(0))
    def _():
        issue(dest_next_ref, 1 - slot)

    def drain(t, carry):
        _tile_copy(ys_ref, 0, buf.at[slot, 0], 0, sem.at[slot]).wait()
        _tile_copy(ys_ref, 0, buf.at[slot, 1], 0, sem.at[slot]).wait()
        return carry

    lax.fori_loop(0, tile, drain, 0, unroll=DMA_UNROLL)
    w = w_ref[...]
    y = x_ref[...] + (w[:, 0:1] * _from_row_tiles(buf.at[slot, 0], tile)
                      + w[:, 1:2] * _from_row_tiles(buf.at[slot, 1], tile))
    if final_norm:
        y = _rms_rows(y, g_ref[...])
    y_ref[...] = y


def _moe(x, g_ffn, wrT, br, wgu, wd, g_final, final_norm):
    n_tok = x.shape[0]
    tile = min(TOK_TILE, n_tok)
    n_tiles = n_tok // tile
    n_blocks = (2 * n_tok + MOE_ROWS - 1) // MOE_ROWS + N_EXPERTS
    n_blocks_padded = (n_blocks + 127) // 128 * 128
    n_rows = n_blocks * MOE_ROWS

    ri, rw, cnt = pl.pallas_call(
        _router_kernel,
        grid=(n_tiles,),
        in_specs=[
            pl.BlockSpec((tile, D_MODEL), lambda i: (i, 0)),
            _const_spec((1, D_MODEL)),
            _const_spec((2, ROUTER_LANES, D_MODEL)),
            _const_spec((ROUTER_LANES, 1)),
        ],
        out_specs=[
            pl.BlockSpec((8, tile), lambda i: (0, i)),
            pl.BlockSpec((8, tile), lambda i: (0, i)),
            _const_spec((N_EXPERTS, 128)),
        ],
        out_shape=[
            jax.ShapeDtypeStruct((8, n_tok), jnp.int32),
            jax.ShapeDtypeStruct((8, n_tok), F32),
            jax.ShapeDtypeStruct((N_EXPERTS, 128), F32),
        ],
        scratch_shapes=[pltpu.VMEM((N_EXPERTS, 1), F32), pltpu.VMEM((tile, tile), BF16)],
        compiler_params=_cparams(1),
        name="moe_router",
    )(x, g_ffn, wrT, br)

    dest, blk, nused, padinfo = pl.pallas_call(
        functools.partial(_dest_kernel, n_blocks_padded=n_blocks_padded),
        grid=(n_tiles,),
        in_specs=[_const_spec((N_EXPERTS, 128)), pl.BlockSpec((8, tile), lambda i: (0, i))],
        out_specs=[
            pl.BlockSpec((8, tile), lambda i: (0, i)),
            _const_spec((8, n_blocks_padded)),
            _const_spec((8, 128)),
            _const_spec((8, 128)),
        ],
        out_shape=[
            jax.ShapeDtypeStruct((8, n_tok), jnp.int32),
            jax.ShapeDtypeStruct((8, n_blocks_padded), jnp.int32),
            jax.ShapeDtypeStruct((8, 128), jnp.int32),
            jax.ShapeDtypeStruct((8, 128), jnp.int32),
        ],
        compiler_params=_cparams(1),
        name="moe_dest",
    )(cnt, ri)

    dest_tiles = dest[0:2].reshape(2, n_tiles, tile).transpose(1, 0, 2).reshape(n_tiles, 1, 2 * tile)
    dest_spec = pl.BlockSpec((1, 1, 2 * tile), lambda i: (i, 0, 0), memory_space=pltpu.SMEM)
    any_spec = pl.BlockSpec(memory_space=pl.ANY)

    xs = pl.pallas_call(
        functools.partial(_dispatch_kernel, tile=tile, experts_per_step=pl.cdiv(N_EXPERTS, n_tiles),
                          n_blocks=n_blocks),
        grid=(n_tiles,),
        in_specs=[dest_spec, pl.BlockSpec((8, 128), lambda i: (0, 0), memory_space=pltpu.SMEM),
                  pl.BlockSpec((tile, D_MODEL), lambda i: (i, 0)), _const_spec((1, D_MODEL))],
        out_specs=any_spec,
        out_shape=jax.ShapeDtypeStruct((n_rows * ROW_SUBLANES, 128), F32),
        scratch_shapes=[pltpu.VMEM((tile * ROW_SUBLANES, 128), F32),
                        pltpu.VMEM((MOE_ROWS * ROW_SUBLANES, 128), F32),
                        pltpu.SemaphoreType.DMA(()), pltpu.SemaphoreType.DMA(())],
        compiler_params=_cparams(1),
        name="moe_dispatch",
    )(dest_tiles, padinfo, x, g_ffn)

    blk_rows = MOE_ROWS * ROW_SUBLANES
    ys = pl.pallas_call(
        _expert_kernel,
        grid_spec=pltpu.PrefetchScalarGridSpec(
            num_scalar_prefetch=2,
            grid=(n_blocks,),
            in_specs=[
                pl.BlockSpec((blk_rows, 128), lambda b, blk, nu: (b, 0)),
                pl.BlockSpec((1, D_MODEL, 2 * D_EXPERT), lambda b, blk, nu: (blk[b], 0, 0)),
                pl.BlockSpec((1, D_EXPERT, D_MODEL), lambda b, blk, nu: (blk[b], 0, 0)),
            ],
            out_specs=pl.BlockSpec((blk_rows, 128), lambda b, blk, nu: (b, 0)),
        ),
        out_shape=jax.ShapeDtypeStruct((n_rows * ROW_SUBLANES, 128), F32),
        compiler_params=_cparams(1),
        name="moe_experts",
    )(blk[0, :n_blocks], nused[0, :1], xs, wgu, wd)

    ctile = min(256, n_tok)
    c_tiles = n_tok // ctile
    dest_c = dest[0:2].reshape(2, c_tiles, ctile).transpose(1, 0, 2).reshape(c_tiles, 1, 2 * ctile)
    return pl.pallas_call(
        functools.partial(_combine_kernel, tile=ctile, final_norm=final_norm),
        grid=(c_tiles,),
        in_specs=[
            pl.BlockSpec((1, 1, 2 * ctile), lambda i: (i, 0, 0), memory_space=pltpu.SMEM),
            pl.BlockSpec((1, 1, 2 * ctile), lambda i: (jnp.minimum(i + 1, c_tiles - 1), 0, 0),
                         memory_space=pltpu.SMEM),
            any_spec,
            pl.BlockSpec((ctile, D_MODEL), lambda i: (i, 0)),
            pl.BlockSpec((ctile, 2), lambda i: (i, 0)),
            _const_spec((1, D_MODEL)),
        ],
        out_specs=pl.BlockSpec((ctile, D_MODEL), lambda i: (i, 0)),
        out_shape=jax.ShapeDtypeStruct((n_tok, D_MODEL), F32),
        scratch_shapes=[pltpu.VMEM((2, 2, ctile * ROW_SUBLANES, 128), F32), pltpu.SemaphoreType.DMA((2,))],
        compiler_params=_cparams(1),
        name="moe_combine",
    )(dest_c, dest_c, ys, x, rw[0:2].T, g_final)


def _rope_tables(n_tokens):
    rows = n_tokens // GRID_W
    row_idx = jnp.repeat(jnp.arange(rows, dtype=F32), GRID_W)
    col_idx = jnp.tile(jnp.arange(GRID_W, dtype=F32), rows)

    def angles(rot_dim):
        n_freq = rot_dim // 4
        inv_freq = ROPE_THETA ** (-jnp.arange(n_freq, dtype=F32) / n_freq)
        return jnp.concatenate([row_idx[:, None] * inv_freq, col_idx[:, None] * inv_freq], axis=-1)

    ag, am = angles(GQA_HD), angles(MLA_ROPE)
    cg, sg, cm, sm = jnp.cos(ag), jnp.sin(ag), jnp.cos(am), jnp.sin(am)
    hr = MLA_ROPE // 2
    zl = jnp.zeros((n_tokens, MLA_NOPE), F32)
    zr = jnp.zeros((n_tokens, HEAD_PAD - MLA_NOPE - MLA_ROPE), F32)
    del hr
    return dict(
        cT_g=cg.T, sT_g=sg.T,
        cF_g=jnp.concatenate([cg, cg], axis=1), sF_g=jnp.concatenate([-sg, sg], axis=1),
        cT_m=cm.T, sT_m=sm.T,
        cP_m=jnp.concatenate([zl, cm, cm, zr], axis=1), sP_m=jnp.concatenate([zl, -sm, sm, zr], axis=1),
    )


def _hi_lo(w):
    hi = w.astype(BF16)
    return jnp.stack([hi, (w - hi.astype(F32)).astype(BF16)])


def _layer_params(l, norm_mix, norm_mem, w_in, gqa_q_norm, gqa_k_norm, mla_q_a_norm, mla_kv_a_norm, mla_w_qb,
                  mla_w_kvb, mem_w_kv, w_out, norm_ffn, w_group, b_group, w_expert, b_expert, w_gate_up, w_down):
    splits = []
    off = 0
    for width in (GQA_HEADS * GQA_HD, GQA_KV_HEADS * GQA_HD, GQA_KV_HEADS * GQA_HD, MLA_Q_LORA, MLA_KV_LORA,
                  MLA_ROPE, MEM_HEADS * MEM_HD, N_BRANCH * D_MODEL):
        splits.append(w_in[l][:, off:off + width])
        off += width
    wq_g, wk_g, wv_g, wq_a, wkv_a, wk_r, wq_c, w_gate = splits
    hr = MLA_ROPE // 2
    pad_l = jnp.zeros((D_MODEL, MLA_NOPE), F32)
    pad_r = jnp.zeros((D_MODEL, HEAD_PAD - MLA_NOPE - MLA_ROPE), F32)
    wk_r_swapped = jnp.concatenate([wk_r[:, hr:], wk_r[:, :hr]], axis=1)
    wkr2 = jnp.concatenate([pad_l, wk_r, pad_r, pad_l, wk_r_swapped, pad_r], axis=1)
    qk = MLA_NOPE + MLA_ROPE
    wqb = jnp.pad(mla_w_qb[l].reshape(MLA_Q_LORA, MLA_HEADS, qk), ((0, 0), (0, 0), (0, HEAD_PAD - qk)))
    wkvb = mla_w_kvb[l].reshape(MLA_KV_LORA, MLA_HEADS, MLA_NOPE + MLA_V)
    wkn = jnp.pad(wkvb[:, :, :MLA_NOPE], ((0, 0), (0, 0), (0, HEAD_PAD - MLA_NOPE)))
    wv_m = wkvb[:, :, MLA_NOPE:]
    n_router = N_GROUPS + N_EXPERTS
    return dict(
        g_mix=norm_mix[l][None, :], g_mem=norm_mem[l][None, :], g_ffn=norm_ffn[l][None, :],
        wqT_g=wq_g.T.astype(BF16), wk_g=wk_g.astype(BF16), wvT_g=wv_g.T.astype(BF16),
        gq=gqa_q_norm[l][:, None], gk=gqa_k_norm[l][None, :],
        wqaT=wq_a.T.astype(BF16), gqa=mla_q_a_norm[l][:, None],
        wqbT=wqb.reshape(MLA_Q_LORA, MLA_HEADS * HEAD_PAD).T.astype(BF16),
        wkva=wkv_a.astype(BF16), gkva=mla_kv_a_norm[l][None, :],
        wkn=wkn.reshape(MLA_KV_LORA, MLA_HEADS * HEAD_PAD).astype(BF16), wkr2=wkr2.astype(BF16),
        wvT_m=wv_m.reshape(MLA_KV_LORA, MLA_HEADS * MLA_V).T.astype(BF16),
        wqcT=wq_c.T.astype(BF16), wgT=w_gate.T.astype(BF16),
        wmk=mem_w_kv[l][:, :MEM_HEADS * MEM_HD].astype(BF16),
        wmvT=mem_w_kv[l][:, MEM_HEADS * MEM_HD:].T.astype(BF16),
        w_out=w_out[l].astype(BF16),
        wrT=_hi_lo(jnp.concatenate([w_group[l].T, w_expert[l].T,
                                    jnp.zeros((ROUTER_LANES - n_router, D_MODEL), F32)], axis=0)),
        br=jnp.concatenate([b_group[l], b_expert[l], jnp.zeros((ROUTER_LANES - n_router,), F32)])[:, None],
        wgu=w_gate_up[l].astype(BF16), wd=w_down[l].astype(BF16),
    )


def _mixer(x, mem, p, tabs):
    b, s, _ = x.shape
    tile = min(TOK_TILE, s)
    n_t = s // tile
    grid = (b, n_t)
    x_spec = pl.BlockSpec((1, tile, D_MODEL), lambda bi, i: (bi, i, 0))
    colT = lambda rows: pl.BlockSpec((rows, tile), lambda bi, i: (0, i))
    rowF = pl.BlockSpec((tile, HEAD_PAD), lambda bi, i: (i, 0))
    headT = lambda heads: pl.BlockSpec((1, heads, 1, HEAD_PAD, tile), lambda bi, i: (bi, 0, i, 0, 0))
    keyF = lambda heads: pl.BlockSpec((1, heads, tile, HEAD_PAD), lambda bi, i: (bi, 0, i, 0))
    valT = lambda heads: pl.BlockSpec((1, heads, 1, V_ROWS, tile), lambda bi, i: (bi, 0, i, 0, 0))

    gqa_in = [(p["wqT_g"], _const_spec(p["wqT_g"].shape)), (p["wk_g"], _const_spec(p["wk_g"].shape)),
              (p["wvT_g"], _const_spec(p["wvT_g"].shape)), (p["gq"], _const_spec((GQA_HD, 1))),
              (p["gk"], _const_spec((1, GQA_HD))), (tabs["cT_g"], colT(GQA_HD // 2)),
              (tabs["sT_g"], colT(GQA_HD // 2)), (tabs["cF_g"], rowF), (tabs["sF_g"], rowF)]
    mla_in = [(p["wqaT"], _const_spec(p["wqaT"].shape)), (p["gqa"], _const_spec((MLA_Q_LORA, 1))),
              (p["wqbT"], _const_spec(p["wqbT"].shape)), (p["wkva"], _const_spec(p["wkva"].shape)),
              (p["gkva"], _const_spec((1, MLA_KV_LORA))), (p["wkn"], _const_spec(p["wkn"].shape)),
              (p["wkr2"], _const_spec(p["wkr2"].shape)), (p["wvT_m"], _const_spec(p["wvT_m"].shape)),
              (tabs["cT_m"], colT(MLA_ROPE // 2)), (tabs["sT_m"], colT(MLA_ROPE // 2)),
              (tabs["cP_m"], rowF), (tabs["sP_m"], rowF)]
    assert len(gqa_in) == 9 and len(mla_in) == 12
    qkv_shapes = lambda heads, kv_heads: [jax.ShapeDtypeStruct((b, heads, n_t, HEAD_PAD, tile), BF16),
                                          jax.ShapeDtypeStruct((b, kv_heads, s, HEAD_PAD), BF16),
                                          jax.ShapeDtypeStruct((b, kv_heads, n_t, V_ROWS, tile), BF16)]
    qg, kg, vg, qm, km, vm = pl.pallas_call(
        _proj_attn_kernel,
        grid=grid,
        in_specs=[x_spec, _const_spec((1, D_MODEL))] + [spec for _, spec in gqa_in + mla_in],
        out_specs=[headT(GQA_HEADS), keyF(GQA_KV_HEADS), valT(GQA_KV_HEADS),
                   headT(MLA_HEADS), keyF(MLA_HEADS), valT(MLA_HEADS)],
        out_shape=qkv_shapes(GQA_HEADS, GQA_KV_HEADS) + qkv_shapes(MLA_HEADS, MLA_HEADS),
        compiler_params=_cparams(2),
        name="proj_attn",
    )(x, p["g_mix"], *[arr for arr, _ in gqa_in + mla_in])

    n_mem = mem.shape[1]
    mk, mvT = pl.pallas_call(
        _mem_kv_kernel,
        grid=(b,),
        in_specs=[pl.BlockSpec((1, n_mem, D_MODEL), lambda bi: (bi, 0, 0)), _const_spec((1, D_MODEL)),
                  _const_spec(p["wmk"].shape), _const_spec(p["wmvT"].shape)],
        out_specs=[pl.BlockSpec((1, n_mem, D_MODEL), lambda bi: (bi, 0, 0)),
                   pl.BlockSpec((1, D_MODEL, n_mem), lambda bi: (bi, 0, 0))],
        out_shape=[jax.ShapeDtypeStruct((b, n_mem, D_MODEL), BF16), jax.ShapeDtypeStruct((b, D_MODEL, n_mem), BF16)],
        compiler_params=_cparams(1),
        name="mem_kv",
    )(mem, p["g_mem"], p["wmk"], p["wmvT"])

    oa = _attention(qg, kg, vg, GQA_HEADS // GQA_KV_HEADS)
    ob = _attention(qm, km, vm, 1)
    return pl.pallas_call(
        _mixer_tail_kernel,
        grid=grid,
        in_specs=[x_spec, _const_spec((1, D_MODEL)), headT(GQA_HEADS), headT(MLA_HEADS),
                  pl.BlockSpec((1, n_mem, D_MODEL), lambda bi, i: (bi, 0, 0)),
                  pl.BlockSpec((1, D_MODEL, n_mem), lambda bi, i: (bi, 0, 0)),
                  _const_spec(p["wqcT"].shape), _const_spec(p["wgT"].shape), _const_spec((D_MODEL, D_MODEL))],
        out_specs=x_spec,
        out_shape=jax.ShapeDtypeStruct(x.shape, F32),
        compiler_params=_cparams(2),
        name="mixer_tail",
    )(x, p["g_mix"], oa, ob, mk, mvT, p["wqcT"], p["wgT"], p["w_out"])


def _trunk(x, mem, layers, g_final):
    b, s, d = x.shape
    tabs = _rope_tables(s)
    for l, p in enumerate(layers):
        x = _mixer(x, mem, p, tabs)
        x = _moe(x.reshape(b * s, d), p["g_ffn"], p["wrT"], p["br"], p["wgu"], p["wd"], g_final,
                 final_norm=(l == len(layers) - 1)).reshape(b, s, d)
    return x


def kernel(x_prompt, x_sample, mem_prompt, mem_sample, norm_mix, norm_mem, w_in, gqa_q_norm, gqa_k_norm, mla_q_a_norm, mla_kv_a_norm, mla_w_qb, mla_w_kvb, mem_w_kv, w_out, norm_ffn, w_group, b_group, w_expert, b_expert, w_gate_up, w_down, norm_final):
    depth = w_in.shape[0]
    layers = [_layer_params(l, norm_mix, norm_mem, w_in, gqa_q_norm, gqa_k_norm, mla_q_a_norm, mla_kv_a_norm,
                            mla_w_qb, mla_w_kvb, mem_w_kv, w_out, norm_ffn, w_group, b_group, w_expert, b_expert,
                            w_gate_up, w_down) for l in range(depth)]
    g_final = norm_final[None, :]
    return (_trunk(x_prompt, mem_prompt, layers, g_final), _trunk(x_sample, mem_sample, layers, g_final))
```

```python
import functools
import math

import jax
import jax.numpy as jnp
from jax import lax
from jax.experimental import pallas as pl
from jax.experimental.pallas import tpu as pltpu

F32 = jnp.float32
BF16 = jnp.bfloat16

D_MODEL = 1024
EPS = 1e-6
GRID_W = 64
ROPE_THETA = 10000.0
GQA_HEADS, GQA_KV_HEADS, GQA_HD = 8, 2, 128
MLA_HEADS, MLA_Q_LORA, MLA_KV_LORA = 8, 384, 256
MLA_NOPE, MLA_ROPE, MLA_V = 64, 32, 128
MEM_HEADS, MEM_HD = 4, 256
N_BRANCH = 3
N_GROUPS, EXPERTS_PER_GROUP, D_EXPERT = 8, 8, 256
N_EXPERTS = N_GROUPS * EXPERTS_PER_GROUP
HEAD_PAD = 128
BF16_SUBLANES = 16
V_ROWS = HEAD_PAD + BF16_SUBLANES
LOG2E = 1.4426950408889634

TOK_TILE = 512
ATT_Q_COLS = 4096
ATT_CHUNKS_PER_BODY = 2
MOE_ROWS = 512
COMBINE_TILE = 512
ROUTER_LANES = 128
ROW_SUBLANES = D_MODEL // 128
DMA_UNROLL = 8
VMEM_LIMIT = 56 * 1024 * 1024


def _cparams(n_axes, flags=None):
    return pltpu.CompilerParams(dimension_semantics=("arbitrary",) * n_axes, vmem_limit_bytes=VMEM_LIMIT,
                                flags=flags)


def _nt(a, b):
    return lax.dot_general(a, b, (((1,), (1,)), ((), ())), preferred_element_type=F32)


def _rms_rows(x, g):
    return x * lax.rsqrt(jnp.mean(x * x, axis=-1, keepdims=True) + EPS) * g


def _rms_cols(x, g):
    return x * lax.rsqrt(jnp.mean(x * x, axis=0, keepdims=True) + EPS) * g


def _const_spec(shape):
    return pl.BlockSpec(shape, lambda *_: (0,) * len(shape))


def _store_values(v_ref, vT):
    t = vT.shape[1]
    v_ref[0:HEAD_PAD, :] = vT.astype(BF16)
    ones_row = lax.broadcasted_iota(jnp.int32, (BF16_SUBLANES, t), 0) == 0
    v_ref[HEAD_PAD:V_ROWS, :] = jnp.where(ones_row, 1.0, 0.0).astype(BF16)


def _proj_attn_kernel(x_ref, g_ref, *refs):
    h = _rms_rows(x_ref[0], g_ref[...]).astype(BF16)
    n_g, n_m = 9, 12
    _proj_mla(h, *refs[n_g:n_g + n_m], *refs[n_g + n_m + 3:])
    _proj_gqa(h, *refs[:n_g], *refs[n_g + n_m:n_g + n_m + 3])


def _proj_gqa(h, wqT_ref, wk_ref, wvT_ref, gq_ref, gk_ref, cT_ref, sT_ref, cF_ref, sF_ref, q_ref, k_ref, v_ref):
    c, s, gq = cT_ref[...], sT_ref[...], gq_ref[...]
    half = GQA_HD // 2
    scale = GQA_HD ** -0.5 * LOG2E
    heads_per_dot = 4
    k = jnp.dot(h, wk_ref[...], preferred_element_type=F32)
    q_pieces = []
    for h0 in range(0, GQA_HEADS, heads_per_dot):
        q_pieces.append(_nt(wqT_ref[h0 * GQA_HD:(h0 + heads_per_dot) * GQA_HD, :], h))
        if h0 == 0:
            for j in range(GQA_KV_HEADS):
                kn = _rms_rows(k[:, j * GQA_HD:(j + 1) * GQA_HD], gk_ref[...])
                ko = kn * cF_ref[...] + pltpu.roll(kn, half, 1) * sF_ref[...]
                k_ref[0, j] = ko.astype(BF16)
    vT = _nt(wvT_ref[...], h)
    for piece, qT in enumerate(q_pieces):
        for hd in range(heads_per_dot):
            qn = _rms_cols(qT[hd * GQA_HD:(hd + 1) * GQA_HD], gq)
            x1, x2 = qn[:half], qn[half:]
            o = jnp.concatenate([x1 * c - x2 * s, x2 * c + x1 * s], axis=0) * scale
            q_ref[0, piece * heads_per_dot + hd, 0] = o.astype(BF16)
    for j in range(GQA_KV_HEADS):
        _store_values(v_ref.at[0, j, 0], vT[j * GQA_HD:(j + 1) * GQA_HD])


def _proj_mla(h, wqaT_ref, gqa_ref, wqbT_ref, wkva_ref, gkva_ref, wkn_ref, wkr_ref, wvT_ref,
              cT_ref, sT_ref, cP_ref, sP_ref, q_ref, k_ref, v_ref):
    qa_raw = _nt(wqaT_ref[...], h)
    kva = jnp.dot(h, wkva_ref[...], preferred_element_type=F32)
    kr2 = jnp.dot(h, wkr_ref[...], preferred_element_type=F32)
    qa = _rms_cols(qa_raw, gqa_ref[...]).astype(BF16)
    qT = jnp.dot(wqbT_ref[...], qa, preferred_element_type=F32)
    kvn = _rms_rows(kva, gkva_ref[...]).astype(BF16)
    kn = jnp.dot(kvn, wkn_ref[...], preferred_element_type=F32)
    kr = kr2[:, :HEAD_PAD] * cP_ref[...] + kr2[:, HEAD_PAD:] * sP_ref[...]
    c, s = cT_ref[...], sT_ref[...]
    hr = MLA_ROPE // 2
    scale = (MLA_NOPE + MLA_ROPE) ** -0.5 * LOG2E
    for hd in range(MLA_HEADS):
        q = qT[hd * HEAD_PAD:(hd + 1) * HEAD_PAD]
        x1, x2 = q[MLA_NOPE:MLA_NOPE + hr], q[MLA_NOPE + hr:MLA_NOPE + 2 * hr]
        o = jnp.concatenate([q[:MLA_NOPE], x1 * c - x2 * s, x2 * c + x1 * s, q[MLA_NOPE + 2 * hr:]], axis=0) * scale
        q_ref[0, hd, 0] = o.astype(BF16)
    vT = _nt(wvT_ref[...], kvn)
    for hd in range(MLA_HEADS):
        k_ref[0, hd] = (kn[:, hd * HEAD_PAD:(hd + 1) * HEAD_PAD] + kr).astype(BF16)
    for hd in range(MLA_HEADS):
        _store_values(v_ref.at[0, hd, 0], vT[hd * MLA_V:(hd + 1) * MLA_V])


def _mem_kv_kernel(mem_ref, g_ref, wk_ref, wvT_ref, mk_ref, mvT_ref):
    h = _rms_rows(mem_ref[0], g_ref[...]).astype(BF16)
    mk_ref[0] = jnp.dot(h, wk_ref[...], preferred_element_type=F32).astype(BF16)
    mvT_ref[0] = _nt(wvT_ref[...], h).astype(BF16)


def _attn_kernel(q_ref, k_ref, v_ref, o_ref, m_sc, acc_sc, s_sc, mc_sc, *, chunks_per_head, n_kv):
    chunk = TOK_TILE
    n_sub = m_sc.shape[0]

    m_sc[...] = jnp.full(m_sc.shape, -jnp.inf, F32)
    acc_sc[...] = jnp.zeros(acc_sc.shape, F32)

    def scores(c, u, slot):
        k = k_ref[0, 0, pl.ds(pl.multiple_of(c * chunk, chunk), chunk), :]
        s = jnp.dot(k, q_ref[0, u // chunks_per_head, u % chunks_per_head],
                    preferred_element_type=F32)
        s_sc[slot, u] = s
        mc_sc[slot, u] = jnp.max(s, axis=0, keepdims=True)

    def softmax_values(c, u, slot):
        m_prev = m_sc[u]
        m_cur = jnp.maximum(m_prev, mc_sc[slot, u])
        alpha = jnp.exp2(m_prev - m_cur)
        p = jnp.exp2(s_sc[slot, u] - m_cur)
        acc_sc[u] = alpha * acc_sc[u] + jnp.dot(v_ref[0, 0, c], p.astype(BF16), preferred_element_type=F32)
        m_sc[u] = m_cur

    def stage(c_scores, c_values):
        for u in range(n_sub):
            if c_scores is not None:
                scores(c_scores, u, c_scores % 2)
            if c_values is not None:
                softmax_values(c_values, u, c_values % 2)

    per_body = ATT_CHUNKS_PER_BODY if (n_kv - 1) // ATT_CHUNKS_PER_BODY >= 2 else 2
    n_bodies = (n_kv - 1) // per_body
    stage(0, None)

    def body(i, carry):
        for t in range(per_body):
            for u in range(n_sub):
                scores(per_body * i + t + 1, u, (t + 1) % 2)
                softmax_values(per_body * i + t, u, t % 2)
        return carry

    lax.fori_loop(0, n_bodies, body, 0)
    for c in range(per_body * n_bodies, n_kv - 1):
        stage(c + 1, c)
    stage(None, n_kv - 1)

    for u in range(n_sub):
        o_ref[0, u // chunks_per_head, u % chunks_per_head] = (
            acc_sc[u, 0:HEAD_PAD] / acc_sc[u, HEAD_PAD:HEAD_PAD + 1]).astype(BF16)


def _attention(qT, k, vT, group):
    b, h, n_kv, hd, chunk = qT.shape
    hkv = k.shape[1]
    s = n_kv * chunk
    assert h == hkv * group and chunk == TOK_TILE
    chunks_per_head = min(ATT_Q_COLS // group, s) // chunk
    n_sub = group * chunks_per_head
    kern = functools.partial(_attn_kernel, chunks_per_head=chunks_per_head, n_kv=n_kv)
    q_spec = pl.BlockSpec((1, group, chunks_per_head, hd, chunk), lambda bi, j, qi: (bi, j, qi, 0, 0))
    return pl.pallas_call(
        kern,
        grid=(b, hkv, n_kv // chunks_per_head),
        in_specs=[
            q_spec,
            pl.BlockSpec((1, 1, s, hd), lambda bi, j, qi: (bi, j, 0, 0)),
            pl.BlockSpec((1, 1, n_kv, V_ROWS, chunk), lambda bi, j, qi: (bi, j, 0, 0, 0)),
        ],
        out_specs=q_spec,
        out_shape=jax.ShapeDtypeStruct(qT.shape, BF16),
        scratch_shapes=[
            pltpu.VMEM((n_sub, 1, chunk), F32),
            pltpu.VMEM((n_sub, V_ROWS, chunk), F32),
            pltpu.VMEM((2, n_sub, chunk, chunk), F32),
            pltpu.VMEM((2, n_sub, 1, chunk), F32),
        ],
        compiler_params=_cparams(3),
        name="attention",
    )(qT, k, vT)


def _mixer_tail_kernel(x_ref, g_ref, oa_ref, ob_ref, mk_ref, mvT_ref, wqcT_ref, wgT_ref, w_ref, y_ref):
    t = x_ref.shape[1]
    x = x_ref[0]
    h = _rms_rows(x, g_ref[...]).astype(BF16)
    scale = MEM_HD ** -0.5 * LOG2E
    piece = 2 * MEM_HD
    out = None
    for p0 in range(0, D_MODEL, piece):
        rows = slice(p0, p0 + piece)
        heads = [slice(p0, p0 + MEM_HD), slice(p0 + MEM_HD, p0 + piece)]
        gate_logits = lambda b: _nt(wgT_ref[b * D_MODEL + p0:b * D_MODEL + p0 + piece, :], h)
        q = [(_nt(wqcT_ref[r, :], h) * scale).astype(BF16) for r in heads]
        s = []
        g = []
        for i, r in enumerate(heads):
            s.append(jnp.dot(mk_ref[0, :, r], q[i], preferred_element_type=F32))
            g.append(gate_logits(i))
        oc = []
        for i, r in enumerate(heads):
            pr = jnp.exp2(s[i] - jnp.max(s[i], axis=0, keepdims=True))
            l = jnp.sum(pr, axis=0, keepdims=True)
            oc.append(jnp.dot(mvT_ref[0, r, :], pr.astype(BF16), preferred_element_type=F32) / l)
            if i == 0:
                g.append(gate_logits(2))
        oa = oa_ref[0].reshape(D_MODEL, t)[rows].astype(F32)
        ob = ob_ref[0].reshape(D_MODEL, t)[rows].astype(F32)
        merged = (jax.nn.sigmoid(g[0]) * oa + jax.nn.sigmoid(g[1]) * ob
                  + jax.nn.sigmoid(g[2]) * jnp.concatenate(oc, axis=0)).astype(BF16)
        part = lax.dot_general(merged, w_ref[rows, :], (((0,), (0,)), ((), ())), preferred_element_type=F32)
        out = part if out is None else out + part
    y_ref[0] = x + out


def _first_index_of_max(v, vmax, iota, n):
    return jnp.min(jnp.where(v == vmax, iota, n), axis=0, keepdims=True)


def _router_kernel(x_ref, g_ref, wT_ref, b_ref, ri_ref, rw_ref, cnt_ref, carry_sc, before_sc):
    step = pl.program_id(0)
    t = x_ref.shape[0]

    @pl.when(step == 0)
    def _():
        carry_sc[...] = jnp.zeros(carry_sc.shape, F32)
        before = lax.broadcasted_iota(jnp.int32, (t, t), 0) < lax.broadcasted_iota(jnp.int32, (t, t), 1)
        before_sc[...] = jnp.where(before, 1.0, 0.0).astype(BF16)

    h = _rms_rows(x_ref[...], g_ref[...])
    h_hi = h.astype(BF16)
    h_lo = (h - h_hi.astype(F32)).astype(BF16)
    w_hi, w_lo = wT_ref[0], wT_ref[1]
    logits = _nt(w_hi, h_hi) + _nt(w_hi, h_lo) + _nt(w_lo, h_hi) + b_ref[...]
    iota_g = lax.broadcasted_iota(jnp.int32, (N_GROUPS, t), 0)
    gl = logits[0:N_GROUPS]
    gmax = jnp.max(gl, axis=0, keepdims=True)
    gidx = _first_index_of_max(gl, gmax, iota_g, N_GROUPS)
    gw = 1.0 / jnp.sum(jnp.exp(gl - gmax), axis=0, keepdims=True)
    sel = jnp.zeros((EXPERTS_PER_GROUP, t), F32)
    for g in range(N_GROUPS):
        r0 = N_GROUPS + g * EXPERTS_PER_GROUP
        sel = sel + jnp.where(gidx == g, logits[r0:r0 + EXPERTS_PER_GROUP], 0.0)
    v1 = jnp.max(sel, axis=0, keepdims=True)
    i1 = _first_index_of_max(sel, v1, iota_g, EXPERTS_PER_GROUP)
    sel2 = jnp.where(iota_g == i1, -jnp.inf, sel)
    v2 = jnp.max(sel2, axis=0, keepdims=True)
    i2 = _first_index_of_max(sel2, v2, iota_g, EXPERTS_PER_GROUP)
    e = jnp.exp(v2 - v1)
    w1 = gw / (1.0 + e)
    w2 = gw * e / (1.0 + e)
    e1 = gidx * EXPERTS_PER_GROUP + i1
    e2 = gidx * EXPERTS_PER_GROUP + i2

    iota_e = lax.broadcasted_iota(jnp.int32, (N_EXPERTS, t), 0)
    hit1 = iota_e == e1
    hit2 = iota_e == e2
    hits = jnp.where(hit1 | hit2, 1.0, 0.0)
    prefix = jnp.dot(hits.astype(BF16), before_sc[...],
                     preferred_element_type=F32) + carry_sc[...]
    r1 = jnp.sum(jnp.where(hit1, prefix, 0.0), axis=0, keepdims=True)
    r2 = jnp.sum(jnp.where(hit2, prefix, 0.0), axis=0, keepdims=True)
    carry_sc[...] = carry_sc[...] + jnp.sum(hits, axis=1, keepdims=True)

    ri_ref[...] = jnp.zeros(ri_ref.shape, jnp.int32)
    ri_ref[0:1, :] = e1
    ri_ref[1:2, :] = e2
    ri_ref[2:3, :] = r1.astype(jnp.int32)
    ri_ref[3:4, :] = r2.astype(jnp.int32)
    rw_ref[...] = jnp.zeros(rw_ref.shape, F32)
    rw_ref[0:1, :] = w1
    rw_ref[1:2, :] = w2
    cnt_ref[...] = jnp.broadcast_to(carry_sc[...], cnt_ref.shape)


def _dest_kernel(cnt_ref, ri_ref, dest_ref, blk_ref, nused_ref, padinfo_ref, *, n_blocks_padded):
    cnt = cnt_ref[...]
    pad = jnp.floor((cnt + (MOE_ROWS - 1)) * (1.0 / MOE_ROWS)) * MOE_ROWS
    lower = jnp.where(lax.broadcasted_iota(jnp.int32, (N_EXPERTS, N_EXPERTS), 1)
                      <= lax.broadcasted_iota(jnp.int32, (N_EXPERTS, N_EXPERTS), 0), 1.0, 0.0)
    pad_end = jnp.dot(lower, pad, precision=lax.Precision.HIGHEST, preferred_element_type=F32)
    pad_start = (pad_end - pad)[:, 0:1]
    t = ri_ref.shape[1]
    iota_e = lax.broadcasted_iota(jnp.int32, (N_EXPERTS, t), 0)
    dest_ref[...] = jnp.zeros(dest_ref.shape, jnp.int32)
    for kk in range(2):
        start = jnp.sum(jnp.where(iota_e == ri_ref[kk:kk + 1, :], pad_start, 0.0), axis=0, keepdims=True)
        dest_ref[kk:kk + 1, :] = (start.astype(jnp.int32) + ri_ref[2 + kk:3 + kk, :]) * ROW_SUBLANES
    blk_start = (lax.broadcasted_iota(jnp.int32, (N_EXPERTS, n_blocks_padded), 1) * MOE_ROWS).astype(F32)
    owner = jnp.sum(jnp.where(pad_end[:, 0:1] <= blk_start, 1.0, 0.0), axis=0, keepdims=True)
    blk_ref[...] = jnp.broadcast_to(jnp.minimum(owner, N_EXPERTS - 1.0).astype(jnp.int32), blk_ref.shape)
    nused_ref[...] = jnp.broadcast_to((pad_end[N_EXPERTS - 1:N_EXPERTS, :] * (1.0 / MOE_ROWS)).astype(jnp.int32),
                                      nused_ref.shape)
    diag = (lax.broadcasted_iota(jnp.int32, (N_EXPERTS, 128), 0) == lax.broadcasted_iota(jnp.int32, (N_EXPERTS, 128), 1))
    first_pad = jnp.sum(jnp.where(diag, (pad_end - pad + cnt) * ROW_SUBLANES, 0.0), axis=0, keepdims=True)
    n_pad = jnp.sum(jnp.where(diag, pad - cnt, 0.0), axis=0, keepdims=True)
    padinfo_ref[...] = jnp.zeros(padinfo_ref.shape, jnp.int32)
    padinfo_ref[0:1, :] = first_pad.astype(jnp.int32)
    padinfo_ref[1:2, :] = n_pad.astype(jnp.int32)
    padinfo_ref[2:3, :] = (pad_end[N_EXPERTS - 1:N_EXPERTS, :] * (1.0 / MOE_ROWS)).astype(jnp.int32)


def _to_row_tiles(ref, x):
    n = x.shape[0]
    for c in range(ROW_SUBLANES):
        ref[pl.ds(c, n, stride=ROW_SUBLANES), :] = x[:, c * 128:(c + 1) * 128]


def _from_row_tiles(ref, n):
    return jnp.concatenate([ref[pl.ds(c, n, stride=ROW_SUBLANES), :] for c in range(ROW_SUBLANES)], axis=1)


def _tile_copy(src_ref, src_row8, dst_ref, dst_row8, sem):
    return pltpu.make_async_copy(src_ref.at[pl.ds(pl.multiple_of(src_row8, ROW_SUBLANES), ROW_SUBLANES)],
                                 dst_ref.at[pl.ds(pl.multiple_of(dst_row8, ROW_SUBLANES), ROW_SUBLANES)], sem)


def _dispatch_kernel(dest_ref, padinfo_ref, x_ref, g_ref, xs_ref, h_sc, zero_sc, sem, blk_sem, *, tile,
                     experts_per_step, n_blocks):
    step = pl.program_id(0)
    _to_row_tiles(h_sc, _rms_rows(x_ref[...], g_ref[...]))
    zero_sc[...] = jnp.zeros(zero_sc.shape, F32)

    def block_copy(b):
        rows = MOE_ROWS * ROW_SUBLANES
        return pltpu.make_async_copy(zero_sc, xs_ref.at[pl.ds(pl.multiple_of(b * rows, rows), rows)], blk_sem)

    for j in range(experts_per_step):
        b = padinfo_ref[2, 0] + step * experts_per_step + j

        @pl.when(b < n_blocks)
        def _():
            block_copy(b).start(priority=1)

    def issue(t, carry):
        _tile_copy(h_sc, t * ROW_SUBLANES, xs_ref, dest_ref[0, 0, t], sem).start()
        _tile_copy(h_sc, t * ROW_SUBLANES, xs_ref, dest_ref[0, 0, tile + t], sem).start(priority=1)
        return carry

    def drain(t, carry):
        _tile_copy(h_sc, 0, xs_ref, 0, sem).wait()
        _tile_copy(h_sc, 0, xs_ref, 0, sem).wait()
        return carry

    lax.fori_loop(0, tile, issue, 0, unroll=DMA_UNROLL)

    n_zero = jnp.int32(0)
    for j in range(experts_per_step):
        e = step * experts_per_step + j
        e_c = jnp.minimum(e, N_EXPERTS - 1)
        first = padinfo_ref[0, e_c]
        count = jnp.where(e < N_EXPERTS, padinfo_ref[1, e_c], 0)

        def zero_row(r, carry, first=first):
            _tile_copy(zero_sc, 0, xs_ref, first + r * ROW_SUBLANES, sem).start()
            return carry

        lax.fori_loop(0, count, zero_row, 0)
        n_zero = n_zero + count

    lax.fori_loop(0, tile, drain, 0, unroll=DMA_UNROLL)
    lax.fori_loop(0, n_zero, lambda r, carry: (_tile_copy(zero_sc, 0, xs_ref, 0, sem).wait(), carry)[1], 0)
    for j in range(experts_per_step):
        b = padinfo_ref[2, 0] + step * experts_per_step + j

        @pl.when(b < n_blocks)
        def _():
            block_copy(b).wait()


def _expert_kernel(blk_ref, nused_ref, xs_ref, wgu_ref, wd_ref, ys_ref):
    b = pl.program_id(0)

    @pl.when(b < nused_ref[0])
    def _():
        x = _from_row_tiles(xs_ref, MOE_ROWS).astype(BF16)
        gu = jnp.dot(x, wgu_ref[0, 0].astype(BF16), preferred_element_type=F32)
        a, u = gu[:, :D_EXPERT], gu[:, D_EXPERT:]
        hidden = (a * jax.nn.sigmoid(a) * u).astype(BF16)
        _to_row_tiles(ys_ref, jnp.dot(hidden, wd_ref[0, 0].astype(BF16), preferred_element_type=F32))

    @pl.when(b >= nused_ref[0])
    def _():
        ys_ref[...] = jnp.zeros(ys_ref.shape, F32)


def _combine_kernel(dest_ref, dest_next_ref, ys_ref, x_ref, w_ref, g_ref, y_ref, buf, sem, *, tile, final_norm):
    step = pl.program_id(0)
    slot = lax.rem(step, 2)

    def issue(d_ref, slot_):
        def body(t, carry):
            _tile_copy(ys_ref, d_ref[0, 0, t], buf.at[slot_, 0], t * ROW_SUBLANES, sem.at[slot_]).start()
            _tile_copy(ys_ref, d_ref[0, 0, tile + t], buf.at[slot_, 1], t * ROW_SUBLANES,
                       sem.at[slot_]).start(priority=1)
            return carry
        lax.fori_loop(0, tile, body, 0, unroll=DMA_UNROLL)

    @pl.when(step == 0)
    def _():
        issue(dest_ref, 0)

    @pl.when(step + 1 < pl.num_programs(0))
    def _():
        issue(dest_next_ref, 1 - slot)

    def drain(t, carry):
        _tile_copy(ys_ref, 0, buf.at[slot, 0], 0, sem.at[slot]).wait()
        _tile_copy(ys_ref, 0, buf.at[slot, 1], 0, sem.at[slot]).wait()
        return carry

    lax.fori_loop(0, tile, drain, 0, unroll=DMA_UNROLL)
    w = w_ref[...]
    y = x_ref[...] + (w[:, 0:1] * _from_row_tiles(buf.at[slot, 0], tile)
                      + w[:, 1:2] * _from_row_tiles(buf.at[slot, 1], tile))
    if final_norm:
        y = _rms_rows(y, g_ref[...])
    y_ref[...] = y


def _moe(x, g_ffn, wrT, br, w_gate_up, w_down, layer, g_final, final_norm):
    n_tok = x.shape[0]
    tile = min(TOK_TILE, n_tok)
    n_tiles = n_tok // tile
    n_blocks = (2 * n_tok + MOE_ROWS - 1) // MOE_ROWS + N_EXPERTS
    n_blocks_padded = (n_blocks + 127) // 128 * 128
    n_rows = n_blocks * MOE_ROWS

    ri, rw, cnt = pl.pallas_call(
        _router_kernel,
        grid=(n_tiles,),
        in_specs=[
            pl.BlockSpec((tile, D_MODEL), lambda i: (i, 0)),
            _const_spec((1, D_MODEL)),
            _const_spec((2, ROUTER_LANES, D_MODEL)),
            _const_spec((ROUTER_LANES, 1)),
        ],
        out_specs=[
            pl.BlockSpec((8, tile), lambda i: (0, i)),
            pl.BlockSpec((8, tile), lambda i: (0, i)),
            _const_spec((N_EXPERTS, 128)),
        ],
        out_shape=[
            jax.ShapeDtypeStruct((8, n_tok), jnp.int32),
            jax.ShapeDtypeStruct((8, n_tok), F32),
            jax.ShapeDtypeStruct((N_EXPERTS, 128), F32),
        ],
        scratch_shapes=[pltpu.VMEM((N_EXPERTS, 1), F32), pltpu.VMEM((tile, tile), BF16)],
        compiler_params=_cparams(1),
        name="moe_router",
    )(x, g_ffn, wrT, br)

    dest, blk, nused, padinfo = pl.pallas_call(
        functools.partial(_dest_kernel, n_blocks_padded=n_blocks_padded),
        grid=(n_tiles,),
        in_specs=[_const_spec((N_EXPERTS, 128)), pl.BlockSpec((8, tile), lambda i: (0, i))],
        out_specs=[
            pl.BlockSpec((8, tile), lambda i: (0, i)),
            _const_spec((8, n_blocks_padded)),
            _const_spec((8, 128)),
            _const_spec((8, 128)),
        ],
        out_shape=[
            jax.ShapeDtypeStruct((8, n_tok), jnp.int32),
            jax.ShapeDtypeStruct((8, n_blocks_padded), jnp.int32),
            jax.ShapeDtypeStruct((8, 128), jnp.int32),
            jax.ShapeDtypeStruct((8, 128), jnp.int32),
        ],
        compiler_params=_cparams(1),
        name="moe_dest",
    )(cnt, ri)

    dest_tiles = dest[0:2].reshape(2, n_tiles, tile).transpose(1, 0, 2).reshape(n_tiles, 1, 2 * tile)
    dest_spec = pl.BlockSpec((1, 1, 2 * tile), lambda i: (i, 0, 0), memory_space=pltpu.SMEM)
    any_spec = pl.BlockSpec(memory_space=pl.ANY)

    xs = pl.pallas_call(
        functools.partial(_dispatch_kernel, tile=tile, experts_per_step=pl.cdiv(N_EXPERTS, n_tiles),
                          n_blocks=n_blocks),
        grid=(n_tiles,),
        in_specs=[dest_spec, pl.BlockSpec((8, 128), lambda i: (0, 0), memory_space=pltpu.SMEM),
                  pl.BlockSpec((tile, D_MODEL), lambda i: (i, 0)), _const_spec((1, D_MODEL))],
        out_specs=any_spec,
        out_shape=jax.ShapeDtypeStruct((n_rows * ROW_SUBLANES, 128), F32),
        scratch_shapes=[pltpu.VMEM((tile * ROW_SUBLANES, 128), F32),
                        pltpu.VMEM((MOE_ROWS * ROW_SUBLANES, 128), F32),
                        pltpu.SemaphoreType.DMA(()), pltpu.SemaphoreType.DMA(())],
        compiler_params=_cparams(1),
        name="moe_dispatch",
    )(dest_tiles, padinfo, x, g_ffn)

    blk_rows = MOE_ROWS * ROW_SUBLANES
    ys = pl.pallas_call(
        _expert_kernel,
        grid_spec=pltpu.PrefetchScalarGridSpec(
            num_scalar_prefetch=2,
            grid=(n_blocks,),
            in_specs=[
                pl.BlockSpec((blk_rows, 128), lambda b, blk, nu: (b, 0)),
                pl.BlockSpec((1, 1, D_MODEL, 2 * D_EXPERT), lambda b, blk, nu: (layer, blk[b], 0, 0)),
                pl.BlockSpec((1, 1, D_EXPERT, D_MODEL), lambda b, blk, nu: (layer, blk[b], 0, 0)),
            ],
            out_specs=pl.BlockSpec((blk_rows, 128), lambda b, blk, nu: (b, 0)),
        ),
        out_shape=jax.ShapeDtypeStruct((n_rows * ROW_SUBLANES, 128), F32),
        compiler_params=_cparams(1),
        name="moe_experts",
    )(blk[0, :n_blocks], nused[0, :1], xs, w_gate_up, w_down)

    ctile = min(COMBINE_TILE, n_tok)
    c_tiles = n_tok // ctile
    dest_c = dest[0:2].reshape(2, c_tiles, ctile).transpose(1, 0, 2).reshape(c_tiles, 1, 2 * ctile)
    return pl.pallas_call(
        functools.partial(_combine_kernel, tile=ctile, final_norm=final_norm),
        grid=(c_tiles,),
        in_specs=[
            pl.BlockSpec((1, 1, 2 * ctile), lambda i: (i, 0, 0), memory_space=pltpu.SMEM),
            pl.BlockSpec((1, 1, 2 * ctile), lambda i: (jnp.minimum(i + 1, c_tiles - 1), 0, 0),
                         memory_space=pltpu.SMEM),
            any_spec,
            pl.BlockSpec((ctile, D_MODEL), lambda i: (i, 0)),
            pl.BlockSpec((ctile, 2), lambda i: (i, 0)),
            _const_spec((1, D_MODEL)),
        ],
        out_specs=pl.BlockSpec((ctile, D_MODEL), lambda i: (i, 0)),
        out_shape=jax.ShapeDtypeStruct((n_tok, D_MODEL), F32),
        scratch_shapes=[pltpu.VMEM((2, 2, ctile * ROW_SUBLANES, 128), F32), pltpu.SemaphoreType.DMA((2,))],
        compiler_params=_cparams(1),
        name="moe_combine",
    )(dest_c, dest_c, ys, x, rw[0:2].T, g_final)


def _rope_tables(n_tokens):
    rows = n_tokens // GRID_W
    row_idx = jnp.repeat(jnp.arange(rows, dtype=F32), GRID_W)
    col_idx = jnp.tile(jnp.arange(GRID_W, dtype=F32), rows)

    def angles(rot_dim):
        n_freq = rot_dim // 4
        inv_freq = ROPE_THETA ** (-jnp.arange(n_freq, dtype=F32) / n_freq)
        return jnp.concatenate([row_idx[:, None] * inv_freq, col_idx[:, None] * inv_freq], axis=-1)

    ag, am = angles(GQA_HD), angles(MLA_ROPE)
    cg, sg, cm, sm = jnp.cos(ag), jnp.sin(ag), jnp.cos(am), jnp.sin(am)
    hr = MLA_ROPE // 2
    zl = jnp.zeros((n_tokens, MLA_NOPE), F32)
    zr = jnp.zeros((n_tokens, HEAD_PAD - MLA_NOPE - MLA_ROPE), F32)
    del hr
    return dict(
        cT_g=cg.T, sT_g=sg.T,
        cF_g=jnp.concatenate([cg, cg], axis=1), sF_g=jnp.concatenate([-sg, sg], axis=1),
        cT_m=cm.T, sT_m=sm.T,
        cP_m=jnp.concatenate([zl, cm, cm, zr], axis=1), sP_m=jnp.concatenate([zl, -sm, sm, zr], axis=1),
    )


def _hi_lo(w):
    hi = w.astype(BF16)
    return jnp.stack([hi, (w - hi.astype(F32)).astype(BF16)])


def _layer_params(l, norm_mix, norm_mem, w_in, gqa_q_norm, gqa_k_norm, mla_q_a_norm, mla_kv_a_norm, mla_w_qb,
                  mla_w_kvb, mem_w_kv, w_out, norm_ffn, w_group, b_group, w_expert, b_expert):
    splits = []
    off = 0
    for width in (GQA_HEADS * GQA_HD, GQA_KV_HEADS * GQA_HD, GQA_KV_HEADS * GQA_HD, MLA_Q_LORA, MLA_KV_LORA,
                  MLA_ROPE, MEM_HEADS * MEM_HD, N_BRANCH * D_MODEL):
        splits.append(w_in[l][:, off:off + width])
        off += width
    wq_g, wk_g, wv_g, wq_a, wkv_a, wk_r, wq_c, w_gate = splits
    hr = MLA_ROPE // 2
    pad_l = jnp.zeros((D_MODEL, MLA_NOPE), F32)
    pad_r = jnp.zeros((D_MODEL, HEAD_PAD - MLA_NOPE - MLA_ROPE), F32)
    wk_r_swapped = jnp.concatenate([wk_r[:, hr:], wk_r[:, :hr]], axis=1)
    wkr2 = jnp.concatenate([pad_l, wk_r, pad_r, pad_l, wk_r_swapped, pad_r], axis=1)
    qk = MLA_NOPE + MLA_ROPE
    wqb = jnp.pad(mla_w_qb[l].reshape(MLA_Q_LORA, MLA_HEADS, qk), ((0, 0), (0, 0), (0, HEAD_PAD - qk)))
    wkvb = mla_w_kvb[l].reshape(MLA_KV_LORA, MLA_HEADS, MLA_NOPE + MLA_V)
    wkn = jnp.pad(wkvb[:, :, :MLA_NOPE], ((0, 0), (0, 0), (0, HEAD_PAD - MLA_NOPE)))
    wv_m = wkvb[:, :, MLA_NOPE:]
    n_router = N_GROUPS + N_EXPERTS
    return dict(
        g_mix=norm_mix[l][None, :], g_mem=norm_mem[l][None, :], g_ffn=norm_ffn[l][None, :],
        wqT_g=wq_g.T.astype(BF16), wk_g=wk_g.astype(BF16), wvT_g=wv_g.T.astype(BF16),
        gq=gqa_q_norm[l][:, None], gk=gqa_k_norm[l][None, :],
        wqaT=wq_a.T.astype(BF16), gqa=mla_q_a_norm[l][:, None],
        wqbT=wqb.reshape(MLA_Q_LORA, MLA_HEADS * HEAD_PAD).T.astype(BF16),
        wkva=wkv_a.astype(BF16), gkva=mla_kv_a_norm[l][None, :],
        wkn=wkn.reshape(MLA_KV_LORA, MLA_HEADS * HEAD_PAD).astype(BF16), wkr2=wkr2.astype(BF16),
        wvT_m=wv_m.reshape(MLA_KV_LORA, MLA_HEADS * MLA_V).T.astype(BF16),
        wqcT=wq_c.T.astype(BF16), wgT=w_gate.T.astype(BF16),
        wmk=mem_w_kv[l][:, :MEM_HEADS * MEM_HD].astype(BF16),
        wmvT=mem_w_kv[l][:, MEM_HEADS * MEM_HD:].T.astype(BF16),
        w_out=w_out[l].astype(BF16),
        wrT=_hi_lo(jnp.concatenate([w_group[l].T, w_expert[l].T,
                                    jnp.zeros((ROUTER_LANES - n_router, D_MODEL), F32)], axis=0)),
        br=jnp.concatenate([b_group[l], b_expert[l], jnp.zeros((ROUTER_LANES - n_router,), F32)])[:, None],
    )


def _mixer(x, mem, p, tabs):
    b, s, _ = x.shape
    tile = min(TOK_TILE, s)
    n_t = s // tile
    grid = (b, n_t)
    x_spec = pl.BlockSpec((1, tile, D_MODEL), lambda bi, i: (bi, i, 0))
    colT = lambda rows: pl.BlockSpec((rows, tile), lambda bi, i: (0, i))
    rowF = pl.BlockSpec((tile, HEAD_PAD), lambda bi, i: (i, 0))
    headT = lambda heads: pl.BlockSpec((1, heads, 1, HEAD_PAD, tile), lambda bi, i: (bi, 0, i, 0, 0))
    keyF = lambda heads: pl.BlockSpec((1, heads, tile, HEAD_PAD), lambda bi, i: (bi, 0, i, 0))
    valT = lambda heads: pl.BlockSpec((1, heads, 1, V_ROWS, tile), lambda bi, i: (bi, 0, i, 0, 0))

    gqa_in = [(p["wqT_g"], _const_spec(p["wqT_g"].shape)), (p["wk_g"], _const_spec(p["wk_g"].shape)),
              (p["wvT_g"], _const_spec(p["wvT_g"].shape)), (p["gq"], _const_spec((GQA_HD, 1))),
              (p["gk"], _const_spec((1, GQA_HD))), (tabs["cT_g"], colT(GQA_HD // 2)),
              (tabs["sT_g"], colT(GQA_HD // 2)), (tabs["cF_g"], rowF), (tabs["sF_g"], rowF)]
    mla_in = [(p["wqaT"], _const_spec(p["wqaT"].shape)), (p["gqa"], _const_spec((MLA_Q_LORA, 1))),
              (p["wqbT"], _const_spec(p["wqbT"].shape)), (p["wkva"], _const_spec(p["wkva"].shape)),
              (p["gkva"], _const_spec((1, MLA_KV_LORA))), (p["wkn"], _const_spec(p["wkn"].shape)),
              (p["wkr2"], _const_spec(p["wkr2"].shape)), (p["wvT_m"], _const_spec(p["wvT_m"].shape)),
              (tabs["cT_m"], colT(MLA_ROPE // 2)), (tabs["sT_m"], colT(MLA_ROPE // 2)),
              (tabs["cP_m"], rowF), (tabs["sP_m"], rowF)]
    assert len(gqa_in) == 9 and len(mla_in) == 12
    qkv_shapes = lambda heads, kv_heads: [jax.ShapeDtypeStruct((b, heads, n_t, HEAD_PAD, tile), BF16),
                                          jax.ShapeDtypeStruct((b, kv_heads, s, HEAD_PAD), BF16),
                                          jax.ShapeDtypeStruct((b, kv_heads, n_t, V_ROWS, tile), BF16)]
    qg, kg, vg, qm, km, vm = pl.pallas_call(
        _proj_attn_kernel,
        grid=grid,
        in_specs=[x_spec, _const_spec((1, D_MODEL))] + [spec for _, spec in gqa_in + mla_in],
        out_specs=[headT(GQA_HEADS), keyF(GQA_KV_HEADS), valT(GQA_KV_HEADS),
                   headT(MLA_HEADS), keyF(MLA_HEADS), valT(MLA_HEADS)],
        out_shape=qkv_shapes(GQA_HEADS, GQA_KV_HEADS) + qkv_shapes(MLA_HEADS, MLA_HEADS),
        compiler_params=_cparams(2),
        name="proj_attn",
    )(x, p["g_mix"], *[arr for arr, _ in gqa_in + mla_in])

    n_mem = mem.shape[1]
    mk, mvT = pl.pallas_call(
        _mem_kv_kernel,
        grid=(b,),
        in_specs=[pl.BlockSpec((1, n_mem, D_MODEL), lambda bi: (bi, 0, 0)), _const_spec((1, D_MODEL)),
                  _const_spec(p["wmk"].shape), _const_spec(p["wmvT"].shape)],
        out_specs=[pl.BlockSpec((1, n_mem, D_MODEL), lambda bi: (bi, 0, 0)),
                   pl.BlockSpec((1, D_MODEL, n_mem), lambda bi: (bi, 0, 0))],
        out_shape=[jax.ShapeDtypeStruct((b, n_mem, D_MODEL), BF16), jax.ShapeDtypeStruct((b, D_MODEL, n_mem), BF16)],
        compiler_params=_cparams(1),
        name="mem_kv",
    )(mem, p["g_mem"], p["wmk"], p["wmvT"])

    oa = _attention(qg, kg, vg, GQA_HEADS // GQA_KV_HEADS)
    ob = _attention(qm, km, vm, 1)
    return pl.pallas_call(
        _mixer_tail_kernel,
        grid=grid,
        in_specs=[x_spec, _const_spec((1, D_MODEL)), headT(GQA_HEADS), headT(MLA_HEADS),
                  pl.BlockSpec((1, n_mem, D_MODEL), lambda bi, i: (bi, 0, 0)),
                  pl.BlockSpec((1, D_MODEL, n_mem), lambda bi, i: (bi, 0, 0)),
                  _const_spec(p["wqcT"].shape), _const_spec(p["wgT"].shape), _const_spec((D_MODEL, D_MODEL))],
        out_specs=x_spec,
        out_shape=jax.ShapeDtypeStruct(x.shape, F32),
        compiler_params=_cparams(2),
        name="mixer_tail",
    )(x, p["g_mix"], oa, ob, mk, mvT, p["wqcT"], p["wgT"], p["w_out"])


def _trunk(x, mem, layers, w_gate_up, w_down, g_final):
    b, s, d = x.shape
    tabs = _rope_tables(s)
    for l, p in enumerate(layers):
        x = _mixer(x, mem, p, tabs)
        x = _moe(x.reshape(b * s, d), p["g_ffn"], p["wrT"], p["br"], w_gate_up, w_down, l, g_final,
                 final_norm=(l == len(layers) - 1)).reshape(b, s, d)
    return x


def kernel(x_prompt, x_sample, mem_prompt, mem_sample, norm_mix, norm_mem, w_in, gqa_q_norm, gqa_k_norm, mla_q_a_norm, mla_kv_a_norm, mla_w_qb, mla_w_kvb, mem_w_kv, w_out, norm_ffn, w_group, b_group, w_expert, b_expert, w_gate_up, w_down, norm_final):
    depth = w_in.shape[0]
    layers = [_layer_params(l, norm_mix, norm_mem, w_in, gqa_q_norm, gqa_k_norm, mla_q_a_norm, mla_kv_a_norm,
                            mla_w_qb, mla_w_kvb, mem_w_kv, w_out, norm_ffn, w_group, b_group, w_expert, b_expert)
              for l in range(depth)]
    g_final = norm_final[None, :]
    return (_trunk(x_prompt, mem_prompt, layers, w_gate_up, w_down, g_final),
            _trunk(x_sample, mem_sample, layers, w_gate_up, w_down, g_final))
```

```python
import functools
import math

import jax
import jax.numpy as jnp
from jax import lax
from jax.experimental import pallas as pl
from jax.experimental.pallas import tpu as pltpu

F32 = jnp.float32
BF16 = jnp.bfloat16

D_MODEL = 1024
EPS = 1e-6
GRID_W = 64
ROPE_THETA = 10000.0
GQA_HEADS, GQA_KV_HEADS, GQA_HD = 8, 2, 128
MLA_HEADS, MLA_Q_LORA, MLA_KV_LORA = 8, 384, 256
MLA_NOPE, MLA_ROPE, MLA_V = 64, 32, 128
MEM_HEADS, MEM_HD = 4, 256
N_BRANCH = 3
N_GROUPS, EXPERTS_PER_GROUP, D_EXPERT = 8, 8, 256
N_EXPERTS = N_GROUPS * EXPERTS_PER_GROUP
HEAD_PAD = 128
BF16_SUBLANES = 16
V_ROWS = HEAD_PAD + BF16_SUBLANES
LOG2E = 1.4426950408889634

TOK_TILE = 512
ATT_Q_COLS = 4096
ATT_CHUNKS_PER_BODY = 2
MOE_ROWS = 256
COMBINE_TILE = 256
XS_RING = 3
ROUTER_LANES = 128
ROW_SUBLANES = D_MODEL // 128
DMA_UNROLL = 8
VMEM_LIMIT = 56 * 1024 * 1024


def _cparams(n_axes, flags=None):
    return pltpu.CompilerParams(dimension_semantics=("arbitrary",) * n_axes, vmem_limit_bytes=VMEM_LIMIT,
                                flags=flags)


def _nt(a, b):
    return lax.dot_general(a, b, (((1,), (1,)), ((), ())), preferred_element_type=F32)


def _rms_rows(x, g):
    return x * lax.rsqrt(jnp.mean(x * x, axis=-1, keepdims=True) + EPS) * g


def _rms_cols(x, g):
    return x * lax.rsqrt(jnp.mean(x * x, axis=0, keepdims=True) + EPS) * g


def _const_spec(shape):
    return pl.BlockSpec(shape, lambda *_: (0,) * len(shape))


def _store_values(v_ref, vT):
    t = vT.shape[1]
    v_ref[0:HEAD_PAD, :] = vT.astype(BF16)
    ones_row = lax.broadcasted_iota(jnp.int32, (BF16_SUBLANES, t), 0) == 0
    v_ref[HEAD_PAD:V_ROWS, :] = jnp.where(ones_row, 1.0, 0.0).astype(BF16)


def _proj_attn_kernel(x_ref, g_ref, *refs):
    h = _rms_rows(x_ref[0], g_ref[...]).astype(BF16)
    n_g, n_m = 9, 12
    _proj_mla(h, *refs[n_g:n_g + n_m], *refs[n_g + n_m + 3:])
    _proj_gqa(h, *refs[:n_g], *refs[n_g + n_m:n_g + n_m + 3])


def _proj_gqa(h, wqT_ref, wk_ref, wvT_ref, gq_ref, gk_ref, cT_ref, sT_ref, cF_ref, sF_ref, q_ref, k_ref, v_ref):
    c, s, gq = cT_ref[...], sT_ref[...], gq_ref[...]
    half = GQA_HD // 2
    scale = GQA_HD ** -0.5 * LOG2E
    heads_per_dot = 4
    k = jnp.dot(h, wk_ref[...], preferred_element_type=F32)
    q_pieces = []
    for h0 in range(0, GQA_HEADS, heads_per_dot):
        q_pieces.append(_nt(wqT_ref[h0 * GQA_HD:(h0 + heads_per_dot) * GQA_HD, :], h))
        if h0 == 0:
            for j in range(GQA_KV_HEADS):
                kn = _rms_rows(k[:, j * GQA_HD:(j + 1) * GQA_HD], gk_ref[...])
                ko = kn * cF_ref[...] + pltpu.roll(kn, half, 1) * sF_ref[...]
                k_ref[0, j] = ko.astype(BF16)
    vT = _nt(wvT_ref[...], h)
    for piece, qT in enumerate(q_pieces):
        for hd in range(heads_per_dot):
            qn = _rms_cols(qT[hd * GQA_HD:(hd + 1) * GQA_HD], gq)
            x1, x2 = qn[:half], qn[half:]
            o = jnp.concatenate([x1 * c - x2 * s, x2 * c + x1 * s], axis=0) * scale
            q_ref[0, piece * heads_per_dot + hd, 0] = o.astype(BF16)
    for j in range(GQA_KV_HEADS):
        _store_values(v_ref.at[0, j, 0], vT[j * GQA_HD:(j + 1) * GQA_HD])


def _proj_mla(h, wqaT_ref, gqa_ref, wqbT_ref, wkva_ref, gkva_ref, wkn_ref, wkr_ref, wvT_ref,
              cT_ref, sT_ref, cP_ref, sP_ref, q_ref, k_ref, v_ref):
    qa_raw = _nt(wqaT_ref[...], h)
    kva = jnp.dot(h, wkva_ref[...], preferred_element_type=F32)
    kr2 = jnp.dot(h, wkr_ref[...], preferred_element_type=F32)
    qa = _rms_cols(qa_raw, gqa_ref[...]).astype(BF16)
    qT = jnp.dot(wqbT_ref[...], qa, preferred_element_type=F32)
    kvn = _rms_rows(kva, gkva_ref[...]).astype(BF16)
    kn = jnp.dot(kvn, wkn_ref[...], preferred_element_type=F32)
    kr = kr2[:, :HEAD_PAD] * cP_ref[...] + kr2[:, HEAD_PAD:] * sP_ref[...]
    c, s = cT_ref[...], sT_ref[...]
    hr = MLA_ROPE // 2
    scale = (MLA_NOPE + MLA_ROPE) ** -0.5 * LOG2E
    for hd in range(MLA_HEADS):
        q = qT[hd * HEAD_PAD:(hd + 1) * HEAD_PAD]
        x1, x2 = q[MLA_NOPE:MLA_NOPE + hr], q[MLA_NOPE + hr:MLA_NOPE + 2 * hr]
        o = jnp.concatenate([q[:MLA_NOPE], x1 * c - x2 * s, x2 * c + x1 * s, q[MLA_NOPE + 2 * hr:]], axis=0) * scale
        q_ref[0, hd, 0] = o.astype(BF16)
    vT = _nt(wvT_ref[...], kvn)
    for hd in range(MLA_HEADS):
        k_ref[0, hd] = (kn[:, hd * HEAD_PAD:(hd + 1) * HEAD_PAD] + kr).astype(BF16)
    for hd in range(MLA_HEADS):
        _store_values(v_ref.at[0, hd, 0], vT[hd * MLA_V:(hd + 1) * MLA_V])


def _mem_kv_kernel(mem_ref, g_ref, wk_ref, wvT_ref, mk_ref, mvT_ref):
    h = _rms_rows(mem_ref[0], g_ref[...]).astype(BF16)
    mk_ref[0] = jnp.dot(h, wk_ref[...], preferred_element_type=F32).astype(BF16)
    mvT_ref[0] = _nt(wvT_ref[...], h).astype(BF16)


def _attn_kernel(q_ref, k_ref, v_ref, o_ref, m_sc, acc_sc, s_sc, mc_sc, *, chunks_per_head, n_kv):
    chunk = TOK_TILE
    n_sub = m_sc.shape[0]

    m_sc[...] = jnp.full(m_sc.shape, -jnp.inf, F32)
    acc_sc[...] = jnp.zeros(acc_sc.shape, F32)

    def scores(c, u, slot):
        k = k_ref[0, 0, pl.ds(pl.multiple_of(c * chunk, chunk), chunk), :]
        s = jnp.dot(k, q_ref[0, u // chunks_per_head, u % chunks_per_head],
                    preferred_element_type=F32)
        s_sc[slot, u] = s
        mc_sc[slot, u] = jnp.max(s, axis=0, keepdims=True)

    def softmax_values(c, u, slot):
        m_prev = m_sc[u]
        m_cur = jnp.maximum(m_prev, mc_sc[slot, u])
        alpha = jnp.exp2(m_prev - m_cur)
        p = jnp.exp2(s_sc[slot, u] - m_cur)
        acc_sc[u] = alpha * acc_sc[u] + jnp.dot(v_ref[0, 0, c], p.astype(BF16), preferred_element_type=F32)
        m_sc[u] = m_cur

    def stage(c_scores, c_values):
        for u in range(n_sub):
            if c_scores is not None:
                scores(c_scores, u, c_scores % 2)
            if c_values is not None:
                softmax_values(c_values, u, c_values % 2)

    per_body = ATT_CHUNKS_PER_BODY if (n_kv - 1) // ATT_CHUNKS_PER_BODY >= 2 else 2
    n_bodies = (n_kv - 1) // per_body
    stage(0, None)

    def body(i, carry):
        for t in range(per_body):
            for u in range(n_sub):
                scores(per_body * i + t + 1, u, (t + 1) % 2)
                softmax_values(per_body * i + t, u, t % 2)
        return carry

    lax.fori_loop(0, n_bodies, body, 0)
    for c in range(per_body * n_bodies, n_kv - 1):
        stage(c + 1, c)
    stage(None, n_kv - 1)

    for u in range(n_sub):
        o_ref[0, u // chunks_per_head, u % chunks_per_head] = (
            acc_sc[u, 0:HEAD_PAD] / acc_sc[u, HEAD_PAD:HEAD_PAD + 1]).astype(BF16)


def _attention(qT, k, vT, group):
    b, h, n_kv, hd, chunk = qT.shape
    hkv = k.shape[1]
    s = n_kv * chunk
    assert h == hkv * group and chunk == TOK_TILE
    chunks_per_head = min(ATT_Q_COLS // group, s) // chunk
    n_sub = group * chunks_per_head
    kern = functools.partial(_attn_kernel, chunks_per_head=chunks_per_head, n_kv=n_kv)
    q_spec = pl.BlockSpec((1, group, chunks_per_head, hd, chunk), lambda bi, j, qi: (bi, j, qi, 0, 0))
    return pl.pallas_call(
        kern,
        grid=(b, hkv, n_kv // chunks_per_head),
        in_specs=[
            q_spec,
            pl.BlockSpec((1, 1, s, hd), lambda bi, j, qi: (bi, j, 0, 0)),
            pl.BlockSpec((1, 1, n_kv, V_ROWS, chunk), lambda bi, j, qi: (bi, j, 0, 0, 0)),
        ],
        out_specs=q_spec,
        out_shape=jax.ShapeDtypeStruct(qT.shape, BF16),
        scratch_shapes=[
            pltpu.VMEM((n_sub, 1, chunk), F32),
            pltpu.VMEM((n_sub, V_ROWS, chunk), F32),
            pltpu.VMEM((2, n_sub, chunk, chunk), F32),
            pltpu.VMEM((2, n_sub, 1, chunk), F32),
        ],
        compiler_params=_cparams(3),
        name="attention",
    )(qT, k, vT)


def _mixer_tail_kernel(x_ref, g_ref, oa_ref, ob_ref, mk_ref, mvT_ref, wqcT_ref, wgT_ref, w_ref, y_ref):
    t = x_ref.shape[1]
    x = x_ref[0]
    h = _rms_rows(x, g_ref[...]).astype(BF16)
    scale = MEM_HD ** -0.5 * LOG2E
    piece = 2 * MEM_HD
    out = None
    for p0 in range(0, D_MODEL, piece):
        rows = slice(p0, p0 + piece)
        heads = [slice(p0, p0 + MEM_HD), slice(p0 + MEM_HD, p0 + piece)]
        gate_logits = lambda b: _nt(wgT_ref[b * D_MODEL + p0:b * D_MODEL + p0 + piece, :], h)
        q = [(_nt(wqcT_ref[r, :], h) * scale).astype(BF16) for r in heads]
        s = []
        g = []
        for i, r in enumerate(heads):
            s.append(jnp.dot(mk_ref[0, :, r], q[i], preferred_element_type=F32))
            g.append(gate_logits(i))
        oc = []
        for i, r in enumerate(heads):
            pr = jnp.exp2(s[i] - jnp.max(s[i], axis=0, keepdims=True))
            l = jnp.sum(pr, axis=0, keepdims=True)
            oc.append(jnp.dot(mvT_ref[0, r, :], pr.astype(BF16), preferred_element_type=F32) / l)
            if i == 0:
                g.append(gate_logits(2))
        oa = oa_ref[0].reshape(D_MODEL, t)[rows].astype(F32)
        ob = ob_ref[0].reshape(D_MODEL, t)[rows].astype(F32)
        merged = (jax.nn.sigmoid(g[0]) * oa + jax.nn.sigmoid(g[1]) * ob
                  + jax.nn.sigmoid(g[2]) * jnp.concatenate(oc, axis=0)).astype(BF16)
        part = lax.dot_general(merged, w_ref[rows, :], (((0,), (0,)), ((), ())), preferred_element_type=F32)
        out = part if out is None else out + part
    y_ref[0] = x + out


def _first_index_of_max(v, vmax, iota, n):
    return jnp.min(jnp.where(v == vmax, iota, n), axis=0, keepdims=True)


def _router_kernel(x_ref, g_ref, wT_ref, b_ref, ri_ref, rw_ref, cnt_ref, carry_sc, before_sc):
    step = pl.program_id(0)
    t = x_ref.shape[0]

    @pl.when(step == 0)
    def _():
        carry_sc[...] = jnp.zeros(carry_sc.shape, F32)
        before = lax.broadcasted_iota(jnp.int32, (t, t), 0) < lax.broadcasted_iota(jnp.int32, (t, t), 1)
        before_sc[...] = jnp.where(before, 1.0, 0.0).astype(BF16)

    h = _rms_rows(x_ref[...], g_ref[...])
    h_hi = h.astype(BF16)
    h_lo = (h - h_hi.astype(F32)).astype(BF16)
    w_hi, w_lo = wT_ref[0], wT_ref[1]
    logits = _nt(w_hi, h_hi) + _nt(w_hi, h_lo) + _nt(w_lo, h_hi) + b_ref[...]
    iota_g = lax.broadcasted_iota(jnp.int32, (N_GROUPS, t), 0)
    gl = logits[0:N_GROUPS]
    gmax = jnp.max(gl, axis=0, keepdims=True)
    gidx = _first_index_of_max(gl, gmax, iota_g, N_GROUPS)
    gw = 1.0 / jnp.sum(jnp.exp(gl - gmax), axis=0, keepdims=True)
    sel = jnp.zeros((EXPERTS_PER_GROUP, t), F32)
    for g in range(N_GROUPS):
        r0 = N_GROUPS + g * EXPERTS_PER_GROUP
        sel = sel + jnp.where(gidx == g, logits[r0:r0 + EXPERTS_PER_GROUP], 0.0)
    v1 = jnp.max(sel, axis=0, keepdims=True)
    i1 = _first_index_of_max(sel, v1, iota_g, EXPERTS_PER_GROUP)
    sel2 = jnp.where(iota_g == i1, -jnp.inf, sel)
    v2 = jnp.max(sel2, axis=0, keepdims=True)
    i2 = _first_index_of_max(sel2, v2, iota_g, EXPERTS_PER_GROUP)
    e = jnp.exp(v2 - v1)
    w1 = gw / (1.0 + e)
    w2 = gw * e / (1.0 + e)
    e1 = gidx * EXPERTS_PER_GROUP + i1
    e2 = gidx * EXPERTS_PER_GROUP + i2

    iota_e = lax.broadcasted_iota(jnp.int32, (N_EXPERTS, t), 0)
    hit1 = iota_e == e1
    hit2 = iota_e == e2
    hits = jnp.where(hit1 | hit2, 1.0, 0.0)
    prefix = jnp.dot(hits.astype(BF16), before_sc[...],
                     preferred_element_type=F32) + carry_sc[...]
    r1 = jnp.sum(jnp.where(hit1, prefix, 0.0), axis=0, keepdims=True)
    r2 = jnp.sum(jnp.where(hit2, prefix, 0.0), axis=0, keepdims=True)
    carry_sc[...] = carry_sc[...] + jnp.sum(hits, axis=1, keepdims=True)

    ri_ref[...] = jnp.zeros(ri_ref.shape, jnp.int32)
    ri_ref[0:1, :] = e1
    ri_ref[1:2, :] = e2
    ri_ref[2:3, :] = r1.astype(jnp.int32)
    ri_ref[3:4, :] = r2.astype(jnp.int32)
    rw_ref[...] = jnp.zeros(rw_ref.shape, F32)
    rw_ref[0:1, :] = w1
    rw_ref[1:2, :] = w2
    cnt_ref[...] = jnp.broadcast_to(carry_sc[...], cnt_ref.shape)


def _dest_kernel(cnt_ref, ri_ref, dest_ref, blk_ref, nused_ref, padinfo_ref, *, n_blocks_padded):
    cnt = cnt_ref[...]
    pad = jnp.floor((cnt + (MOE_ROWS - 1)) * (1.0 / MOE_ROWS)) * MOE_ROWS
    lower = jnp.where(lax.broadcasted_iota(jnp.int32, (N_EXPERTS, N_EXPERTS), 1)
                      <= lax.broadcasted_iota(jnp.int32, (N_EXPERTS, N_EXPERTS), 0), 1.0, 0.0)
    pad_end = jnp.dot(lower, pad, precision=lax.Precision.HIGHEST, preferred_element_type=F32)
    pad_start = (pad_end - pad)[:, 0:1]
    t = ri_ref.shape[1]
    iota_e = lax.broadcasted_iota(jnp.int32, (N_EXPERTS, t), 0)
    dest_ref[...] = jnp.zeros(dest_ref.shape, jnp.int32)
    for kk in range(2):
        start = jnp.sum(jnp.where(iota_e == ri_ref[kk:kk + 1, :], pad_start, 0.0), axis=0, keepdims=True)
        dest_ref[kk:kk + 1, :] = (start.astype(jnp.int32) + ri_ref[2 + kk:3 + kk, :]) * ROW_SUBLANES
    blk_start = (lax.broadcasted_iota(jnp.int32, (N_EXPERTS, n_blocks_padded), 1) * MOE_ROWS).astype(F32)
    owner = jnp.sum(jnp.where(pad_end[:, 0:1] <= blk_start, 1.0, 0.0), axis=0, keepdims=True)
    blk_ref[...] = jnp.broadcast_to(jnp.minimum(owner, N_EXPERTS - 1.0).astype(jnp.int32), blk_ref.shape)
    nused_ref[...] = jnp.broadcast_to((pad_end[N_EXPERTS - 1:N_EXPERTS, :] * (1.0 / MOE_ROWS)).astype(jnp.int32),
                                      nused_ref.shape)
    diag = (lax.broadcasted_iota(jnp.int32, (N_EXPERTS, 128), 0) == lax.broadcasted_iota(jnp.int32, (N_EXPERTS, 128), 1))
    first_pad = jnp.sum(jnp.where(diag, (pad_end - pad + cnt) * ROW_SUBLANES, 0.0), axis=0, keepdims=True)
    n_pad = jnp.sum(jnp.where(diag, pad - cnt, 0.0), axis=0, keepdims=True)
    padinfo_ref[...] = jnp.zeros(padinfo_ref.shape, jnp.int32)
    padinfo_ref[0:1, :] = first_pad.astype(jnp.int32)
    padinfo_ref[1:2, :] = n_pad.astype(jnp.int32)
    padinfo_ref[2:3, :] = (pad_end[N_EXPERTS - 1:N_EXPERTS, :] * (1.0 / MOE_ROWS)).astype(jnp.int32)


def _to_row_tiles(ref, x):
    n = x.shape[0]
    for c in range(ROW_SUBLANES):
        ref[pl.ds(c, n, stride=ROW_SUBLANES), :] = x[:, c * 128:(c + 1) * 128]


def _from_row_tiles(ref, n):
    return jnp.concatenate([ref[pl.ds(c, n, stride=ROW_SUBLANES), :] for c in range(ROW_SUBLANES)], axis=1)


def _tile_copy(src_ref, src_row8, dst_ref, dst_row8, sem):
    return pltpu.make_async_copy(src_ref.at[pl.ds(pl.multiple_of(src_row8, ROW_SUBLANES), ROW_SUBLANES)],
                                 dst_ref.at[pl.ds(pl.multiple_of(dst_row8, ROW_SUBLANES), ROW_SUBLANES)], sem)


def _dispatch_kernel(dest_ref, padinfo_ref, x_ref, g_ref, xs_ref, h_sc, zero_sc, sem, blk_sem, *, tile,
                     experts_per_step, n_blocks):
    step = pl.program_id(0)
    _to_row_tiles(h_sc, _rms_rows(x_ref[...], g_ref[...]))
    zero_sc[...] = jnp.zeros(zero_sc.shape, F32)

    def block_copy(b):
        rows = MOE_ROWS * ROW_SUBLANES
        return pltpu.make_async_copy(zero_sc, xs_ref.at[pl.ds(pl.multiple_of(b * rows, rows), rows)], blk_sem)

    for j in range(experts_per_step):
        b = padinfo_ref[2, 0] + step * experts_per_step + j

        @pl.when(b < n_blocks)
        def _():
            block_copy(b).start(priority=1)

    def issue(t, carry):
        _tile_copy(h_sc, t * ROW_SUBLANES, xs_ref, dest_ref[0, 0, t], sem).start()
        _tile_copy(h_sc, t * ROW_SUBLANES, xs_ref, dest_ref[0, 0, tile + t], sem).start(priority=1)
        return carry

    def drain(t, carry):
        _tile_copy(h_sc, 0, xs_ref, 0, sem).wait()
        _tile_copy(h_sc, 0, xs_ref, 0, sem).wait()
        return carry

    lax.fori_loop(0, tile, issue, 0, unroll=DMA_UNROLL)

    n_zero = jnp.int32(0)
    for j in range(experts_per_step):
        e = step * experts_per_step + j
        e_c = jnp.minimum(e, N_EXPERTS - 1)
        first = padinfo_ref[0, e_c]
        count = jnp.where(e < N_EXPERTS, padinfo_ref[1, e_c], 0)

        def zero_row(r, carry, first=first):
            _tile_copy(zero_sc, 0, xs_ref, first + r * ROW_SUBLANES, sem).start()
            return carry

        lax.fori_loop(0, count, zero_row, 0)
        n_zero = n_zero + count

    lax.fori_loop(0, tile, drain, 0, unroll=DMA_UNROLL)
    lax.fori_loop(0, n_zero, lambda r, carry: (_tile_copy(zero_sc, 0, xs_ref, 0, sem).wait(), carry)[1], 0)
    for j in range(experts_per_step):
        b = padinfo_ref[2, 0] + step * experts_per_step + j

        @pl.when(b < n_blocks)
        def _():
            block_copy(b).wait()


def _expert_kernel(blk_ref, nused_ref, xs_ref, wgu_ref, wd_ref, ys_ref, ring, sem):
    b = pl.program_id(0)
    n_blocks = pl.num_programs(0)
    rows = MOE_ROWS * ROW_SUBLANES

    def fetch(i):
        return pltpu.make_async_copy(xs_ref.at[pl.ds(pl.multiple_of(i * rows, rows), rows)],
                                     ring.at[lax.rem(i, XS_RING)], sem.at[lax.rem(i, XS_RING)])

    @pl.when(b == 0)
    def _():
        for i in range(XS_RING - 1):
            @pl.when(i < n_blocks)
            def _():
                fetch(jnp.int32(i)).start()

    @pl.when(b + XS_RING - 1 < n_blocks)
    def _():
        fetch(b + XS_RING - 1).start()

    fetch(b).wait()
    x_ring = ring.at[lax.rem(b, XS_RING)]

    @pl.when(b < nused_ref[0])
    def _():
        x = _from_row_tiles(x_ring, MOE_ROWS).astype(BF16)
        gu = jnp.dot(x, wgu_ref[0, 0].astype(BF16), preferred_element_type=F32)
        a, u = gu[:, :D_EXPERT], gu[:, D_EXPERT:]
        hidden = (a * jax.nn.sigmoid(a) * u).astype(BF16)
        _to_row_tiles(ys_ref, jnp.dot(hidden, wd_ref[0, 0].astype(BF16), preferred_element_type=F32))

    @pl.when(b >= nused_ref[0])
    def _():
        ys_ref[...] = jnp.zeros(ys_ref.shape, F32)


def _combine_kernel(dest_ref, dest_next_ref, ys_ref, x_ref, w_ref, g_ref, y_ref, buf, sem, *, tile, final_norm):
    step = pl.program_id(0)
    slot = lax.rem(step, 2)

    def issue(d_ref, slot_):
        def body(t, carry):
            _tile_copy(ys_ref, d_ref[0, 0, t], buf.at[slot_, 0], t * ROW_SUBLANES, sem.at[slot_]).start()
            _tile_copy(ys_ref, d_ref[0, 0, tile + t], buf.at[slot_, 1], t * ROW_SUBLANES,
                       sem.at[slot_]).start(priority=1)
            return carry
        lax.fori_loop(0, tile, body, 0, unroll=DMA_UNROLL)

    @pl.when(step == 0)
    def _():
        issue(dest_ref, 0)

    @pl.when(step + 1 < pl.num_programs(0))
    def _():
        issue(dest_next_ref, 1 - slot)

    def drain(t, carry):
        _tile_copy(ys_ref, 0, buf.at[slot, 0], 0, sem.at[slot]).wait()
        _tile_copy(ys_ref, 0, buf.at[slot, 1], 0, sem.at[slot]).wait()
        return carry

    lax.fori_loop(0, tile, drain, 0, unroll=DMA_UNROLL)
    w = w_ref[...]
    y = x_ref[...] + (w[:, 0:1] * _from_row_tiles(buf.at[slot, 0], tile)
                      + w[:, 1:2] * _from_row_tiles(buf.at[slot, 1], tile))
    if final_norm:
        y = _rms_rows(y, g_ref[...])
    y_ref[...] = y


def _moe(x, g_ffn, wrT, br, w_gate_up, w_down, layer, g_final, final_norm):
    n_tok = x.shape[0]
    tile = min(TOK_TILE, n_tok)
    n_tiles = n_tok // tile
    n_blocks = (2 * n_tok + MOE_ROWS - 1) // MOE_ROWS + N_EXPERTS
    n_blocks_padded = (n_blocks + 127) // 128 * 128
    n_rows = n_blocks * MOE_ROWS

    ri, rw, cnt = pl.pallas_call(
        _router_kernel,
        grid=(n_tiles,),
        in_specs=[
            pl.BlockSpec((tile, D_MODEL), lambda i: (i, 0)),
            _const_spec((1, D_MODEL)),
            _const_spec((2, ROUTER_LANES, D_MODEL)),
            _const_spec((ROUTER_LANES, 1)),
        ],
        out_specs=[
            pl.BlockSpec((8, tile), lambda i: (0, i)),
            pl.BlockSpec((8, tile), lambda i: (0, i)),
            _const_spec((N_EXPERTS, 128)),
        ],
        out_shape=[
            jax.ShapeDtypeStruct((8, n_tok), jnp.int32),
            jax.ShapeDtypeStruct((8, n_tok), F32),
            jax.ShapeDtypeStruct((N_EXPERTS, 128), F32),
        ],
        scratch_shapes=[pltpu.VMEM((N_EXPERTS, 1), F32), pltpu.VMEM((tile, tile), BF16)],
        compiler_params=_cparams(1),
        name="moe_router",
    )(x, g_ffn, wrT, br)

    dest, blk, nused, padinfo = pl.pallas_call(
        functools.partial(_dest_kernel, n_blocks_padded=n_blocks_padded),
        grid=(n_tiles,),
        in_specs=[_const_spec((N_EXPERTS, 128)), pl.BlockSpec((8, tile), lambda i: (0, i))],
        out_specs=[
            pl.BlockSpec((8, tile), lambda i: (0, i)),
            _const_spec((8, n_blocks_padded)),
            _const_spec((8, 128)),
            _const_spec((8, 128)),
        ],
        out_shape=[
            jax.ShapeDtypeStruct((8, n_tok), jnp.int32),
            jax.ShapeDtypeStruct((8, n_blocks_padded), jnp.int32),
            jax.ShapeDtypeStruct((8, 128), jnp.int32),
            jax.ShapeDtypeStruct((8, 128), jnp.int32),
        ],
        compiler_params=_cparams(1),
        name="moe_dest",
    )(cnt, ri)

    dest_tiles = dest[0:2].reshape(2, n_tiles, tile).transpose(1, 0, 2).reshape(n_tiles, 1, 2 * tile)
    dest_spec = pl.BlockSpec((1, 1, 2 * tile), lambda i: (i, 0, 0), memory_space=pltpu.SMEM)
    any_spec = pl.BlockSpec(memory_space=pl.ANY)

    xs = pl.pallas_call(
        functools.partial(_dispatch_kernel, tile=tile, experts_per_step=pl.cdiv(N_EXPERTS, n_tiles),
                          n_blocks=n_blocks),
        grid=(n_tiles,),
        in_specs=[dest_spec, pl.BlockSpec((8, 128), lambda i: (0, 0), memory_space=pltpu.SMEM),
                  pl.BlockSpec((tile, D_MODEL), lambda i: (i, 0)), _const_spec((1, D_MODEL))],
        out_specs=any_spec,
        out_shape=jax.ShapeDtypeStruct((n_rows * ROW_SUBLANES, 128), F32),
        scratch_shapes=[pltpu.VMEM((tile * ROW_SUBLANES, 128), F32),
                        pltpu.VMEM((MOE_ROWS * ROW_SUBLANES, 128), F32),
                        pltpu.SemaphoreType.DMA(()), pltpu.SemaphoreType.DMA(())],
        compiler_params=_cparams(1),
        name="moe_dispatch",
    )(dest_tiles, padinfo, x, g_ffn)

    blk_rows = MOE_ROWS * ROW_SUBLANES
    ys = pl.pallas_call(
        _expert_kernel,
        grid_spec=pltpu.PrefetchScalarGridSpec(
            num_scalar_prefetch=2,
            grid=(n_blocks,),
            in_specs=[
                pl.BlockSpec(memory_space=pl.ANY),
                pl.BlockSpec((1, 1, D_MODEL, 2 * D_EXPERT), lambda b, blk, nu: (layer, blk[b], 0, 0)),
                pl.BlockSpec((1, 1, D_EXPERT, D_MODEL), lambda b, blk, nu: (layer, blk[b], 0, 0)),
            ],
            out_specs=pl.BlockSpec((blk_rows, 128), lambda b, blk, nu: (b, 0)),
            scratch_shapes=[pltpu.VMEM((XS_RING, blk_rows, 128), F32), pltpu.SemaphoreType.DMA((XS_RING,))],
        ),
        out_shape=jax.ShapeDtypeStruct((n_rows * ROW_SUBLANES, 128), F32),
        compiler_params=_cparams(1),
        name="moe_experts",
    )(blk[0, :n_blocks], nused[0, :1], xs, w_gate_up, w_down)

    ctile = min(COMBINE_TILE, n_tok)
    c_tiles = n_tok // ctile
    dest_c = dest[0:2].reshape(2, c_tiles, ctile).transpose(1, 0, 2).reshape(c_tiles, 1, 2 * ctile)
    return pl.pallas_call(
        functools.partial(_combine_kernel, tile=ctile, final_norm=final_norm),
        grid=(c_tiles,),
        in_specs=[
            pl.BlockSpec((1, 1, 2 * ctile), lambda i: (i, 0, 0), memory_space=pltpu.SMEM),
            pl.BlockSpec((1, 1, 2 * ctile), lambda i: (jnp.minimum(i + 1, c_tiles - 1), 0, 0),
                         memory_space=pltpu.SMEM),
            any_spec,
            pl.BlockSpec((ctile, D_MODEL), lambda i: (i, 0)),
            pl.BlockSpec((ctile, 2), lambda i: (i, 0)),
            _const_spec((1, D_MODEL)),
        ],
        out_specs=pl.BlockSpec((ctile, D_MODEL), lambda i: (i, 0)),
        out_shape=jax.ShapeDtypeStruct((n_tok, D_MODEL), F32),
        scratch_shapes=[pltpu.VMEM((2, 2, ctile * ROW_SUBLANES, 128), F32), pltpu.SemaphoreType.DMA((2,))],
        compiler_params=_cparams(1),
        name="moe_combine",
    )(dest_c, dest_c, ys, x, rw[0:2].T, g_final)


def _rope_tables(n_tokens):
    rows = n_tokens // GRID_W
    row_idx = jnp.repeat(jnp.arange(rows, dtype=F32), GRID_W)
    col_idx = jnp.tile(jnp.arange(GRID_W, dtype=F32), rows)

    def angles(rot_dim):
        n_freq = rot_dim // 4
        inv_freq = ROPE_THETA ** (-jnp.arange(n_freq, dtype=F32) / n_freq)
        return jnp.concatenate([row_idx[:, None] * inv_freq, col_idx[:, None] * inv_freq], axis=-1)

    ag, am = angles(GQA_HD), angles(MLA_ROPE)
    cg, sg, cm, sm = jnp.cos(ag), jnp.sin(ag), jnp.cos(am), jnp.sin(am)
    hr = MLA_ROPE // 2
    zl = jnp.zeros((n_tokens, MLA_NOPE), F32)
    zr = jnp.zeros((n_tokens, HEAD_PAD - MLA_NOPE - MLA_ROPE), F32)
    del hr
    return dict(
        cT_g=cg.T, sT_g=sg.T,
        cF_g=jnp.concatenate([cg, cg], axis=1), sF_g=jnp.concatenate([-sg, sg], axis=1),
        cT_m=cm.T, sT_m=sm.T,
        cP_m=jnp.concatenate([zl, cm, cm, zr], axis=1), sP_m=jnp.concatenate([zl, -sm, sm, zr], axis=1),
    )


def _hi_lo(w):
    hi = w.astype(BF16)
    return jnp.stack([hi, (w - hi.astype(F32)).astype(BF16)])


def _layer_params(l, norm_mix, norm_mem, w_in, gqa_q_norm, gqa_k_norm, mla_q_a_norm, mla_kv_a_norm, mla_w_qb,
                  mla_w_kvb, mem_w_kv, w_out, norm_ffn, w_group, b_group, w_expert, b_expert):
    splits = []
    off = 0
    for width in (GQA_HEADS * GQA_HD, GQA_KV_HEADS * GQA_HD, GQA_KV_HEADS * GQA_HD, MLA_Q_LORA, MLA_KV_LORA,
                  MLA_ROPE, MEM_HEADS * MEM_HD, N_BRANCH * D_MODEL):
        splits.append(w_in[l][:, off:off + width])
        off += width
    wq_g, wk_g, wv_g, wq_a, wkv_a, wk_r, wq_c, w_gate = splits
    hr = MLA_ROPE // 2
    pad_l = jnp.zeros((D_MODEL, MLA_NOPE), F32)
    pad_r = jnp.zeros((D_MODEL, HEAD_PAD - MLA_NOPE - MLA_ROPE), F32)
    wk_r_swapped = jnp.concatenate([wk_r[:, hr:], wk_r[:, :hr]], axis=1)
    wkr2 = jnp.concatenate([pad_l, wk_r, pad_r, pad_l, wk_r_swapped, pad_r], axis=1)
    qk = MLA_NOPE + MLA_ROPE
    wqb = jnp.pad(mla_w_qb[l].reshape(MLA_Q_LORA, MLA_HEADS, qk), ((0, 0), (0, 0), (0, HEAD_PAD - qk)))
    wkvb = mla_w_kvb[l].reshape(MLA_KV_LORA, MLA_HEADS, MLA_NOPE + MLA_V)
    wkn = jnp.pad(wkvb[:, :, :MLA_NOPE], ((0, 0), (0, 0), (0, HEAD_PAD - MLA_NOPE)))
    wv_m = wkvb[:, :, MLA_NOPE:]
    n_router = N_GROUPS + N_EXPERTS
    return dict(
        g_mix=norm_mix[l][None, :], g_mem=norm_mem[l][None, :], g_ffn=norm_ffn[l][None, :],
        wqT_g=wq_g.T.astype(BF16), wk_g=wk_g.astype(BF16), wvT_g=wv_g.T.astype(BF16),
        gq=gqa_q_norm[l][:, None], gk=gqa_k_norm[l][None, :],
        wqaT=wq_a.T.astype(BF16), gqa=mla_q_a_norm[l][:, None],
        wqbT=wqb.reshape(MLA_Q_LORA, MLA_HEADS * HEAD_PAD).T.astype(BF16),
        wkva=wkv_a.astype(BF16), gkva=mla_kv_a_norm[l][None, :],
        wkn=wkn.reshape(MLA_KV_LORA, MLA_HEADS * HEAD_PAD).astype(BF16), wkr2=wkr2.astype(BF16),
        wvT_m=wv_m.reshape(MLA_KV_LORA, MLA_HEADS * MLA_V).T.astype(BF16),
        wqcT=wq_c.T.astype(BF16), wgT=w_gate.T.astype(BF16),
        wmk=mem_w_kv[l][:, :MEM_HEADS * MEM_HD].astype(BF16),
        wmvT=mem_w_kv[l][:, MEM_HEADS * MEM_HD:].T.astype(BF16),
        w_out=w_out[l].astype(BF16),
        wrT=_hi_lo(jnp.concatenate([w_group[l].T, w_expert[l].T,
                                    jnp.zeros((ROUTER_LANES - n_router, D_MODEL), F32)], axis=0)),
        br=jnp.concatenate([b_group[l], b_expert[l], jnp.zeros((ROUTER_LANES - n_router,), F32)])[:, None],
    )


def _mixer(x, mem, p, tabs):
    b, s, _ = x.shape
    tile = min(TOK_TILE, s)
    n_t = s // tile
    grid = (b, n_t)
    x_spec = pl.BlockSpec((1, tile, D_MODEL), lambda bi, i: (bi, i, 0))
    colT = lambda rows: pl.BlockSpec((rows, tile), lambda bi, i: (0, i))
    rowF = pl.BlockSpec((tile, HEAD_PAD), lambda bi, i: (i, 0))
    headT = lambda heads: pl.BlockSpec((1, heads, 1, HEAD_PAD, tile), lambda bi, i: (bi, 0, i, 0, 0))
    keyF = lambda heads: pl.BlockSpec((1, heads, tile, HEAD_PAD), lambda bi, i: (bi, 0, i, 0))
    valT = lambda heads: pl.BlockSpec((1, heads, 1, V_ROWS, tile), lambda bi, i: (bi, 0, i, 0, 0))

    gqa_in = [(p["wqT_g"], _const_spec(p["wqT_g"].shape)), (p["wk_g"], _const_spec(p["wk_g"].shape)),
              (p["wvT_g"], _const_spec(p["wvT_g"].shape)), (p["gq"], _const_spec((GQA_HD, 1))),
              (p["gk"], _const_spec((1, GQA_HD))), (tabs["cT_g"], colT(GQA_HD // 2)),
              (tabs["sT_g"], colT(GQA_HD // 2)), (tabs["cF_g"], rowF), (tabs["sF_g"], rowF)]
    mla_in = [(p["wqaT"], _const_spec(p["wqaT"].shape)), (p["gqa"], _const_spec((MLA_Q_LORA, 1))),
              (p["wqbT"], _const_spec(p["wqbT"].shape)), (p["wkva"], _const_spec(p["wkva"].shape)),
              (p["gkva"], _const_spec((1, MLA_KV_LORA))), (p["wkn"], _const_spec(p["wkn"].shape)),
              (p["wkr2"], _const_spec(p["wkr2"].shape)), (p["wvT_m"], _const_spec(p["wvT_m"].shape)),
              (tabs["cT_m"], colT(MLA_ROPE // 2)), (tabs["sT_m"], colT(MLA_ROPE // 2)),
              (tabs["cP_m"], rowF), (tabs["sP_m"], rowF)]
    assert len(gqa_in) == 9 and len(mla_in) == 12
    qkv_shapes = lambda heads, kv_heads: [jax.ShapeDtypeStruct((b, heads, n_t, HEAD_PAD, tile), BF16),
                                          jax.ShapeDtypeStruct((b, kv_heads, s, HEAD_PAD), BF16),
                                          jax.ShapeDtypeStruct((b, kv_heads, n_t, V_ROWS, tile), BF16)]
    qg, kg, vg, qm, km, vm = pl.pallas_call(
        _proj_attn_kernel,
        grid=grid,
        in_specs=[x_spec, _const_spec((1, D_MODEL))] + [spec for _, spec in gqa_in + mla_in],
        out_specs=[headT(GQA_HEADS), keyF(GQA_KV_HEADS), valT(GQA_KV_HEADS),
                   headT(MLA_HEADS), keyF(MLA_HEADS), valT(MLA_HEADS)],
        out_shape=qkv_shapes(GQA_HEADS, GQA_KV_HEADS) + qkv_shapes(MLA_HEADS, MLA_HEADS),
        compiler_params=_cparams(2),
        name="proj_attn",
    )(x, p["g_mix"], *[arr for arr, _ in gqa_in + mla_in])

    n_mem = mem.shape[1]
    mk, mvT = pl.pallas_call(
        _mem_kv_kernel,
        grid=(b,),
        in_specs=[pl.BlockSpec((1, n_mem, D_MODEL), lambda bi: (bi, 0, 0)), _const_spec((1, D_MODEL)),
                  _const_spec(p["wmk"].shape), _const_spec(p["wmvT"].shape)],
        out_specs=[pl.BlockSpec((1, n_mem, D_MODEL), lambda bi: (bi, 0, 0)),
                   pl.BlockSpec((1, D_MODEL, n_mem), lambda bi: (bi, 0, 0))],
        out_shape=[jax.ShapeDtypeStruct((b, n_mem, D_MODEL), BF16), jax.ShapeDtypeStruct((b, D_MODEL, n_mem), BF16)],
        compiler_params=_cparams(1),
        name="mem_kv",
    )(mem, p["g_mem"], p["wmk"], p["wmvT"])

    oa = _attention(qg, kg, vg, GQA_HEADS // GQA_KV_HEADS)
    ob = _attention(qm, km, vm, 1)
    return pl.pallas_call(
        _mixer_tail_kernel,
        grid=grid,
        in_specs=[x_spec, _const_spec((1, D_MODEL)), headT(GQA_HEADS), headT(MLA_HEADS),
                  pl.BlockSpec((1, n_mem, D_MODEL), lambda bi, i: (bi, 0, 0)),
                  pl.BlockSpec((1, D_MODEL, n_mem), lambda bi, i: (bi, 0, 0)),
                  _const_spec(p["wqcT"].shape), _const_spec(p["wgT"].shape), _const_spec((D_MODEL, D_MODEL))],
        out_specs=x_spec,
        out_shape=jax.ShapeDtypeStruct(x.shape, F32),
        compiler_params=_cparams(2),
        name="mixer_tail",
    )(x, p["g_mix"], oa, ob, mk, mvT, p["wqcT"], p["wgT"], p["w_out"])


def _trunk(x, mem, layers, w_gate_up, w_down, g_final):
    b, s, d = x.shape
    tabs = _rope_tables(s)
    for l, p in enumerate(layers):
        x = _mixer(x, mem, p, tabs)
        x = _moe(x.reshape(b * s, d), p["g_ffn"], p["wrT"], p["br"], w_gate_up, w_down, l, g_final,
                 final_norm=(l == len(layers) - 1)).reshape(b, s, d)
    return x


def kernel(x_prompt, x_sample, mem_prompt, mem_sample, norm_mix, norm_mem, w_in, gqa_q_norm, gqa_k_norm, mla_q_a_norm, mla_kv_a_norm, mla_w_qb, mla_w_kvb, mem_w_kv, w_out, norm_ffn, w_group, b_group, w_expert, b_expert, w_gate_up, w_down, norm_final):
    depth = w_in.shape[0]
    layers = [_layer_params(l, norm_mix, norm_mem, w_in, gqa_q_norm, gqa_k_norm, mla_q_a_norm, mla_kv_a_norm,
                            mla_w_qb, mla_w_kvb, mem_w_kv, w_out, norm_ffn, w_group, b_group, w_expert, b_expert)
              for l in range(depth)]
    g_final = norm_final[None, :]
    return (_trunk(x_prompt, mem_prompt, layers, w_gate_up, w_down, g_final),
            _trunk(x_sample, mem_sample, layers, w_gate_up, w_down, g_final))
```
